```python
import jax, jax.numpy as jnp
from jax import lax
import numpy as np

D_MODEL = 1024
BATCH = 8
SEQ = 2048
DEPTH = 2
DEC_BATCH = 32
DEC_SEQ = 8
PAST_LEN = 8192
PAGE_SIZE = 128

N_A_LAYERS = DEPTH // 2
N_B_LAYERS = DEPTH - N_A_LAYERS
RET_HEADS = 4
RET_DK = D_MODEL // RET_HEADS
RET_DV = 2 * RET_DK
RET_CHUNK = 128
ROPE_BASE = 10000.0
FOX_HEADS = 16
FOX_DH = D_MODEL // FOX_HEADS
Q_BLOCK = 128
NEG_INF = -1e30
N_EXPERTS = 64
TOPK = 8
N_GROUPS = 8
TOPK_GROUPS = 4
FF_EXPERT = D_MODEL // 4
FF_SHARED = FF_EXPERT
ROUTED_SCALE = 2.5
EXP_BLOCK = 128
DEEPNORM_ALPHA = (2.0 * DEPTH) ** 0.25
DEEPNORM_BETA = (8.0 * DEPTH) ** -0.25
LN_EPS = 1e-5
GN_EPS = 1e-5
F32 = jnp.float32

kernel_name = "yoco_retention_fox_moe_step"


def _layer_norm(x, g, b):
    x32 = x.astype(F32)
    mu = jnp.mean(x32, axis=-1, keepdims=True)
    var = jnp.mean(jnp.square(x32 - mu), axis=-1, keepdims=True)
    return ((x32 - mu) * lax.rsqrt(var + LN_EPS) * g.astype(F32) + b.astype(F32)).astype(x.dtype)


def _rotary(x, pos):
    half = x.shape[-1] // 2
    inv_freq = ROPE_BASE ** (-jnp.arange(half, dtype=F32) / half)
    ang = pos.astype(F32)[:, None] * inv_freq[None, :]
    cos = jnp.cos(ang)[None, :, None, :]
    sin = jnp.sin(ang)[None, :, None, :]
    x1 = x[..., :half].astype(F32)
    x2 = x[..., half:].astype(F32)
    return jnp.concatenate([x1 * cos - x2 * sin, x2 * cos + x1 * sin], axis=-1)


def _head_groupnorm(o):
    mu = jnp.mean(o, axis=-1, keepdims=True)
    var = jnp.mean(jnp.square(o - mu), axis=-1, keepdims=True)
    return (o - mu) * lax.rsqrt(var + GN_EPS)


def _retention_chunkwise(q, k, v, state):
    B, L, H, _ = q.shape
    C = min(RET_CHUNK, L)
    NC = L // C
    log_g = jnp.log1p(-(2.0 ** (-5.0 - jnp.arange(H, dtype=F32))))
    idx = jnp.arange(C, dtype=F32)
    rel = idx[:, None] - idx[None, :]
    decay_in = jnp.where(rel >= 0, jnp.exp(log_g[:, None, None] * jnp.maximum(rel, 0.0)), 0.0)
    decay_q = jnp.exp(log_g[None, :] * (idx[:, None] + 1.0))[None, :, :, None]
    decay_k = jnp.exp(log_g[None, :] * (C - 1.0 - idx[:, None]))[None, :, :, None]
    decay_chunk = jnp.exp(log_g * C)[None, :, None, None]

    def to_chunks(t):
        return t.reshape(B, NC, C, H, t.shape[-1]).swapaxes(0, 1)

    def step(S, inp):
        qc, kc, vc = inp
        scores = jnp.einsum('bihd,bjhd->bhij', qc, kc) * decay_in
        o = (jnp.einsum('bhij,bjhv->bihv', scores, vc)
             + jnp.einsum('bihk,bhkv->bihv', qc, S) * decay_q)
        S = S * decay_chunk + jnp.einsum('bjhk,bjhv->bhkv', kc * decay_k, vc)
        return S, o

    S, o = lax.scan(step, state, (to_chunks(q), to_chunks(k), to_chunks(v)))
    return o.swapaxes(0, 1).reshape(B, L, H, v.shape[-1]), S


def retention_mixer(x, state, pos_offset, w_in, w_out):
    B, L, _ = x.shape
    hq = RET_HEADS * RET_DK
    hv = RET_HEADS * RET_DV
    q, k, v, g = jnp.split(x @ w_in, [hq, 2 * hq, 2 * hq + hv], axis=-1)
    pos = pos_offset + jnp.arange(L)
    q = _rotary(q.reshape(B, L, RET_HEADS, RET_DK), pos)
    k = _rotary(k.reshape(B, L, RET_HEADS, RET_DK), pos) * (RET_DK ** -0.5)
    v = v.reshape(B, L, RET_HEADS, RET_DV).astype(F32)
    o, new_state = _retention_chunkwise(q, k, v, state.astype(F32))
    o = _head_groupnorm(o).reshape(B, L, hv)
    out = (jax.nn.silu(g.astype(F32)) * o).astype(x.dtype) @ w_out
    return out, new_state.astype(x.dtype)


def shared_kv(h, w_kvf, b_f):
    B, L, _ = h.shape
    fd = FOX_HEADS * FOX_DH
    p = h @ w_kvf
    k = p[..., :fd].reshape(B, L, FOX_HEADS, FOX_DH)
    v = p[..., fd:2 * fd].reshape(B, L, FOX_HEADS, FOX_DH)
    logf = jax.nn.log_sigmoid((p[..., 2 * fd:] + b_f).astype(F32))
    return k, v, logf.astype(h.dtype)


def forgetting_attention(q, k, v, logf, q_offset):
    B, Lq, H, Dh = q.shape
    Lk = k.shape[1]
    cum = jnp.cumsum(logf.astype(F32), axis=1).transpose(0, 2, 1)
    blk = min(Q_BLOCK, Lq)
    nb = Lq // blk
    q_blocks = q.reshape(B, nb, blk, H, Dh).swapaxes(0, 1)
    k_pos = jnp.arange(Lk)
    scale = Dh ** -0.5

    def one_block(args):
        qb, start = args
        q_pos = q_offset + start + jnp.arange(blk)
        cum_q = lax.dynamic_slice_in_dim(cum, q_offset + start, blk, axis=2)
        logits = (jnp.einsum('bqhd,bkhd->bhqk', qb, k, preferred_element_type=F32) * scale
                  + cum_q[..., None] - cum[:, :, None, :])
        logits = jnp.where(k_pos[None, :] <= q_pos[:, None], logits, NEG_INF)
        p = jax.nn.softmax(logits, axis=-1)
        return jnp.einsum('bhqk,bkhd->bqhd', p.astype(v.dtype), v)

    out = lax.map(one_block, (q_blocks, jnp.arange(nb) * blk))
    return out.swapaxes(0, 1).reshape(B, Lq, H, Dh)


def fox_mixer(x, k, v, logf, q_offset, w_q, w_out):
    B, L, _ = x.shape
    q = (x @ w_q).reshape(B, L, FOX_HEADS, FOX_DH)
    o = forgetting_attention(q, k, v, logf, q_offset)
    return o.reshape(B, L, FOX_HEADS * FOX_DH) @ w_out


def _gather_pages(cache, page_table):
    pages = cache[page_table]
    db, n_pages, page = pages.shape[:3]
    return pages.reshape((db, n_pages * page) + pages.shape[3:])


def _swiglu(x, wg, wu, wd):
    return (jax.nn.silu(x @ wg) * (x @ wu)) @ wd


def _routed_experts(xt, idx, gate_w, w_gate, w_up, w_down):
    T, D = xt.shape
    A = T * TOPK
    e_flat = idx.reshape(A)
    order = jnp.argsort(e_flat)
    e_sorted = e_flat[order]
    counts = jnp.bincount(e_flat, length=N_EXPERTS)
    blocks_per_e = (counts + EXP_BLOCK - 1) // EXP_BLOCK
    blk_end = jnp.cumsum(blocks_per_e)
    blk_start = blk_end - blocks_per_e
    grp_start = jnp.cumsum(counts) - counts
    slot = blk_start[e_sorted] * EXP_BLOCK + (jnp.arange(A) - grp_start[e_sorted])
    nb = -(-A // EXP_BLOCK) + N_EXPERTS
    n_slots = nb * EXP_BLOCK
    slot_assign = jnp.full((n_slots,), A, jnp.int32).at[slot].set(order.astype(jnp.int32))
    slot_token = jnp.where(slot_assign < A, slot_assign // TOPK, T)
    block_expert = jnp.minimum(jnp.searchsorted(blk_end, jnp.arange(nb), side='right'), N_EXPERTS - 1)
    x_pad = jnp.concatenate([xt, jnp.zeros((1, D), xt.dtype)], axis=0)
    xb = x_pad[slot_token].reshape(nb, EXP_BLOCK, D)

    def one_block(args):
        xi, e = args
        return _swiglu(xi, w_gate[e], w_up[e], w_down[e])

    yb = lax.map(one_block, (xb, block_expert)).reshape(n_slots, D)
    y_assign = jnp.zeros((A + 1, D), yb.dtype).at[slot_assign].set(yb)[:A]
    return jnp.einsum('tk,tkd->td', gate_w.astype(yb.dtype), y_assign.reshape(T, TOPK, D))


def moe_ffn(x, w_router, b_router, w_gate, w_up, w_down, ws_gate, ws_up, ws_down):
    B, L, D = x.shape
    xt = x.reshape(B * L, D)
    T = xt.shape[0]
    scores = jax.nn.sigmoid((xt @ w_router).astype(F32))
    biased = scores + b_router.astype(F32)
    grp = biased.reshape(T, N_GROUPS, N_EXPERTS // N_GROUPS)
    grp_score = lax.top_k(grp, 2)[0].sum(-1)
    _, top_groups = lax.top_k(grp_score, TOPK_GROUPS)
    group_mask = jnp.any(top_groups[..., None] == jnp.arange(N_GROUPS), axis=1)
    expert_mask = jnp.repeat(group_mask, N_EXPERTS // N_GROUPS, axis=1)
    _, idx = lax.top_k(jnp.where(expert_mask, biased, -jnp.inf), TOPK)
    w = jnp.take_along_axis(scores, idx, axis=-1)
    w = w / jnp.sum(w, axis=-1, keepdims=True) * ROUTED_SCALE
    y = _swiglu(xt, ws_gate, ws_up, ws_down) + _routed_experts(xt, idx, w, w_gate, w_up, w_down)
    return y.reshape(B, L, D)


def _w(key, shape, fan_in, scale=1.0):
    return jax.random.normal(key, shape, jnp.float32) * (scale * fan_in ** -0.5)


def setup_inputs(seed: int = 0) -> dict:
    key = jax.random.key(seed)
    ks = jax.random.split(key, 32)
    n_pages = PAST_LEN // PAGE_SIZE
    n_used = DEC_BATCH * n_pages
    n_pool = n_used + (n_used + 3) // 4
    hq = RET_HEADS * RET_DK
    hv = RET_HEADS * RET_DV
    fd = FOX_HEADS * FOX_DH
    x_prompt = jax.random.normal(ks[0], (BATCH, SEQ, D_MODEL), jnp.float32)
    x_sample = jax.random.normal(ks[1], (DEC_BATCH, DEC_SEQ, D_MODEL), jnp.float32)
    state_ret = 0.3 * jax.random.normal(ks[2], (N_A_LAYERS, DEC_BATCH, RET_HEADS, RET_DK, RET_DV), jnp.float32)
    cache_k = jax.random.normal(ks[3], (n_pool, PAGE_SIZE, FOX_HEADS, FOX_DH), jnp.float32)
    cache_v = jax.random.normal(ks[4], (n_pool, PAGE_SIZE, FOX_HEADS, FOX_DH), jnp.float32)
    cache_logf = jax.nn.log_sigmoid(2.0 + jax.random.normal(ks[5], (n_pool, PAGE_SIZE, FOX_HEADS), jnp.float32))
    page_table = jax.random.permutation(ks[6], n_pool)[:n_used].reshape(DEC_BATCH, n_pages).astype(jnp.int32)
    ret_w_in = jnp.concatenate([
        _w(ks[7], (N_A_LAYERS, D_MODEL, hq), D_MODEL),
        _w(ks[8], (N_A_LAYERS, D_MODEL, hq), D_MODEL),
        _w(ks[9], (N_A_LAYERS, D_MODEL, hv), D_MODEL, DEEPNORM_BETA),
        _w(ks[10], (N_A_LAYERS, D_MODEL, hv), D_MODEL)], axis=-1)
    ret_w_out = _w(ks[11], (N_A_LAYERS, hv, D_MODEL), hv, DEEPNORM_BETA)
    fox_w_kvf = jnp.concatenate([
        _w(ks[12], (D_MODEL, fd), D_MODEL),
        _w(ks[13], (D_MODEL, fd), D_MODEL, DEEPNORM_BETA),
        _w(ks[14], (D_MODEL, FOX_HEADS), D_MODEL)], axis=-1)
    fox_b_f = 2.0 + 0.1 * jax.random.normal(ks[15], (FOX_HEADS,), jnp.float32)
    fox_w_q = _w(ks[16], (N_B_LAYERS, D_MODEL, fd), D_MODEL)
    fox_w_out = _w(ks[17], (N_B_LAYERS, fd, D_MODEL), fd, DEEPNORM_BETA)
    ln_g = 1.0 + 0.02 * jax.random.normal(ks[18], (DEPTH, 2, D_MODEL), jnp.float32)
    ln_b = 0.02 * jax.random.normal(ks[19], (DEPTH, 2, D_MODEL), jnp.float32)
    moe_w_router = _w(ks[20], (DEPTH, D_MODEL, N_EXPERTS), D_MODEL)
    moe_b_router = 0.01 * jax.random.normal(ks[21], (DEPTH, N_EXPERTS), jnp.float32)
    moe_w_gate = _w(ks[22], (DEPTH, N_EXPERTS, D_MODEL, FF_EXPERT), D_MODEL)
    moe_w_up = _w(ks[23], (DEPTH, N_EXPERTS, D_MODEL, FF_EXPERT), D_MODEL, DEEPNORM_BETA)
    moe_w_down = _w(ks[24], (DEPTH, N_EXPERTS, FF_EXPERT, D_MODEL), FF_EXPERT, DEEPNORM_BETA)
    moe_ws_gate = _w(ks[25], (DEPTH, D_MODEL, FF_SHARED), D_MODEL)
    moe_ws_up = _w(ks[26], (DEPTH, D_MODEL, FF_SHARED), D_MODEL, DEEPNORM_BETA)
    moe_ws_down = _w(ks[27], (DEPTH, FF_SHARED, D_MODEL), FF_SHARED, DEEPNORM_BETA)
    return {
        'x_prompt': x_prompt, 'x_sample': x_sample, 'state_ret': state_ret,
        'cache_k': cache_k, 'cache_v': cache_v, 'cache_logf': cache_logf, 'page_table': page_table,
        'ret_w_in': ret_w_in, 'ret_w_out': ret_w_out,
        'fox_w_kvf': fox_w_kvf, 'fox_b_f': fox_b_f, 'fox_w_q': fox_w_q, 'fox_w_out': fox_w_out,
        'ln_g': ln_g, 'ln_b': ln_b,
        'moe_w_router': moe_w_router, 'moe_b_router': moe_b_router,
        'moe_w_gate': moe_w_gate, 'moe_w_up': moe_w_up, 'moe_w_down': moe_w_down,
        'moe_ws_gate': moe_ws_gate, 'moe_ws_up': moe_ws_up, 'moe_ws_down': moe_ws_down,
    }


def reference(x_prompt, x_sample, state_ret, cache_k, cache_v, cache_logf, page_table,
              ret_w_in, ret_w_out, fox_w_kvf, fox_b_f, fox_w_q, fox_w_out, ln_g, ln_b,
              moe_w_router, moe_b_router, moe_w_gate, moe_w_up, moe_w_down,
              moe_ws_gate, moe_ws_up, moe_ws_down):
    xp, xs = x_prompt, x_sample
    ret_p, ret_s = [], []
    for layer in range(DEPTH):
        if layer < N_A_LAYERS:
            a = layer
            s0 = jnp.zeros((xp.shape[0], RET_HEADS, RET_DK, RET_DV), F32)
            mp, sp = retention_mixer(xp, s0, 0, ret_w_in[a], ret_w_out[a])
            ms, ss = retention_mixer(xs, state_ret[a], PAST_LEN, ret_w_in[a], ret_w_out[a])
            ret_p.append(sp)
            ret_s.append(ss)
        else:
            b = layer - N_A_LAYERS
            mp = fox_mixer(xp, k_p, v_p, lf_p, 0, fox_w_q[b], fox_w_out[b])
            ms = fox_mixer(xs, k_all, v_all, lf_all, PAST_LEN, fox_w_q[b], fox_w_out[b])
        xp = _layer_norm(DEEPNORM_ALPHA * xp + mp, ln_g[layer, 0], ln_b[layer, 0])
        xs = _layer_norm(DEEPNORM_ALPHA * xs + ms, ln_g[layer, 0], ln_b[layer, 0])
        fp = moe_ffn(xp, moe_w_router[layer], moe_b_router[layer], moe_w_gate[layer], moe_w_up[layer],
                     moe_w_down[layer], moe_ws_gate[layer], moe_ws_up[layer], moe_ws_down[layer])
        fs = moe_ffn(xs, moe_w_router[layer], moe_b_router[layer], moe_w_gate[layer], moe_w_up[layer],
                     moe_w_down[layer], moe_ws_gate[layer], moe_ws_up[layer], moe_ws_down[layer])
        xp = _layer_norm(DEEPNORM_ALPHA * xp + fp, ln_g[layer, 1], ln_b[layer, 1])
        xs = _layer_norm(DEEPNORM_ALPHA * xs + fs, ln_g[layer, 1], ln_b[layer, 1])
        if layer == N_A_LAYERS - 1:
            k_p, v_p, lf_p = shared_kv(xp, fox_w_kvf, fox_b_f)
            k_s, v_s, lf_s = shared_kv(xs, fox_w_kvf, fox_b_f)
            k_all = jnp.concatenate([_gather_pages(cache_k, page_table).astype(k_s.dtype), k_s], axis=1)
            v_all = jnp.concatenate([_gather_pages(cache_v, page_table).astype(v_s.dtype), v_s], axis=1)
            lf_all = jnp.concatenate([_gather_pages(cache_logf, page_table).astype(lf_s.dtype), lf_s], axis=1)
    return (xp, xs, jnp.stack(ret_p), jnp.stack(ret_s), k_p, v_p, lf_p, k_s, v_s, lf_s)
```

```python
import functools

import jax
import jax.numpy as jnp
from jax import lax
from jax.experimental import pallas as pl
from jax.experimental.pallas import tpu as pltpu

F32 = jnp.float32
BF16 = jnp.bfloat16
I32 = jnp.int32

D_MODEL = 1024
DEPTH = 2
PAST_LEN = 8192
PAGE_SIZE = 128
N_A_LAYERS = DEPTH // 2
RET_HEADS = 4
RET_DK = D_MODEL // RET_HEADS
RET_DV = 2 * RET_DK
RET_CHUNK = 128
ROPE_BASE = 10000.0
FOX_HEADS = 16
FOX_DH = D_MODEL // FOX_HEADS
NEG_INF = -1e30
N_EXPERTS = 64
TOPK = 8
N_GROUPS = 8
GROUP_SIZE = N_EXPERTS // N_GROUPS
TOPK_GROUPS = 4
FF_EXPERT = D_MODEL // 4
ROUTED_SCALE = 2.5
DEEPNORM_ALPHA = (2.0 * DEPTH) ** 0.25
LN_EPS = 1e-5
GN_EPS = 1e-5

LANES = 128
SUBLANES = 8
ROW_VREGS = D_MODEL // LANES
EXP_BLOCK = 128
VMEM_LIMIT_BIG = 60 * 1024 * 1024
VMEM_LIMIT = 48 * 1024 * 1024

_NT = (((1,), (1,)), ((), ()))
_TN = (((0,), (0,)), ((), ()))


def _params(sem, vmem=VMEM_LIMIT):
    return pltpu.CompilerParams(dimension_semantics=sem, vmem_limit_bytes=vmem)


def _bdot(a, b):
    return jnp.dot(a, b, preferred_element_type=F32)


def _layer_norm_rows(z, g, b):
    mu = jnp.mean(z, axis=-1, keepdims=True)
    zc = z - mu
    var = jnp.mean(zc * zc, axis=-1, keepdims=True)
    return zc * lax.rsqrt(var + LN_EPS) * g + b


def _silu(x):
    return x * jax.nn.sigmoid(x)


def _split3(x):
    hi = x.astype(BF16)
    r1 = x - hi.astype(F32)
    mid = r1.astype(BF16)
    lo = (r1 - mid.astype(F32)).astype(BF16)
    return hi, mid, lo


def _mm_kernel(x_ref, w_ref, o_ref):
    o_ref[...] = _bdot(x_ref[...].astype(BF16), w_ref[...]).astype(o_ref.dtype)


def _matmul(x, w, col0, n, out_dtype, tm, tn):
    m, k = x.shape
    j0 = col0 // tn
    return pl.pallas_call(
        _mm_kernel,
        grid=(m // tm, n // tn),
        in_specs=[pl.BlockSpec((tm, k), lambda i, j: (i, 0)),
                  pl.BlockSpec((k, tn), lambda i, j: (0, j + j0))],
        out_specs=pl.BlockSpec((tm, tn), lambda i, j: (i, j)),
        out_shape=jax.ShapeDtypeStruct((m, n), out_dtype),
        compiler_params=_params(("parallel", "arbitrary")),
        name="proj",
    )(x, w)


def _mm_rope_kernel(x_ref, w_ref, cos_ref, sin_ref, o_ref, *, n_q_tiles, k_scale):
    j = pl.program_id(1)
    acc = _bdot(x_ref[...].astype(BF16), w_ref[...])
    half = RET_DK // 2
    x1 = acc[:, :half]
    x2 = acc[:, half:]
    c = cos_ref[...]
    s = sin_ref[...]
    sc = jnp.where(j < n_q_tiles, 1.0, k_scale).astype(F32)
    o_ref[:, :half] = (x1 * c - x2 * s) * sc
    o_ref[:, half:] = (x2 * c + x1 * s) * sc


def _qk_rope(x, w, cos_rows, sin_rows, tm):
    m, k = x.shape
    n = 2 * RET_HEADS * RET_DK
    nper = cos_rows.shape[0] // tm
    kern = functools.partial(_mm_rope_kernel, n_q_tiles=RET_HEADS, k_scale=RET_DK ** -0.5)
    return pl.pallas_call(
        kern,
        grid=(m // tm, n // RET_DK),
        in_specs=[pl.BlockSpec((tm, k), lambda i, j: (i, 0)),
                  pl.BlockSpec((k, RET_DK), lambda i, j: (0, j)),
                  pl.BlockSpec((tm, RET_DK // 2), lambda i, j: (i % nper, 0)),
                  pl.BlockSpec((tm, RET_DK // 2), lambda i, j: (i % nper, 0))],
        out_specs=pl.BlockSpec((tm, RET_DK), lambda i, j: (i, j)),
        out_shape=jax.ShapeDtypeStruct((m, n), F32),
        compiler_params=_params(("parallel", "arbitrary")),
        name="qk_rope",
    )(x, w, cos_rows, sin_rows)


def _mm_res_ln_kernel(a_ref, w_ref, x_ref, g_ref, b_ref, o_ref):
    mix = _bdot(a_ref[...].astype(BF16), w_ref[...])
    z = DEEPNORM_ALPHA * x_ref[...] + mix
    o_ref[...] = _layer_norm_rows(z, g_ref[...], b_ref[...])


def _mm_res_ln(a, w, x, g, b, tm):
    m, k = a.shape
    d = w.shape[1]
    return pl.pallas_call(
        _mm_res_ln_kernel,
        grid=(m // tm,),
        in_specs=[pl.BlockSpec((tm, k), lambda i: (i, 0)),
                  pl.BlockSpec((k, d), lambda i: (0, 0)),
                  pl.BlockSpec((tm, d), lambda i: (i, 0)),
                  pl.BlockSpec((1, d), lambda i: (0, 0)),
                  pl.BlockSpec((1, d), lambda i: (0, 0))],
        out_specs=pl.BlockSpec((tm, d), lambda i: (i, 0)),
        out_shape=jax.ShapeDtypeStruct((m, d), F32),
        compiler_params=_params(("parallel",)),
        name="mix_out_ln",
    )(a, w, x, g.reshape(1, d), b.reshape(1, d))


def _mm_logsig_kernel(x_ref, w_ref, b_ref, o_ref):
    z = _bdot(x_ref[...].astype(BF16), w_ref[...]) + b_ref[...]
    o_ref[...] = -(jnp.maximum(-z, 0.0) + jnp.log1p(jnp.exp(-jnp.abs(z))))


def _logf_proj(x, w_pad, b_pad, tm):
    m, k = x.shape
    return pl.pallas_call(
        _mm_logsig_kernel,
        grid=(m // tm,),
        in_specs=[pl.BlockSpec((tm, k), lambda i: (i, 0)),
                  pl.BlockSpec((k, LANES), lambda i: (0, 0)),
                  pl.BlockSpec((1, LANES), lambda i: (0, 0))],
        out_specs=pl.BlockSpec((tm, LANES), lambda i: (i, 0)),
        out_shape=jax.ShapeDtypeStruct((m, LANES), F32),
        compiler_params=_params(("parallel",)),
        name="logf_proj",
    )(x, w_pad, b_pad)


def _ret_kernel(q_ref, k_ref, v_ref, g_ref, s0_ref, din_ref, dq_ref, dk_ref, dc_ref,
                o_ref, sout_ref, state_ref):
    c = pl.program_id(2)

    @pl.when(c == 0)
    def _():
        state_ref[...] = s0_ref[...]

    q = q_ref[...].astype(BF16)
    k = k_ref[...]
    v = v_ref[...].astype(BF16)
    s_prev = state_ref[...]
    scores = lax.dot_general(q, k.astype(BF16), _NT, preferred_element_type=F32) * din_ref[...]
    o = _bdot(scores.astype(BF16), v) + _bdot(q, s_prev.astype(BF16)) * dq_ref[...]
    kd = (k * dk_ref[...]).astype(BF16)
    s_new = s_prev * dc_ref[...] + lax.dot_general(kd, v, _TN, preferred_element_type=F32)
    state_ref[...] = s_new

    mu = jnp.mean(o, axis=-1, keepdims=True)
    oc = o - mu
    var = jnp.mean(oc * oc, axis=-1, keepdims=True)
    on = oc * lax.rsqrt(var + GN_EPS)
    o_ref[...] = (_silu(g_ref[...]) * on).astype(o_ref.dtype)

    @pl.when(c == pl.num_programs(2) - 1)
    def _():
        sout_ref[...] = s_new


def _retention(qk, v, g, s0, decays, batch, seq, out_dtype):
    chunk = min(RET_CHUNK, seq)
    nc = seq // chunk
    din, dq, dk, dc = decays
    h_ = RET_HEADS
    tok = lambda b, h, c: b * nc + c
    return pl.pallas_call(
        _ret_kernel,
        grid=(batch, h_, nc),
        in_specs=[pl.BlockSpec((chunk, RET_DK), lambda b, h, c: (tok(b, h, c), h)),
                  pl.BlockSpec((chunk, RET_DK), lambda b, h, c: (tok(b, h, c), h_ + h)),
                  pl.BlockSpec((chunk, RET_DV), lambda b, h, c: (tok(b, h, c), h)),
                  pl.BlockSpec((chunk, RET_DV), lambda b, h, c: (tok(b, h, c), h)),
                  pl.BlockSpec((None, None, RET_DK, RET_DV), lambda b, h, c: (b, h, 0, 0)),
                  pl.BlockSpec((None, chunk, chunk), lambda b, h, c: (h, 0, 0)),
                  pl.BlockSpec((None, chunk, 1), lambda b, h, c: (h, 0, 0)),
                  pl.BlockSpec((None, chunk, 1), lambda b, h, c: (h, 0, 0)),
                  pl.BlockSpec((None, 1, 1), lambda b, h, c: (h, 0, 0))],
        out_specs=[pl.BlockSpec((chunk, RET_DV), lambda b, h, c: (tok(b, h, c), h)),
                   pl.BlockSpec((None, None, RET_DK, RET_DV), lambda b, h, c: (b, h, 0, 0))],
        out_shape=[jax.ShapeDtypeStruct((batch * seq, h_ * RET_DV), out_dtype),
                   jax.ShapeDtypeStruct((batch, h_, RET_DK, RET_DV), F32)],
        scratch_shapes=[pltpu.VMEM((RET_DK, RET_DV), F32)],
        compiler_params=_params(("parallel", "parallel", "arbitrary")),
        name="retention",
    )(qk, qk, v, g, s0, din, dq, dk, dc)


def _retention_decays(chunk):
    h = jnp.arange(RET_HEADS, dtype=F32)
    log_g = jnp.log1p(-(2.0 ** (-5.0 - h)))
    idx = jnp.arange(chunk, dtype=F32)
    rel = idx[:, None] - idx[None, :]
    din = jnp.where(rel >= 0, jnp.exp(log_g[:, None, None] * jnp.maximum(rel, 0.0)), 0.0)
    dq = jnp.exp(log_g[:, None] * (idx[None, :] + 1.0))[:, :, None]
    dk = jnp.exp(log_g[:, None] * (chunk - 1.0 - idx[None, :]))[:, :, None]
    dc = jnp.exp(log_g * chunk)[:, None, None]
    return din, dq, dk, dc


def _rope_tables(pos):
    half = RET_DK // 2
    inv_freq = ROPE_BASE ** (-jnp.arange(half, dtype=F32) / half)
    ang = pos.astype(F32)[:, None] * inv_freq[None, :]
    return jnp.cos(ang), jnp.sin(ang)


def _router_kernel(x_ref, wh_ref, wl_ref, b_ref, slot_ref, wgt_ref, off_ref, cnt_ref):
    tc = x_ref.shape[0]
    x = x_ref[...]
    xh = x.astype(BF16)
    xl = (x - xh.astype(F32)).astype(BF16)
    wh = wh_ref[...]
    wl = wl_ref[...]
    logits = (lax.dot_general(wh, xh, _NT, preferred_element_type=F32)
              + lax.dot_general(wh, xl, _NT, preferred_element_type=F32)
              + lax.dot_general(wl, xh, _NT, preferred_element_type=F32))
    scores = jax.nn.sigmoid(logits)
    biased = scores + b_ref[...]

    jj = lax.broadcasted_iota(I32, (GROUP_SIZE, tc), 0)
    groups = [biased[g * GROUP_SIZE:(g + 1) * GROUP_SIZE, :] for g in range(N_GROUPS)]
    gscore = []
    for rows in groups:
        m1 = jnp.max(rows, axis=0, keepdims=True)
        j1 = jnp.min(jnp.where(rows == m1, jj, GROUP_SIZE), axis=0, keepdims=True)
        m2 = jnp.max(jnp.where(jj == j1, -jnp.inf, rows), axis=0, keepdims=True)
        gscore.append(m1 + m2)
    gsel = [jnp.zeros((1, tc), dtype=jnp.bool_) for _ in range(N_GROUPS)]
    for _ in range(TOPK_GROUPS):
        gm = gscore[0]
        for sc in gscore[1:]:
            gm = jnp.maximum(gm, sc)
        found = jnp.zeros((1, tc), dtype=jnp.bool_)
        for g in range(N_GROUPS):
            hit = jnp.logical_and(gscore[g] == gm, jnp.logical_not(found))
            found = jnp.logical_or(found, hit)
            gsel[g] = jnp.logical_or(gsel[g], hit)
            gscore[g] = jnp.where(hit, -jnp.inf, gscore[g])
    masked = jnp.concatenate(
        [jnp.where(jnp.broadcast_to(gsel[g], groups[g].shape), groups[g], -jnp.inf)
         for g in range(N_GROUPS)], axis=0)
    ei = lax.broadcasted_iota(I32, masked.shape, 0)
    hits = []
    wk = []
    for _ in range(TOPK):
        m = jnp.max(masked, axis=0, keepdims=True)
        first = jnp.min(jnp.where(masked == m, ei, N_EXPERTS), axis=0, keepdims=True)
        hit = ei == first
        hits.append(hit)
        wk.append(jnp.sum(jnp.where(hit, scores, 0.0), axis=0, keepdims=True))
        masked = jnp.where(hit, -jnp.inf, masked)
    wsum = wk[0]
    for w in wk[1:]:
        wsum = wsum + w
    wgt_ref[...] = jnp.concatenate([w / wsum * ROUTED_SCALE for w in wk], axis=0)

    chosen = hits[0]
    for hit in hits[1:]:
        chosen = jnp.logical_or(chosen, hit)
    chosen = chosen.astype(BF16)
    r = lax.broadcasted_iota(I32, (tc, tc), 0)
    c = lax.broadcasted_iota(I32, (tc, tc), 1)
    rank = _bdot(chosen, (r <= c).astype(BF16))
    er = lax.broadcasted_iota(I32, (N_EXPERTS, N_EXPERTS), 0)
    ec = lax.broadcasted_iota(I32, (N_EXPERTS, N_EXPERTS), 1)
    below = _bdot((ec < er).astype(BF16), chosen)
    off = jnp.sum(below, axis=1, keepdims=True)
    cnt = rank[:, tc - 1:tc]
    pos = off + rank - 1.0
    slot_ref[...] = jnp.concatenate(
        [jnp.sum(jnp.where(hit, pos, 0.0), axis=0, keepdims=True) for hit in hits], axis=0).astype(I32)
    off_ref[...] = jnp.broadcast_to(off, off_ref.shape).astype(I32)
    cnt_ref[...] = jnp.broadcast_to(cnt, cnt_ref.shape).astype(I32)


def _router(x, wr_hi, wr_lo, b_col, tc):
    t, d = x.shape
    n = t // tc
    return pl.pallas_call(
        _router_kernel,
        grid=(n,),
        in_specs=[pl.BlockSpec((tc, d), lambda i: (i, 0)),
                  pl.BlockSpec((N_EXPERTS, d), lambda i: (0, 0)),
                  pl.BlockSpec((N_EXPERTS, d), lambda i: (0, 0)),
                  pl.BlockSpec((N_EXPERTS, 1), lambda i: (0, 0))],
        out_specs=[pl.BlockSpec((None, TOPK, tc), lambda i: (i, 0, 0)),
                   pl.BlockSpec((None, TOPK, tc), lambda i: (i, 0, 0)),
                   pl.BlockSpec((None, N_EXPERTS, LANES), lambda i: (i, 0, 0)),
                   pl.BlockSpec((None, N_EXPERTS, LANES), lambda i: (i, 0, 0))],
        out_shape=[jax.ShapeDtypeStruct((n, TOPK, tc), I32),
                   jax.ShapeDtypeStruct((n, TOPK, tc), F32),
                   jax.ShapeDtypeStruct((n, N_EXPERTS, LANES), I32),
                   jax.ShapeDtypeStruct((n, N_EXPERTS, LANES), I32)],
        compiler_params=_params(("parallel",)),
        name="router",
    )(x, wr_hi, wr_lo, b_col)


def _experts_kernel(off_ref, cnt_ref, slot_ref, wgt_ref, x_ref, wg_ref, wu_ref, wd_ref,
                    o_ref, sorted_ref):
    tc = x_ref.shape[0]
    ci = pl.program_id(0)
    e = pl.program_id(1)
    unroll = 8

    @pl.when(e == 0)
    def _dispatch():
        for kc in range(ROW_VREGS):
            o_ref[pl.ds(kc, tc, stride=ROW_VREGS), :] = x_ref[:, kc * LANES:(kc + 1) * LANES]
        sorted_ref[pl.ds(TOPK * tc * ROW_VREGS, EXP_BLOCK * ROW_VREGS), :] = jnp.zeros(
            (EXP_BLOCK * ROW_VREGS, LANES), F32)

        def body(i, carry):
            for u in range(unroll):
                t = i * unroll + u
                row = o_ref[pl.ds(pl.multiple_of(t * ROW_VREGS, ROW_VREGS), ROW_VREGS), :]
                for k in range(TOPK):
                    s = slot_ref[0, k, t]
                    sorted_ref[pl.ds(pl.multiple_of(s * ROW_VREGS, ROW_VREGS), ROW_VREGS), :] = row
            return carry

        lax.fori_loop(0, tc // unroll, body, 0)

    off = off_ref[ci * N_EXPERTS + e]
    cnt = cnt_ref[ci * N_EXPERTS + e]
    nblk = lax.shift_right_logical(cnt + (EXP_BLOCK - 1), 7)
    wg = wg_ref[0]
    wu = wu_ref[0]
    wd = wd_ref[0]
    rowid = lax.broadcasted_iota(I32, (EXP_BLOCK, 1), 0)

    def ffn_block(i, carry):
        r0 = (off + i * EXP_BLOCK) * ROW_VREGS
        xb = jnp.concatenate(
            [sorted_ref[pl.ds(r0 + kc, EXP_BLOCK, stride=ROW_VREGS), :] for kc in range(ROW_VREGS)], axis=1)
        xbb = xb.astype(BF16)
        h = _silu(_bdot(xbb, wg)) * _bdot(xbb, wu)
        y = _bdot(h.astype(BF16), wd)
        y = jnp.where(rowid < cnt - i * EXP_BLOCK, y, xb)
        for kc in range(ROW_VREGS):
            sorted_ref[pl.ds(r0 + kc, EXP_BLOCK, stride=ROW_VREGS), :] = y[:, kc * LANES:(kc + 1) * LANES]
        return carry

    lax.fori_loop(0, nblk, ffn_block, 0)

    @pl.when(e == pl.num_programs(1) - 1)
    def _combine():
        def body(i, carry):
            for u in range(unroll):
                t = i * unroll + u
                acc = None
                for k in range(TOPK):
                    s = slot_ref[0, k, t]
                    term = wgt_ref[0, k, t] * sorted_ref[
                        pl.ds(pl.multiple_of(s * ROW_VREGS, ROW_VREGS), ROW_VREGS), :]
                    acc = term if acc is None else acc + term
                o_ref[pl.ds(pl.multiple_of(t * ROW_VREGS, ROW_VREGS), ROW_VREGS), :] = acc
            return carry

        lax.fori_loop(0, tc // unroll, body, 0)


def _routed_experts(x, slot, wgt, seg_off, seg_cnt, wg, wu, wd, tc):
    t, d = x.shape
    n = t // tc
    sorted_rows = TOPK * tc + EXP_BLOCK
    grid_spec = pltpu.PrefetchScalarGridSpec(
        num_scalar_prefetch=2,
        grid=(n, N_EXPERTS),
        in_specs=[pl.BlockSpec((1, TOPK, tc), lambda c, e, *_: (c, 0, 0), memory_space=pltpu.SMEM),
                  pl.BlockSpec((1, TOPK, tc), lambda c, e, *_: (c, 0, 0), memory_space=pltpu.SMEM),
                  pl.BlockSpec((tc, d), lambda c, e, *_: (c, 0)),
                  pl.BlockSpec((1, d, FF_EXPERT), lambda c, e, *_: (e, 0, 0)),
                  pl.BlockSpec((1, d, FF_EXPERT), lambda c, e, *_: (e, 0, 0)),
                  pl.BlockSpec((1, FF_EXPERT, d), lambda c, e, *_: (e, 0, 0))],
        out_specs=pl.BlockSpec((tc * ROW_VREGS, LANES), lambda c, e, *_: (c, 0)),
        scratch_shapes=[pltpu.VMEM((sorted_rows * ROW_VREGS, LANES), F32)],
    )
    return pl.pallas_call(
        _experts_kernel,
        grid_spec=grid_spec,
        out_shape=jax.ShapeDtypeStruct((t * ROW_VREGS, LANES), F32),
        compiler_params=_params(("arbitrary", "arbitrary"), VMEM_LIMIT_BIG),
        name="routed_experts",
    )(seg_off, seg_cnt, slot, wgt, x, wg, wu, wd)


def _shared_ln_kernel(x_ref, r_ref, wgu_ref, wd_ref, g_ref, b_ref, o_ref):
    tm = x_ref.shape[0]
    x = x_ref[...]
    gu = _bdot(x.astype(BF16), wgu_ref[...])
    ff = wgu_ref.shape[1] // 2
    h = _silu(gu[:, :ff]) * gu[:, ff:]
    shared = _bdot(h.astype(BF16), wd_ref[...])
    routed = jnp.concatenate(
        [r_ref[pl.ds(kc, tm, stride=ROW_VREGS), :] for kc in range(ROW_VREGS)], axis=1)
    z = DEEPNORM_ALPHA * x + (shared + routed)
    o_ref[...] = _layer_norm_rows(z, g_ref[...], b_ref[...])


def _shared_ln(x, routed_rows, wgu, wd, g, b, tm):
    t, d = x.shape
    return pl.pallas_call(
        _shared_ln_kernel,
        grid=(t // tm,),
        in_specs=[pl.BlockSpec((tm, d), lambda i: (i, 0)),
                  pl.BlockSpec((tm * ROW_VREGS, LANES), lambda i: (i, 0)),
                  pl.BlockSpec(wgu.shape, lambda i: (0, 0)),
                  pl.BlockSpec(wd.shape, lambda i: (0, 0)),
                  pl.BlockSpec((1, d), lambda i: (0, 0)),
                  pl.BlockSpec((1, d), lambda i: (0, 0))],
        out_specs=pl.BlockSpec((tm, d), lambda i: (i, 0)),
        out_shape=jax.ShapeDtypeStruct((t, d), F32),
        compiler_params=_params(("parallel",)),
        name="shared_ln",
    )(x, routed_rows, wgu, wd, g.reshape(1, d), b.reshape(1, d))


def _moe_ln(x, p, g, b, tc):
    slot, wgt, off, cnt = _router(x, p["wr_hi"], p["wr_lo"], p["b_col"], tc)
    routed = _routed_experts(x, slot, wgt, off[:, :, 0].reshape(-1), cnt[:, :, 0].reshape(-1),
                             p["wg"], p["wu"], p["wd"], tc)
    return _shared_ln(x, routed, p["wsgu"], p["wsd"], g, b, min(tc, 512))


def _cumsum_kernel(x_ref, o_ref):
    n = x_ref.shape[0]
    blk = LANES
    r = lax.broadcasted_iota(I32, (blk, blk), 0)
    c = lax.broadcasted_iota(I32, (blk, blk), 1)
    tri = (c <= r).astype(BF16)

    def body(i, carry):
        start = pl.multiple_of(i * blk, blk)
        hi, mid, lo = _split3(x_ref[pl.ds(start, blk), :])
        within = (_bdot(tri, lo) + _bdot(tri, mid)) + _bdot(tri, hi)
        o_ref[pl.ds(start, blk), :] = within + carry
        return carry + jnp.sum(x_ref[pl.ds(start, blk), :], axis=0, keepdims=True)

    lax.fori_loop(0, n // blk, body, jnp.zeros((1, x_ref.shape[1]), F32))


def _cumsum_rows(x, batch, seq):
    return pl.pallas_call(
        _cumsum_kernel,
        grid=(batch,),
        in_specs=[pl.BlockSpec((seq, x.shape[1]), lambda b: (b, 0))],
        out_specs=pl.BlockSpec((seq, x.shape[1]), lambda b: (b, 0)),
        out_shape=jax.ShapeDtypeStruct(x.shape, F32),
        compiler_params=_params(("parallel",)),
        name="logf_cumsum",
    )(x)


def _fox_prompt_kernel(q_ref, k_ref, v_ref, cumq_ref, cumk_ref, o_ref, *, tq):
    h = pl.program_id(1)
    qi = pl.program_id(2)
    q = q_ref[...]
    lane = lax.broadcasted_iota(I32, cumq_ref.shape, 1)
    cq = jnp.sum(jnp.where(lane == h, cumq_ref[...], 0.0), axis=1, keepdims=True)
    qpos = qi * tq + lax.broadcasted_iota(I32, (tq, 1), 0)
    scale = FOX_DH ** -0.5

    def body(kb, carry):
        m, l, acc = carry
        start = pl.multiple_of(kb * tq, tq)
        k = k_ref[pl.ds(start, tq), :]
        v = v_ref[pl.ds(start, tq), :]
        ck = cumk_ref[:, pl.ds(start, tq)]
        s = lax.dot_general(q, k, _NT, preferred_element_type=F32) * scale + cq - ck
        kpos = start + lax.broadcasted_iota(I32, (1, tq), 1)
        s = jnp.where(kpos <= qpos, s, NEG_INF)
        m_new = jnp.maximum(m, jnp.max(s, axis=1, keepdims=True))
        a = jnp.exp(m - m_new)
        p = jnp.exp(s - m_new)
        l = a * l + jnp.sum(p, axis=1, keepdims=True)
        acc = a * acc + _bdot(p.astype(BF16), v)
        return m_new, l, acc

    init = (jnp.full((tq, 1), -jnp.inf, F32), jnp.zeros((tq, 1), F32), jnp.zeros((tq, FOX_DH), F32))
    m, l, acc = lax.fori_loop(0, qi + 1, body, init)
    o_ref[...] = (acc / l).astype(o_ref.dtype)


def _fox_prompt(q, k, v, cum_tok, cum_row, tq):
    b, h, l, dh = q.shape
    nq = l // tq
    kern = functools.partial(_fox_prompt_kernel, tq=tq)
    return pl.pallas_call(
        kern,
        grid=(b, h, nq),
        in_specs=[pl.BlockSpec((None, None, tq, dh), lambda bi, hi, qi: (bi, hi, qi, 0)),
                  pl.BlockSpec((None, None, l, dh), lambda bi, hi, qi: (bi, hi, 0, 0)),
                  pl.BlockSpec((None, None, l, dh), lambda bi, hi, qi: (bi, hi, 0, 0)),
                  pl.BlockSpec((tq, LANES), lambda bi, hi, qi: (bi * nq + qi, 0)),
                  pl.BlockSpec((None, None, 1, l), lambda bi, hi, qi: (bi, hi, 0, 0))],
        out_specs=pl.BlockSpec((None, None, tq, dh), lambda bi, hi, qi: (bi, hi, qi, 0)),
        out_shape=jax.ShapeDtypeStruct((b, h, l, dh), BF16),
        compiler_params=_params(("parallel", "parallel", "arbitrary")),
        name="fox_prompt",
    )(q, k, v, cum_tok, cum_row)


N_QROWS = 128
PIECE_LANE0 = FOX_DH
ROWBIAS_LANE0 = PIECE_LANE0 + 3 * FOX_HEADS


def _fox_sample_kernel(pt_ref, q_ref, lfs_ref, ks_ref, vs_ref, kc_ref, vc_ref, lfc_ref, o_ref,
                       qaug_ref, m_ref, l_ref, acc_ref, carry_ref, *, n_pages, dec_seq):
    del pt_ref
    step = pl.program_id(1)
    rows_kv = PAGE_SIZE * FOX_HEADS
    n_iota = lax.broadcasted_iota(I32, (N_QROWS, 1), 0)
    row_head = n_iota // dec_seq
    row_q = n_iota % dec_seq
    c_iota = lax.broadcasted_iota(I32, (1, rows_kv), 1)
    col_head = c_iota % FOX_HEADS
    col_pos = c_iota // FOX_HEADS
    lane = lax.broadcasted_iota(I32, (1, LANES), 1)

    def place(piece, lane0):
        r = lax.broadcasted_iota(I32, (FOX_HEADS, LANES), 0)
        c = lax.broadcasted_iota(I32, (FOX_HEADS, LANES), 1)
        return _bdot(piece, (c == r + lane0).astype(BF16))

    def attend(k3, v3, key_bias, valid):
        hi, mid, lo = _split3(key_bias)
        wide = (place(hi, PIECE_LANE0) + place(mid, PIECE_LANE0 + FOX_HEADS)
                + place(lo, PIECE_LANE0 + 2 * FOX_HEADS))
        hsub = lax.broadcasted_iota(I32, (FOX_HEADS, LANES), 0)
        ll = lax.broadcasted_iota(I32, (FOX_HEADS, LANES), 1)
        own = jnp.logical_or(jnp.logical_or(ll == hsub + PIECE_LANE0, ll == hsub + PIECE_LANE0 + FOX_HEADS),
                             ll == hsub + PIECE_LANE0 + 2 * FOX_HEADS)
        ones_col = jnp.logical_and(ll >= ROWBIAS_LANE0, ll < ROWBIAS_LANE0 + 3)
        z3 = jnp.where(own[None], jnp.broadcast_to(wide[:, None, :], (PAGE_SIZE, FOX_HEADS, LANES)), 0.0)
        z3 = jnp.where(ones_col[None], 1.0, z3)
        k_wide = jnp.concatenate([k3, jnp.zeros((PAGE_SIZE, FOX_HEADS, LANES - FOX_DH), F32)], axis=-1)
        kaug = (k_wide + z3).reshape(rows_kv, LANES).astype(BF16)
        s = lax.dot_general(qaug_ref[...], kaug, _NT, preferred_element_type=F32)
        s = jnp.where(valid, s, NEG_INF)
        m_old = m_ref[...]
        m_new = jnp.maximum(m_old, jnp.max(s, axis=1, keepdims=True))
        a = jnp.exp(m_old - m_new)
        p = jnp.exp(s - m_new)
        l_ref[...] = a * l_ref[...] + jnp.sum(p, axis=1, keepdims=True)
        acc_ref[...] = a * acc_ref[...] + _bdot(p.astype(BF16), v3.reshape(rows_kv, FOX_DH).astype(BF16))
        m_ref[...] = m_new

    @pl.when(step == 0)
    def _own_rows():
        q = q_ref[0] * (FOX_DH ** -0.5)
        q_rows = jnp.concatenate([q[:, h * FOX_DH:(h + 1) * FOX_DH] for h in range(FOX_HEADS)], axis=0)
        lfs = lfs_ref[0][:, :FOX_HEADS]
        jr = lax.broadcasted_iota(I32, (dec_seq, dec_seq), 0)
        jc = lax.broadcasted_iota(I32, (dec_seq, dec_seq), 1)
        upto = (jc <= jr).astype(BF16)
        p_hi, p_mid, p_lo = _split3(lfs)
        prefix = (_bdot(upto, p_lo) + _bdot(upto, p_mid)) + _bdot(upto, p_hi)
        pre_col = jnp.concatenate([prefix[:, h:h + 1] for h in range(FOX_HEADS)], axis=0)
        hi, mid, lo = _split3(pre_col)
        hi, mid, lo = hi.astype(F32), mid.astype(F32), lo.astype(F32)
        tail = jnp.where(jnp.logical_and(lane >= PIECE_LANE0, lane < ROWBIAS_LANE0), 1.0, 0.0)
        tail = jnp.where(lane == ROWBIAS_LANE0, hi, tail)
        tail = jnp.where(lane == ROWBIAS_LANE0 + 1, mid, tail)
        tail = jnp.where(lane == ROWBIAS_LANE0 + 2, lo, tail)
        qaug_ref[...] = (jnp.concatenate([q_rows, jnp.zeros((N_QROWS, LANES - FOX_DH), F32)], axis=1)
                         + tail).astype(BF16)
        m_ref[...] = jnp.full(m_ref.shape, -jnp.inf, F32)
        l_ref[...] = jnp.zeros(l_ref.shape, F32)
        acc_ref[...] = jnp.zeros(acc_ref.shape, F32)
        carry_ref[...] = jnp.zeros(carry_ref.shape, F32)
        pad = PAGE_SIZE - dec_seq
        key_bias = jnp.concatenate([-prefix, jnp.zeros((pad, FOX_HEADS), F32)], axis=0)
        valid = jnp.logical_and(jnp.logical_and(col_head == row_head, col_pos <= row_q), col_pos < dec_seq)
        attend(ks_ref[0], vs_ref[0], key_bias, valid)

    @pl.when(step > 0)
    def _cache_page():
        lf = lfc_ref[0]
        hi, mid, lo = _split3(lf)
        r = lax.broadcasted_iota(I32, (PAGE_SIZE, PAGE_SIZE), 0)
        c = lax.broadcasted_iota(I32, (PAGE_SIZE, PAGE_SIZE), 1)
        after = (c > r).astype(BF16)
        suffix = (_bdot(after, lo) + _bdot(after, mid)) + _bdot(after, hi)
        carry = carry_ref[...]
        attend(kc_ref[0], vc_ref[0], suffix + carry, col_head == row_head)
        carry_ref[...] = carry + jnp.sum(lf, axis=0, keepdims=True)

    @pl.when(step == n_pages)
    def _finish():
        o_ref[0] = acc_ref[...] / l_ref[...]


def _fox_sample(q, lfs, ks_page, vs_page, cache_k, cache_v, cache_logf, page_table):
    db, dec_seq, _ = q.shape
    n_pages = page_table.shape[1]
    kern = functools.partial(_fox_sample_kernel, n_pages=n_pages, dec_seq=dec_seq)

    def page(b, s, pt):
        return pt[b, n_pages - jnp.maximum(s, 1)]

    grid_spec = pltpu.PrefetchScalarGridSpec(
        num_scalar_prefetch=1,
        grid=(db, n_pages + 1),
        in_specs=[pl.BlockSpec((1, dec_seq, D_MODEL), lambda b, s, pt: (b, 0, 0)),
                  pl.BlockSpec((1, dec_seq, LANES), lambda b, s, pt: (b, 0, 0)),
                  pl.BlockSpec((1, PAGE_SIZE, FOX_HEADS, FOX_DH), lambda b, s, pt: (b, 0, 0, 0)),
                  pl.BlockSpec((1, PAGE_SIZE, FOX_HEADS, FOX_DH), lambda b, s, pt: (b, 0, 0, 0)),
                  pl.BlockSpec((1, PAGE_SIZE, FOX_HEADS, FOX_DH), lambda b, s, pt: (page(b, s, pt), 0, 0, 0)),
                  pl.BlockSpec((1, PAGE_SIZE, FOX_HEADS, FOX_DH), lambda b, s, pt: (page(b, s, pt), 0, 0, 0)),
                  pl.BlockSpec((1, PAGE_SIZE, FOX_HEADS), lambda b, s, pt: (page(b, s, pt), 0, 0))],
        out_specs=pl.BlockSpec((1, N_QROWS, FOX_DH), lambda b, s, pt: (b, 0, 0)),
        scratch_shapes=[pltpu.VMEM((N_QROWS, LANES), BF16),
                        pltpu.VMEM((N_QROWS, 1), F32),
                        pltpu.VMEM((N_QROWS, 1), F32),
                        pltpu.VMEM((N_QROWS, FOX_DH), F32),
                        pltpu.VMEM((1, FOX_HEADS), F32)],
    )
    return pl.pallas_call(
        kern,
        grid_spec=grid_spec,
        out_shape=jax.ShapeDtypeStruct((db, N_QROWS, FOX_DH), F32),
        compiler_params=_params(("parallel", "arbitrary")),
        name="fox_sample",
    )(page_table, q, lfs, ks_page, vs_page, cache_k, cache_v, cache_logf)


def _moe_params(layer, w_router, b_router, w_gate, w_up, w_down, ws_gate, ws_up, ws_down):
    wr_t = w_router[layer].T
    wr_hi = wr_t.astype(BF16)
    wr_lo = (wr_t - wr_hi.astype(F32)).astype(BF16)
    return dict(
        wr_hi=wr_hi, wr_lo=wr_lo, b_col=b_router[layer].reshape(N_EXPERTS, 1),
        wg=w_gate[layer].astype(BF16), wu=w_up[layer].astype(BF16), wd=w_down[layer].astype(BF16),
        wsgu=jnp.concatenate([ws_gate[layer], ws_up[layer]], axis=1).astype(BF16),
        wsd=ws_down[layer].astype(BF16))


def kernel(x_prompt, x_sample, state_ret, cache_k, cache_v, cache_logf, page_table, ret_w_in, ret_w_out,
           fox_w_kvf, fox_b_f, fox_w_q, fox_w_out, ln_g, ln_b, moe_w_router, moe_b_router, moe_w_gate,
           moe_w_up, moe_w_down, moe_ws_gate, moe_ws_up, moe_ws_down):
    bp, lp, d = x_prompt.shape
    bs, ls, _ = x_sample.shape
    tp, ts = bp * lp, bs * ls
    assert fox_heads_rows(ls) == N_QROWS
    xp = x_prompt.reshape(tp, d)
    xs = x_sample.reshape(ts, d)
    hq = RET_HEADS * RET_DK
    hv = RET_HEADS * RET_DV
    fd = FOX_HEADS * FOX_DH
    tm_p = 512
    tc_p = 1024

    moe = [_moe_params(layer, moe_w_router, moe_b_router, moe_w_gate, moe_w_up, moe_w_down,
                       moe_ws_gate, moe_ws_up, moe_ws_down) for layer in range(DEPTH)]

    w_in = ret_w_in[0].astype(BF16)
    w_out = ret_w_out[0].astype(BF16)
    cos_p, sin_p = _rope_tables(jnp.arange(lp))
    cos_s, sin_s = _rope_tables(PAST_LEN + jnp.arange(ls))
    cos_s, sin_s = jnp.tile(cos_s, (bs, 1)), jnp.tile(sin_s, (bs, 1))

    qk_p = _qk_rope(xp, w_in, cos_p, sin_p, tm_p)
    v_p = _matmul(xp, w_in, 2 * hq, hv, BF16, tm_p, 512)
    g_p = _matmul(xp, w_in, 2 * hq + hv, hv, F32, tm_p, 512)
    o_p, st_p = _retention(qk_p, v_p, g_p, jnp.zeros((bp, RET_HEADS, RET_DK, RET_DV), F32),
                           _retention_decays(min(RET_CHUNK, lp)), bp, lp, BF16)
    xp = _mm_res_ln(o_p, w_out, xp, ln_g[0, 0], ln_b[0, 0], tm_p)

    qk_s = _qk_rope(xs, w_in, cos_s, sin_s, ts)
    v_s = _matmul(xs, w_in, 2 * hq, hv, F32, ts, 512)
    g_s = _matmul(xs, w_in, 2 * hq + hv, hv, F32, ts, 512)
    o_s, st_s = _retention(qk_s, v_s, g_s, state_ret[0], _retention_decays(min(RET_CHUNK, ls)), bs, ls, F32)
    xs = _mm_res_ln(o_s, w_out, xs, ln_g[0, 0], ln_b[0, 0], ts)

    xp = _moe_ln(xp, moe[0], ln_g[0, 1], ln_b[0, 1], tc_p)
    xs = _moe_ln(xs, moe[0], ln_g[0, 1], ln_b[0, 1], ts)

    w_kvf = fox_w_kvf.astype(BF16)
    w_f_pad = jnp.zeros((d, LANES), BF16).at[:, :FOX_HEADS].set(w_kvf[:, 2 * fd:])
    b_f_pad = jnp.zeros((1, LANES), F32).at[0, :FOX_HEADS].set(fox_b_f)
    kv_p = _matmul(xp, w_kvf, 0, 2 * fd, F32, tm_p, 512)
    lf_p = _logf_proj(xp, w_f_pad, b_f_pad, tm_p)
    kv_s = _matmul(xs, w_kvf, 0, 2 * fd, F32, ts, 512)
    lf_s = _logf_proj(xs, w_f_pad, b_f_pad, ts)

    k_p = kv_p[:, :fd].reshape(bp, lp, FOX_HEADS, FOX_DH)
    v_p4 = kv_p[:, fd:].reshape(bp, lp, FOX_HEADS, FOX_DH)
    k_s = kv_s[:, :fd].reshape(bs, ls, FOX_HEADS, FOX_DH)
    v_s4 = kv_s[:, fd:].reshape(bs, ls, FOX_HEADS, FOX_DH)

    w_q = fox_w_q[0].astype(BF16)
    w_o = fox_w_out[0].astype(BF16)
    q_p = _matmul(xp, w_q, 0, fd, BF16, tm_p, 512)
    heads = lambda a: a.reshape(bp, lp, FOX_HEADS, FOX_DH).transpose(0, 2, 1, 3)
    cum_p = _cumsum_rows(lf_p, bp, lp)
    cum_row = cum_p[:, :FOX_HEADS].reshape(bp, lp, FOX_HEADS).transpose(0, 2, 1).reshape(bp, FOX_HEADS, 1, lp)
    att_p = _fox_prompt(heads(q_p), heads(kv_p[:, :fd].astype(BF16)), heads(kv_p[:, fd:].astype(BF16)),
                        cum_p, cum_row, 512)
    att_p = att_p.transpose(0, 2, 1, 3).reshape(tp, fd)
    xp = _mm_res_ln(att_p, w_o, xp, ln_g[1, 0], ln_b[1, 0], tm_p)

    q_s = _matmul(xs, w_q, 0, fd, F32, ts, 512)
    pad_rows = lambda a: jnp.pad(a, ((0, 0), (0, PAGE_SIZE - ls), (0, 0), (0, 0)))
    att_s = _fox_sample(q_s.reshape(bs, ls, d), lf_s.reshape(bs, ls, LANES), pad_rows(k_s), pad_rows(v_s4),
                        cache_k, cache_v, cache_logf, page_table)
    att_s = att_s.reshape(bs, FOX_HEADS, ls, FOX_DH).transpose(0, 2, 1, 3).reshape(ts, fd)
    xs = _mm_res_ln(att_s, w_o, xs, ln_g[1, 0], ln_b[1, 0], ts)

    xp = _moe_ln(xp, moe[1], ln_g[1, 1], ln_b[1, 1], tc_p)
    xs = _moe_ln(xs, moe[1], ln_g[1, 1], ln_b[1, 1], ts)

    return (xp.reshape(bp, lp, d), xs.reshape(bs, ls, d), st_p[None], st_s[None],
            k_p, v_p4, lf_p[:, :FOX_HEADS].reshape(bp, lp, FOX_HEADS),
            k_s, v_s4, lf_s[:, :FOX_HEADS].reshape(bs, ls, FOX_HEADS))


def fox_heads_rows(dec_seq):
    return FOX_HEADS * dec_seq
```

```python
import functools

import jax
import jax.numpy as jnp
from jax import lax
from jax.experimental import pallas as pl
from jax.experimental.pallas import tpu as pltpu

F32 = jnp.float32
BF16 = jnp.bfloat16
I32 = jnp.int32

D_MODEL = 1024
DEPTH = 2
PAST_LEN = 8192
PAGE_SIZE = 128
N_A_LAYERS = DEPTH // 2
RET_HEADS = 4
RET_DK = D_MODEL // RET_HEADS
RET_DV = 2 * RET_DK
RET_CHUNK = 128
ROPE_BASE = 10000.0
FOX_HEADS = 16
FOX_DH = D_MODEL // FOX_HEADS
NEG_INF = -1e30
N_EXPERTS = 64
TOPK = 8
N_GROUPS = 8
GROUP_SIZE = N_EXPERTS // N_GROUPS
TOPK_GROUPS = 4
FF_EXPERT = D_MODEL // 4
ROUTED_SCALE = 2.5
DEEPNORM_ALPHA = (2.0 * DEPTH) ** 0.25
LN_EPS = 1e-5
GN_EPS = 1e-5

LANES = 128
SUBLANES = 8
ROW_VREGS = D_MODEL // LANES
EXP_BLOCK = 128
EXP_BLOCK_LOG2 = EXP_BLOCK.bit_length() - 1
EXPERTS_PER_STEP = 2
VMEM_LIMIT_BIG = 60 * 1024 * 1024
VMEM_LIMIT = 48 * 1024 * 1024

_NT = (((1,), (1,)), ((), ()))
_TN = (((0,), (0,)), ((), ()))


def _params(sem, vmem=VMEM_LIMIT):
    return pltpu.CompilerParams(dimension_semantics=sem, vmem_limit_bytes=vmem)


def _bdot(a, b):
    return jnp.dot(a, b, preferred_element_type=F32)


def _layer_norm_rows(z, g, b):
    mu = jnp.mean(z, axis=-1, keepdims=True)
    zc = z - mu
    var = jnp.mean(zc * zc, axis=-1, keepdims=True)
    return zc * lax.rsqrt(var + LN_EPS) * g + b


def _silu(x):
    return x * jax.nn.sigmoid(x)


def _split3(x):
    hi = x.astype(BF16)
    r1 = x - hi.astype(F32)
    mid = r1.astype(BF16)
    lo = (r1 - mid.astype(F32)).astype(BF16)
    return hi, mid, lo


def _mm_kernel(x_ref, w_ref, o_ref):
    o_ref[...] = _bdot(x_ref[...].astype(BF16), w_ref[...]).astype(o_ref.dtype)


def _matmul(x, w, col0, n, out_dtype, tm, tn):
    m, k = x.shape
    j0 = col0 // tn
    return pl.pallas_call(
        _mm_kernel,
        grid=(m // tm, n // tn),
        in_specs=[pl.BlockSpec((tm, k), lambda i, j: (i, 0)),
                  pl.BlockSpec((k, tn), lambda i, j: (0, j + j0))],
        out_specs=pl.BlockSpec((tm, tn), lambda i, j: (i, j)),
        out_shape=jax.ShapeDtypeStruct((m, n), out_dtype),
        compiler_params=_params(("parallel", "arbitrary")),
        name="proj",
    )(x, w)


def _mm_rope_kernel(x_ref, w_ref, cos_ref, sin_ref, o_ref, *, n_q_tiles, k_scale):
    j = pl.program_id(1)
    acc = _bdot(x_ref[...].astype(BF16), w_ref[...])
    half = RET_DK // 2
    x1 = acc[:, :half]
    x2 = acc[:, half:]
    c = cos_ref[...]
    s = sin_ref[...]
    sc = jnp.where(j < n_q_tiles, 1.0, k_scale).astype(F32)
    o_ref[:, :half] = (x1 * c - x2 * s) * sc
    o_ref[:, half:] = (x2 * c + x1 * s) * sc


def _qk_rope(x, w, cos_rows, sin_rows, tm):
    m, k = x.shape
    n = 2 * RET_HEADS * RET_DK
    nper = cos_rows.shape[0] // tm
    kern = functools.partial(_mm_rope_kernel, n_q_tiles=RET_HEADS, k_scale=RET_DK ** -0.5)
    return pl.pallas_call(
        kern,
        grid=(m // tm, n // RET_DK),
        in_specs=[pl.BlockSpec((tm, k), lambda i, j: (i, 0)),
                  pl.BlockSpec((k, RET_DK), lambda i, j: (0, j)),
                  pl.BlockSpec((tm, RET_DK // 2), lambda i, j: (i % nper, 0)),
                  pl.BlockSpec((tm, RET_DK // 2), lambda i, j: (i % nper, 0))],
        out_specs=pl.BlockSpec((tm, RET_DK), lambda i, j: (i, j)),
        out_shape=jax.ShapeDtypeStruct((m, n), F32),
        compiler_params=_params(("parallel", "arbitrary")),
        name="qk_rope",
    )(x, w, cos_rows, sin_rows)


def _mm_res_ln_kernel(a_ref, w_ref, x_ref, g_ref, b_ref, o_ref):
    mix = _bdot(a_ref[...].astype(BF16), w_ref[...])
    z = DEEPNORM_ALPHA * x_ref[...] + mix
    o_ref[...] = _layer_norm_rows(z, g_ref[...], b_ref[...])


def _mm_res_ln(a, w, x, g, b, tm):
    m, k = a.shape
    d = w.shape[1]
    return pl.pallas_call(
        _mm_res_ln_kernel,
        grid=(m // tm,),
        in_specs=[pl.BlockSpec((tm, k), lambda i: (i, 0)),
                  pl.BlockSpec((k, d), lambda i: (0, 0)),
                  pl.BlockSpec((tm, d), lambda i: (i, 0)),
                  pl.BlockSpec((1, d), lambda i: (0, 0)),
                  pl.BlockSpec((1, d), lambda i: (0, 0))],
        out_specs=pl.BlockSpec((tm, d), lambda i: (i, 0)),
        out_shape=jax.ShapeDtypeStruct((m, d), F32),
        compiler_params=_params(("parallel",)),
        name="mix_out_ln",
    )(a, w, x, g.reshape(1, d), b.reshape(1, d))


def _mm_logsig_kernel(x_ref, w_ref, b_ref, o_ref):
    z = _bdot(x_ref[...].astype(BF16), w_ref[...]) + b_ref[...]
    o_ref[...] = -(jnp.maximum(-z, 0.0) + jnp.log1p(jnp.exp(-jnp.abs(z))))


def _logf_proj(x, w_pad, b_pad, tm):
    m, k = x.shape
    return pl.pallas_call(
        _mm_logsig_kernel,
        grid=(m // tm,),
        in_specs=[pl.BlockSpec((tm, k), lambda i: (i, 0)),
                  pl.BlockSpec((k, LANES), lambda i: (0, 0)),
                  pl.BlockSpec((1, LANES), lambda i: (0, 0))],
        out_specs=pl.BlockSpec((tm, LANES), lambda i: (i, 0)),
        out_shape=jax.ShapeDtypeStruct((m, LANES), F32),
        compiler_params=_params(("parallel",)),
        name="logf_proj",
    )(x, w_pad, b_pad)


def _ret_kernel(q_ref, k_ref, v_ref, g_ref, s0_ref, din_ref, dq_ref, dk_ref, dc_ref,
                o_ref, sout_ref, state_ref):
    c = pl.program_id(2)

    @pl.when(c == 0)
    def _():
        state_ref[...] = s0_ref[...]

    q = q_ref[...].astype(BF16)
    k = k_ref[...]
    v = v_ref[...].astype(BF16)
    s_prev = state_ref[...]
    scores = lax.dot_general(q, k.astype(BF16), _NT, preferred_element_type=F32) * din_ref[...]
    o = _bdot(scores.astype(BF16), v) + _bdot(q, s_prev.astype(BF16)) * dq_ref[...]
    kd = (k * dk_ref[...]).astype(BF16)
    s_new = s_prev * dc_ref[...] + lax.dot_general(kd, v, _TN, preferred_element_type=F32)
    state_ref[...] = s_new

    mu = jnp.mean(o, axis=-1, keepdims=True)
    oc = o - mu
    var = jnp.mean(oc * oc, axis=-1, keepdims=True)
    on = oc * lax.rsqrt(var + GN_EPS)
    o_ref[...] = (_silu(g_ref[...]) * on).astype(o_ref.dtype)

    @pl.when(c == pl.num_programs(2) - 1)
    def _():
        sout_ref[...] = s_new


def _retention(qk, v, g, s0, decays, batch, seq, out_dtype):
    chunk = min(RET_CHUNK, seq)
    nc = seq // chunk
    din, dq, dk, dc = decays
    h_ = RET_HEADS
    tok = lambda b, h, c: b * nc + c
    return pl.pallas_call(
        _ret_kernel,
        grid=(batch, h_, nc),
        in_specs=[pl.BlockSpec((chunk, RET_DK), lambda b, h, c: (tok(b, h, c), h)),
                  pl.BlockSpec((chunk, RET_DK), lambda b, h, c: (tok(b, h, c), h_ + h)),
                  pl.BlockSpec((chunk, RET_DV), lambda b, h, c: (tok(b, h, c), h)),
                  pl.BlockSpec((chunk, RET_DV), lambda b, h, c: (tok(b, h, c), h)),
                  pl.BlockSpec((None, None, RET_DK, RET_DV), lambda b, h, c: (b, h, 0, 0)),
                  pl.BlockSpec((None, chunk, chunk), lambda b, h, c: (h, 0, 0)),
                  pl.BlockSpec((None, chunk, 1), lambda b, h, c: (h, 0, 0)),
                  pl.BlockSpec((None, chunk, 1), lambda b, h, c: (h, 0, 0)),
                  pl.BlockSpec((None, 1, 1), lambda b, h, c: (h, 0, 0))],
        out_specs=[pl.BlockSpec((chunk, RET_DV), lambda b, h, c: (tok(b, h, c), h)),
                   pl.BlockSpec((None, None, RET_DK, RET_DV), lambda b, h, c: (b, h, 0, 0))],
        out_shape=[jax.ShapeDtypeStruct((batch * seq, h_ * RET_DV), out_dtype),
                   jax.ShapeDtypeStruct((batch, h_, RET_DK, RET_DV), F32)],
        scratch_shapes=[pltpu.VMEM((RET_DK, RET_DV), F32)],
        compiler_params=_params(("parallel", "parallel", "arbitrary")),
        name="retention",
    )(qk, qk, v, g, s0, din, dq, dk, dc)


def _retention_decays(chunk):
    h = jnp.arange(RET_HEADS, dtype=F32)
    log_g = jnp.log1p(-(2.0 ** (-5.0 - h)))
    idx = jnp.arange(chunk, dtype=F32)
    rel = idx[:, None] - idx[None, :]
    din = jnp.where(rel >= 0, jnp.exp(log_g[:, None, None] * jnp.maximum(rel, 0.0)), 0.0)
    dq = jnp.exp(log_g[:, None] * (idx[None, :] + 1.0))[:, :, None]
    dk = jnp.exp(log_g[:, None] * (chunk - 1.0 - idx[None, :]))[:, :, None]
    dc = jnp.exp(log_g * chunk)[:, None, None]
    return din, dq, dk, dc


def _rope_tables(pos):
    half = RET_DK // 2
    inv_freq = ROPE_BASE ** (-jnp.arange(half, dtype=F32) / half)
    ang = pos.astype(F32)[:, None] * inv_freq[None, :]
    return jnp.cos(ang), jnp.sin(ang)


def _router_kernel(x_ref, wh_ref, wl_ref, b_ref, slot_ref, wgt_ref, off_ref, cnt_ref):
    tc = x_ref.shape[0]
    x = x_ref[...]
    xh = x.astype(BF16)
    xl = (x - xh.astype(F32)).astype(BF16)
    wh = wh_ref[...]
    wl = wl_ref[...]
    logits = (lax.dot_general(wh, xh, _NT, preferred_element_type=F32)
              + lax.dot_general(wh, xl, _NT, preferred_element_type=F32)
              + lax.dot_general(wl, xh, _NT, preferred_element_type=F32))
    scores = jax.nn.sigmoid(logits)
    biased = scores + b_ref[...]

    jj = lax.broadcasted_iota(I32, (GROUP_SIZE, tc), 0)
    groups = [biased[g * GROUP_SIZE:(g + 1) * GROUP_SIZE, :] for g in range(N_GROUPS)]
    gscore = []
    for rows in groups:
        m1 = jnp.max(rows, axis=0, keepdims=True)
        j1 = jnp.min(jnp.where(rows == m1, jj, GROUP_SIZE), axis=0, keepdims=True)
        m2 = jnp.max(jnp.where(jj == j1, -jnp.inf, rows), axis=0, keepdims=True)
        gscore.append(m1 + m2)
    gsel = [jnp.zeros((1, tc), dtype=jnp.bool_) for _ in range(N_GROUPS)]
    for _ in range(TOPK_GROUPS):
        gm = gscore[0]
        for sc in gscore[1:]:
            gm = jnp.maximum(gm, sc)
        found = jnp.zeros((1, tc), dtype=jnp.bool_)
        for g in range(N_GROUPS):
            hit = jnp.logical_and(gscore[g] == gm, jnp.logical_not(found))
            found = jnp.logical_or(found, hit)
            gsel[g] = jnp.logical_or(gsel[g], hit)
            gscore[g] = jnp.where(hit, -jnp.inf, gscore[g])
    masked = jnp.concatenate(
        [jnp.where(jnp.broadcast_to(gsel[g], groups[g].shape), groups[g], -jnp.inf)
         for g in range(N_GROUPS)], axis=0)
    ei = lax.broadcasted_iota(I32, masked.shape, 0)
    hits = []
    wk = []
    for _ in range(TOPK):
        m = jnp.max(masked, axis=0, keepdims=True)
        first = jnp.min(jnp.where(masked == m, ei, N_EXPERTS), axis=0, keepdims=True)
        hit = ei == first
        hits.append(hit)
        wk.append(jnp.sum(jnp.where(hit, scores, 0.0), axis=0, keepdims=True))
        masked = jnp.where(hit, -jnp.inf, masked)
    wsum = wk[0]
    for w in wk[1:]:
        wsum = wsum + w
    wgt_ref[...] = jnp.concatenate([w / wsum * ROUTED_SCALE for w in wk], axis=0)

    chosen = hits[0]
    for hit in hits[1:]:
        chosen = jnp.logical_or(chosen, hit)
    chosen = chosen.astype(BF16)
    r = lax.broadcasted_iota(I32, (tc, tc), 0)
    c = lax.broadcasted_iota(I32, (tc, tc), 1)
    rank = _bdot(chosen, (r <= c).astype(BF16))
    er = lax.broadcasted_iota(I32, (N_EXPERTS, N_EXPERTS), 0)
    ec = lax.broadcasted_iota(I32, (N_EXPERTS, N_EXPERTS), 1)
    below = _bdot((ec < er).astype(BF16), chosen)
    off = jnp.sum(below, axis=1, keepdims=True)
    cnt = rank[:, tc - 1:tc]
    pos = (off + rank - 1.0) * ROW_VREGS
    slot_ref[...] = jnp.concatenate(
        [jnp.sum(jnp.where(hit, pos, 0.0), axis=0, keepdims=True) for hit in hits], axis=0).astype(I32)
    off_ref[...] = jnp.broadcast_to(off, off_ref.shape).astype(I32)
    cnt_ref[...] = jnp.broadcast_to(cnt, cnt_ref.shape).astype(I32)


def _router(x, wr_hi, wr_lo, b_col, tc):
    t, d = x.shape
    n = t // tc
    return pl.pallas_call(
        _router_kernel,
        grid=(n,),
        in_specs=[pl.BlockSpec((tc, d), lambda i: (i, 0)),
                  pl.BlockSpec((N_EXPERTS, d), lambda i: (0, 0)),
                  pl.BlockSpec((N_EXPERTS, d), lambda i: (0, 0)),
                  pl.BlockSpec((N_EXPERTS, 1), lambda i: (0, 0))],
        out_specs=[pl.BlockSpec((None, TOPK, tc), lambda i: (i, 0, 0)),
                   pl.BlockSpec((None, TOPK, tc), lambda i: (i, 0, 0)),
                   pl.BlockSpec((None, N_EXPERTS, LANES), lambda i: (i, 0, 0)),
                   pl.BlockSpec((None, N_EXPERTS, LANES), lambda i: (i, 0, 0))],
        out_shape=[jax.ShapeDtypeStruct((n, TOPK, tc), I32),
                   jax.ShapeDtypeStruct((n, TOPK, tc), F32),
                   jax.ShapeDtypeStruct((n, N_EXPERTS, LANES), I32),
                   jax.ShapeDtypeStruct((n, N_EXPERTS, LANES), I32)],
        compiler_params=_params(("parallel",)),
        name="router",
    )(x, wr_hi, wr_lo, b_col)


def _experts_kernel(off_ref, cnt_ref, slot_ref, wgt_ref, x_ref, wg_ref, wu_ref, wd_ref,
                    o_ref, sorted_ref):
    tc = x_ref.shape[0]
    ci = pl.program_id(0)
    e = pl.program_id(1)
    tok_per_line = LANES // TOPK

    @pl.when(e == 0)
    def _dispatch():
        for kc in range(ROW_VREGS):
            o_ref[pl.ds(kc, tc, stride=ROW_VREGS), :] = x_ref[:, kc * LANES:(kc + 1) * LANES]
        sorted_ref[pl.ds(TOPK * tc * ROW_VREGS, EXP_BLOCK * ROW_VREGS), :] = jnp.zeros(
            (EXP_BLOCK * ROW_VREGS, LANES), F32)

        def body(i, carry):
            for u in range(tok_per_line):
                t = i * tok_per_line + u
                row = o_ref[pl.ds(pl.multiple_of(t * ROW_VREGS, ROW_VREGS), ROW_VREGS), :]
                for k in range(TOPK):
                    s = slot_ref[0, i, u * TOPK + k]
                    sorted_ref[pl.ds(pl.multiple_of(s, ROW_VREGS), ROW_VREGS), :] = row
            return carry

        lax.fori_loop(0, tc // tok_per_line, body, 0)

    base = ci * N_EXPERTS + e * EXPERTS_PER_STEP
    offs = [off_ref[base + j] for j in range(EXPERTS_PER_STEP)]
    cnts = [cnt_ref[base + j] for j in range(EXPERTS_PER_STEP)]
    nblks = [lax.shift_right_logical(cnt + (EXP_BLOCK - 1), EXP_BLOCK_LOG2) for cnt in cnts]
    nmax = nblks[0]
    for nb in nblks[1:]:
        nmax = jnp.maximum(nmax, nb)
    rowid = lax.broadcasted_iota(I32, (EXP_BLOCK, 1), 0)

    def ffn_blocks(i, carry):
        loaded = []
        for j in range(EXPERTS_PER_STEP):
            active = i < nblks[j]
            row0 = jnp.where(active, offs[j] + i * EXP_BLOCK, TOPK * tc)
            nvalid = jnp.where(active, cnts[j] - i * EXP_BLOCK, 0)
            r0 = row0 * ROW_VREGS
            xb = jnp.concatenate(
                [sorted_ref[pl.ds(r0 + kc, EXP_BLOCK, stride=ROW_VREGS), :] for kc in range(ROW_VREGS)], axis=1)
            loaded.append((r0, nvalid, xb))
        outs = []
        for j, (r0, nvalid, xb) in enumerate(loaded):
            xbb = xb.astype(BF16)
            h = _silu(_bdot(xbb, wg_ref[j])) * _bdot(xbb, wu_ref[j])
            y = _bdot(h.astype(BF16), wd_ref[j])
            outs.append(jnp.where(rowid < nvalid, y, xb))
        for (r0, _, _), y in zip(loaded, outs):
            for kc in range(ROW_VREGS):
                sorted_ref[pl.ds(r0 + kc, EXP_BLOCK, stride=ROW_VREGS), :] = y[:, kc * LANES:(kc + 1) * LANES]
        return carry

    lax.fori_loop(0, nmax, ffn_blocks, 0)

    @pl.when(e == pl.num_programs(1) - 1)
    def _combine():
        def body(i, carry):
            for u in range(tok_per_line):
                t = i * tok_per_line + u
                acc = None
                for k in range(TOPK):
                    s = slot_ref[0, i, u * TOPK + k]
                    term = wgt_ref[0, i, u * TOPK + k] * sorted_ref[
                        pl.ds(pl.multiple_of(s, ROW_VREGS), ROW_VREGS), :]
                    acc = term if acc is None else acc + term
                o_ref[pl.ds(pl.multiple_of(t * ROW_VREGS, ROW_VREGS), ROW_VREGS), :] = acc
            return carry

        lax.fori_loop(0, tc // tok_per_line, body, 0)


def _routed_experts(x, slot, wgt, seg_off, seg_cnt, wg, wu, wd, tc):
    t, d = x.shape
    n = t // tc
    sorted_rows = TOPK * tc + EXP_BLOCK
    grid_spec = pltpu.PrefetchScalarGridSpec(
        num_scalar_prefetch=2,
        grid=(n, N_EXPERTS // EXPERTS_PER_STEP),
        in_specs=[pl.BlockSpec((1,) + slot.shape[1:], lambda c, e, *_: (c, 0, 0), memory_space=pltpu.SMEM),
                  pl.BlockSpec((1,) + wgt.shape[1:], lambda c, e, *_: (c, 0, 0), memory_space=pltpu.SMEM),
                  pl.BlockSpec((tc, d), lambda c, e, *_: (c, 0), pipeline_mode=pl.Buffered(1)),
                  pl.BlockSpec((EXPERTS_PER_STEP, d, FF_EXPERT), lambda c, e, *_: (e, 0, 0)),
                  pl.BlockSpec((EXPERTS_PER_STEP, d, FF_EXPERT), lambda c, e, *_: (e, 0, 0)),
                  pl.BlockSpec((EXPERTS_PER_STEP, FF_EXPERT, d), lambda c, e, *_: (e, 0, 0))],
        out_specs=pl.BlockSpec((tc * ROW_VREGS, LANES), lambda c, e, *_: (c, 0)),
        scratch_shapes=[pltpu.VMEM((sorted_rows * ROW_VREGS, LANES), F32)],
    )
    return pl.pallas_call(
        _experts_kernel,
        grid_spec=grid_spec,
        out_shape=jax.ShapeDtypeStruct((t * ROW_VREGS, LANES), F32),
        compiler_params=_params(("arbitrary", "arbitrary"), VMEM_LIMIT_BIG),
        name="routed_experts",
    )(seg_off, seg_cnt, slot, wgt, x, wg, wu, wd)


def _shared_ln_kernel(x_ref, r_ref, wgu_ref, wd_ref, g_ref, b_ref, o_ref):
    tm = x_ref.shape[0]
    x = x_ref[...]
    gu = _bdot(x.astype(BF16), wgu_ref[...])
    ff = wgu_ref.shape[1] // 2
    h = _silu(gu[:, :ff]) * gu[:, ff:]
    shared = _bdot(h.astype(BF16), wd_ref[...])
    routed = jnp.concatenate(
        [r_ref[pl.ds(kc, tm, stride=ROW_VREGS), :] for kc in range(ROW_VREGS)], axis=1)
    z = DEEPNORM_ALPHA * x + (shared + routed)
    o_ref[...] = _layer_norm_rows(z, g_ref[...], b_ref[...])


def _shared_ln(x, routed_rows, wgu, wd, g, b, tm):
    t, d = x.shape
    return pl.pallas_call(
        _shared_ln_kernel,
        grid=(t // tm,),
        in_specs=[pl.BlockSpec((tm, d), lambda i: (i, 0)),
                  pl.BlockSpec((tm * ROW_VREGS, LANES), lambda i: (i, 0)),
                  pl.BlockSpec(wgu.shape, lambda i: (0, 0)),
                  pl.BlockSpec(wd.shape, lambda i: (0, 0)),
                  pl.BlockSpec((1, d), lambda i: (0, 0)),
                  pl.BlockSpec((1, d), lambda i: (0, 0))],
        out_specs=pl.BlockSpec((tm, d), lambda i: (i, 0)),
        out_shape=jax.ShapeDtypeStruct((t, d), F32),
        compiler_params=_params(("parallel",)),
        name="shared_ln",
    )(x, routed_rows, wgu, wd, g.reshape(1, d), b.reshape(1, d))


def _moe_ln(x, p, g, b, tc):
    slot, wgt, off, cnt = _router(x, p["wr_hi"], p["wr_lo"], p["b_col"], tc)
    lines = lambda a: a.transpose(0, 2, 1).reshape(a.shape[0], tc * TOPK // LANES, LANES)
    routed = _routed_experts(x, lines(slot), lines(wgt), off[:, :, 0].reshape(-1), cnt[:, :, 0].reshape(-1),
                             p["wg"], p["wu"], p["wd"], tc)
    return _shared_ln(x, routed, p["wsgu"], p["wsd"], g, b, min(tc, 512))


def _cumsum_kernel(x_ref, o_ref):
    n = x_ref.shape[0]
    blk = LANES
    r = lax.broadcasted_iota(I32, (blk, blk), 0)
    c = lax.broadcasted_iota(I32, (blk, blk), 1)
    tri = (c <= r).astype(BF16)

    def body(i, carry):
        start = pl.multiple_of(i * blk, blk)
        hi, mid, lo = _split3(x_ref[pl.ds(start, blk), :])
        within = (_bdot(tri, lo) + _bdot(tri, mid)) + _bdot(tri, hi)
        o_ref[pl.ds(start, blk), :] = within + carry
        return carry + jnp.sum(x_ref[pl.ds(start, blk), :], axis=0, keepdims=True)

    lax.fori_loop(0, n // blk, body, jnp.zeros((1, x_ref.shape[1]), F32))


def _cumsum_rows(x, batch, seq):
    return pl.pallas_call(
        _cumsum_kernel,
        grid=(batch,),
        in_specs=[pl.BlockSpec((seq, x.shape[1]), lambda b: (b, 0))],
        out_specs=pl.BlockSpec((seq, x.shape[1]), lambda b: (b, 0)),
        out_shape=jax.ShapeDtypeStruct(x.shape, F32),
        compiler_params=_params(("parallel",)),
        name="logf_cumsum",
    )(x)


def _fox_prompt_kernel(q_ref, k_ref, v_ref, cumq_ref, cumk_ref, o_ref, *, tq):
    h = pl.program_id(1)
    qi = pl.program_id(2)
    q = (q_ref[...].astype(F32) * (FOX_DH ** -0.5)).astype(BF16)
    lane = lax.broadcasted_iota(I32, cumq_ref.shape, 1)
    cq = jnp.sum(jnp.where(lane == h, cumq_ref[...], 0.0), axis=1, keepdims=True)

    def block(kb, carry, diagonal):
        m, acc = carry
        start = pl.multiple_of(kb * tq, tq)
        k = k_ref[pl.ds(start, tq), :]
        v = v_ref[pl.ds(start, tq), :]
        ck = cumk_ref[:, pl.ds(start, tq)]
        s = lax.dot_general(q, k, _NT, preferred_element_type=F32) + cq - ck
        if diagonal:
            qpos = lax.broadcasted_iota(I32, (tq, 1), 0)
            kpos = lax.broadcasted_iota(I32, (1, tq), 1)
            s = jnp.where(kpos <= qpos, s, NEG_INF)
        m_new = jnp.maximum(m, jnp.max(s, axis=1, keepdims=True))
        a = jnp.exp(m - m_new)
        p = jnp.exp(s - m_new)
        return m_new, a * acc + _bdot(p.astype(BF16), v)

    init = (jnp.full((tq, 1), -jnp.inf, F32), jnp.zeros((tq, v_ref.shape[1]), F32))
    carry = lax.fori_loop(0, qi, lambda kb, c: block(kb, c, False), init)
    _, acc = block(qi, carry, True)
    o_ref[...] = (acc[:, :FOX_DH] / acc[:, FOX_DH:FOX_DH + 1]).astype(o_ref.dtype)


def _fox_prompt(q, k, v, cum_tok, cum_row, tq):
    b, h, l, dh = q.shape
    nq = l // tq
    kern = functools.partial(_fox_prompt_kernel, tq=tq)
    return pl.pallas_call(
        kern,
        grid=(b, h, nq),
        in_specs=[pl.BlockSpec((None, None, tq, dh), lambda bi, hi, qi: (bi, hi, qi, 0)),
                  pl.BlockSpec((None, None, l, dh), lambda bi, hi, qi: (bi, hi, 0, 0)),
                  pl.BlockSpec((None, None, l, v.shape[-1]), lambda bi, hi, qi: (bi, hi, 0, 0)),
                  pl.BlockSpec((tq, LANES), lambda bi, hi, qi: (bi * nq + qi, 0)),
                  pl.BlockSpec((None, None, 1, l), lambda bi, hi, qi: (bi, hi, 0, 0))],
        out_specs=pl.BlockSpec((None, None, tq, dh), lambda bi, hi, qi: (bi, hi, qi, 0)),
        out_shape=jax.ShapeDtypeStruct((b, h, l, dh), BF16),
        compiler_params=_params(("parallel", "parallel", "arbitrary")),
        name="fox_prompt",
    )(q, k, v, cum_tok, cum_row)


N_QROWS = 128
PAGES_PER_STEP = 4
TAIL_ONES_ROW0 = 3 * FOX_HEADS


def _fox_sample_kernel(pt_ref, q_ref, lfo_ref, kto_ref, vto_ref, *rest, n_steps, dec_seq):
    del pt_ref
    cache_refs = rest[:3 * PAGES_PER_STEP]
    o_ref = rest[3 * PAGES_PER_STEP]
    qbd_ref, qtail_ref, m_ref, l_ref, acc_ref, carry_ref = rest[3 * PAGES_PER_STEP + 1:]
    step = pl.program_id(1)
    n_iota = lax.broadcasted_iota(I32, (N_QROWS, 1), 0)
    row_head = n_iota // dec_seq
    row_q = n_iota % dec_seq
    pos = lax.broadcasted_iota(I32, (1, PAGE_SIZE), 1)
    r = lax.broadcasted_iota(I32, (PAGE_SIZE, PAGE_SIZE), 0)
    c = lax.broadcasted_iota(I32, (PAGE_SIZE, PAGE_SIZE), 1)

    def lane_sums(x_t, sel):
        hi, mid, lo = _split3(x_t)
        return (_bdot(lo, sel) + _bdot(mid, sel)) + _bdot(hi, sel)

    def attend(kt, vt, key_bias_t, valid):
        hi, mid, lo = _split3(key_bias_t)
        ones_rows = (lax.broadcasted_iota(I32, (SUBLANES, PAGE_SIZE), 0) < 3).astype(F32)
        ktail = jnp.concatenate(
            [hi.astype(F32), mid.astype(F32), lo.astype(F32), ones_rows,
             jnp.zeros((LANES - TAIL_ONES_ROW0 - SUBLANES, PAGE_SIZE), F32)], axis=0).astype(BF16)
        s = _bdot(qbd_ref[...], kt.astype(BF16)) + _bdot(qtail_ref[...], ktail)
        if valid is not None:
            s = jnp.where(valid, s, NEG_INF)
        m_old = m_ref[...]
        m_new = jnp.maximum(m_old, jnp.max(s, axis=1, keepdims=True))
        a = jnp.exp(m_old - m_new)
        p = jnp.exp(s - m_new)
        l_ref[...] = a * l_ref[...] + jnp.sum(p, axis=1, keepdims=True)
        acc_ref[...] = a * acc_ref[...] + lax.dot_general(
            p.astype(BF16), vt.astype(BF16), _NT, preferred_element_type=F32)
        m_ref[...] = m_new

    @pl.when(step == 0)
    def _own_rows():
        q = q_ref[0] * (FOX_DH ** -0.5)
        qt = jnp.concatenate([q] * FOX_HEADS, axis=0)
        rr = lax.broadcasted_iota(I32, qt.shape, 0)
        cc = lax.broadcasted_iota(I32, qt.shape, 1)
        qbd_ref[...] = jnp.where(cc // FOX_DH == rr // dec_seq, qt, 0.0).astype(BF16)
        prefix_t = lane_sums(lfo_ref[0], (r <= c).astype(BF16))
        rep = (lax.broadcasted_iota(I32, (N_QROWS, FOX_HEADS), 1)
               == lax.broadcasted_iota(I32, (N_QROWS, FOX_HEADS), 0) // dec_seq).astype(BF16)
        hi, mid, lo = _split3(prefix_t)
        by_row = (_bdot(rep, lo) + _bdot(rep, mid)) + _bdot(rep, hi)
        pre_col = jnp.sum(jnp.where(pos == row_q, by_row, 0.0), axis=1, keepdims=True)
        hi, mid, lo = (x.astype(F32) for x in _split3(pre_col))
        lane = lax.broadcasted_iota(I32, (N_QROWS, LANES), 1)
        tail = jnp.where(jnp.logical_and(lane < TAIL_ONES_ROW0, lane % FOX_HEADS == row_head), 1.0, 0.0)
        tail = jnp.where(lane == TAIL_ONES_ROW0, hi, tail)
        tail = jnp.where(lane == TAIL_ONES_ROW0 + 1, mid, tail)
        tail = jnp.where(lane == TAIL_ONES_ROW0 + 2, lo, tail)
        qtail_ref[...] = tail.astype(BF16)
        m_ref[...] = jnp.full(m_ref.shape, -jnp.inf, F32)
        l_ref[...] = jnp.zeros(l_ref.shape, F32)
        acc_ref[...] = jnp.zeros(acc_ref.shape, F32)
        carry_ref[...] = jnp.zeros(carry_ref.shape, F32)
        valid = jnp.logical_and(pos <= row_q, pos < dec_seq)
        attend(kto_ref[0], vto_ref[0], -prefix_t, valid)

    @pl.when(step > 0)
    def _cache_pages():
        after = (r > c).astype(BF16)
        for u in range(PAGES_PER_STEP):
            kt_ref, vt_ref, lf_ref = cache_refs[3 * u:3 * u + 3]
            lf_t = lf_ref[0]
            carry = carry_ref[...]
            attend(kt_ref[0], vt_ref[0], lane_sums(lf_t, after) + carry, None)
            carry_ref[...] = carry + jnp.sum(lf_t, axis=1, keepdims=True)

    @pl.when(step == n_steps - 1)
    def _finish():
        col_head = lax.broadcasted_iota(I32, (dec_seq, D_MODEL), 1) // FOX_DH
        out = jnp.zeros((dec_seq, D_MODEL), F32)
        for h in range(FOX_HEADS):
            rows = slice(h * dec_seq, (h + 1) * dec_seq)
            out = out + jnp.where(col_head == h, acc_ref[rows, :] / l_ref[rows, :], 0.0)
        o_ref[0] = out


def _fox_sample(q, lft_own, kt_own, vt_own, kt_cache, vt_cache, lft_cache, page_table):
    db, dec_seq, _ = q.shape
    n_pages = page_table.shape[1]
    n_steps = 1 + n_pages // PAGES_PER_STEP
    fd = FOX_HEADS * FOX_DH
    kern = functools.partial(_fox_sample_kernel, n_steps=n_steps, dec_seq=dec_seq)

    def page(u):
        return lambda b, s, pt: (pt[b, n_pages - 1 - (jnp.maximum(s, 1) - 1) * PAGES_PER_STEP - u], 0, 0)

    cache_specs = []
    cache_args = []
    for u in range(PAGES_PER_STEP):
        cache_specs += [pl.BlockSpec((1, fd, PAGE_SIZE), page(u)),
                        pl.BlockSpec((1, fd, PAGE_SIZE), page(u)),
                        pl.BlockSpec((1, FOX_HEADS, PAGE_SIZE), page(u))]
        cache_args += [kt_cache, vt_cache, lft_cache]
    grid_spec = pltpu.PrefetchScalarGridSpec(
        num_scalar_prefetch=1,
        grid=(db, n_steps),
        in_specs=[pl.BlockSpec((1, dec_seq, D_MODEL), lambda b, s, pt: (b, 0, 0)),
                  pl.BlockSpec((1, FOX_HEADS, PAGE_SIZE), lambda b, s, pt: (b, 0, 0)),
                  pl.BlockSpec((1, fd, PAGE_SIZE), lambda b, s, pt: (b, 0, 0)),
                  pl.BlockSpec((1, fd, PAGE_SIZE), lambda b, s, pt: (b, 0, 0))] + cache_specs,
        out_specs=pl.BlockSpec((1, dec_seq, D_MODEL), lambda b, s, pt: (b, 0, 0)),
        scratch_shapes=[pltpu.VMEM((N_QROWS, fd), BF16),
                        pltpu.VMEM((N_QROWS, LANES), BF16),
                        pltpu.VMEM((N_QROWS, 1), F32),
                        pltpu.VMEM((N_QROWS, 1), F32),
                        pltpu.VMEM((N_QROWS, fd), F32),
                        pltpu.VMEM((FOX_HEADS, 1), F32)],
    )
    return pl.pallas_call(
        kern,
        grid_spec=grid_spec,
        out_shape=jax.ShapeDtypeStruct((db, dec_seq, D_MODEL), F32),
        compiler_params=_params(("parallel", "arbitrary")),
        name="fox_sample",
    )(page_table, q, lft_own, kt_own, vt_own, *cache_args)


def _moe_params(layer, w_router, b_router, w_gate, w_up, w_down, ws_gate, ws_up, ws_down):
    wr_t = w_router[layer].T
    wr_hi = wr_t.astype(BF16)
    wr_lo = (wr_t - wr_hi.astype(F32)).astype(BF16)
    return dict(
        wr_hi=wr_hi, wr_lo=wr_lo, b_col=b_router[layer].reshape(N_EXPERTS, 1),
        wg=w_gate[layer].astype(BF16), wu=w_up[layer].astype(BF16), wd=w_down[layer].astype(BF16),
        wsgu=jnp.concatenate([ws_gate[layer], ws_up[layer]], axis=1).astype(BF16),
        wsd=ws_down[layer].astype(BF16))


def kernel(x_prompt, x_sample, state_ret, cache_k, cache_v, cache_logf, page_table, ret_w_in, ret_w_out,
           fox_w_kvf, fox_b_f, fox_w_q, fox_w_out, ln_g, ln_b, moe_w_router, moe_b_router, moe_w_gate,
           moe_w_up, moe_w_down, moe_ws_gate, moe_ws_up, moe_ws_down):
    bp, lp, d = x_prompt.shape
    bs, ls, _ = x_sample.shape
    tp, ts = bp * lp, bs * ls
    assert fox_heads_rows(ls) == N_QROWS
    xp = x_prompt.reshape(tp, d)
    xs = x_sample.reshape(ts, d)
    hq = RET_HEADS * RET_DK
    hv = RET_HEADS * RET_DV
    fd = FOX_HEADS * FOX_DH
    tm_p = 512
    tc_p = 1024

    moe = [_moe_params(layer, moe_w_router, moe_b_router, moe_w_gate, moe_w_up, moe_w_down,
                       moe_ws_gate, moe_ws_up, moe_ws_down) for layer in range(DEPTH)]

    w_in = ret_w_in[0].astype(BF16)
    w_out = ret_w_out[0].astype(BF16)
    cos_p, sin_p = _rope_tables(jnp.arange(lp))
    cos_s, sin_s = _rope_tables(PAST_LEN + jnp.arange(ls))
    cos_s, sin_s = jnp.tile(cos_s, (bs, 1)), jnp.tile(sin_s, (bs, 1))

    qk_p = _qk_rope(xp, w_in, cos_p, sin_p, tm_p)
    v_p = _matmul(xp, w_in, 2 * hq, hv, BF16, tm_p, 512)
    g_p = _matmul(xp, w_in, 2 * hq + hv, hv, F32, tm_p, 512)
    o_p, st_p = _retention(qk_p, v_p, g_p, jnp.zeros((bp, RET_HEADS, RET_DK, RET_DV), F32),
                           _retention_decays(min(RET_CHUNK, lp)), bp, lp, BF16)
    xp = _mm_res_ln(o_p, w_out, xp, ln_g[0, 0], ln_b[0, 0], tm_p)

    qk_s = _qk_rope(xs, w_in, cos_s, sin_s, ts)
    v_s = _matmul(xs, w_in, 2 * hq, hv, F32, ts, 512)
    g_s = _matmul(xs, w_in, 2 * hq + hv, hv, F32, ts, 512)
    o_s, st_s = _retention(qk_s, v_s, g_s, state_ret[0], _retention_decays(min(RET_CHUNK, ls)), bs, ls, F32)
    xs = _mm_res_ln(o_s, w_out, xs, ln_g[0, 0], ln_b[0, 0], ts)

    xp = _moe_ln(xp, moe[0], ln_g[0, 1], ln_b[0, 1], tc_p)
    xs = _moe_ln(xs, moe[0], ln_g[0, 1], ln_b[0, 1], ts)

    w_kvf = fox_w_kvf.astype(BF16)
    w_f_pad = jnp.zeros((d, LANES), BF16).at[:, :FOX_HEADS].set(w_kvf[:, 2 * fd:])
    b_f_pad = jnp.zeros((1, LANES), F32).at[0, :FOX_HEADS].set(fox_b_f)
    kv_p = _matmul(xp, w_kvf, 0, 2 * fd, F32, tm_p, 512)
    lf_p = _logf_proj(xp, w_f_pad, b_f_pad, tm_p)
    kv_s = _matmul(xs, w_kvf, 0, 2 * fd, F32, ts, 512)
    lf_s = _logf_proj(xs, w_f_pad, b_f_pad, ts)

    k_p = kv_p[:, :fd].reshape(bp, lp, FOX_HEADS, FOX_DH)
    v_p4 = kv_p[:, fd:].reshape(bp, lp, FOX_HEADS, FOX_DH)
    k_s = kv_s[:, :fd].reshape(bs, ls, FOX_HEADS, FOX_DH)
    v_s4 = kv_s[:, fd:].reshape(bs, ls, FOX_HEADS, FOX_DH)

    w_q = fox_w_q[0].astype(BF16)
    w_o = fox_w_out[0].astype(BF16)
    q_p = _matmul(xp, w_q, 0, fd, BF16, tm_p, 512)
    heads = lambda a: a.reshape(bp, lp, FOX_HEADS, FOX_DH).transpose(0, 2, 1, 3)
    cum_p = _cumsum_rows(lf_p, bp, lp)
    cum_row = cum_p[:, :FOX_HEADS].reshape(bp, lp, FOX_HEADS).transpose(0, 2, 1).reshape(bp, FOX_HEADS, 1, lp)
    v_heads = heads(kv_p[:, fd:].astype(BF16))
    v_ones = jnp.concatenate([v_heads, jnp.ones(v_heads.shape[:3] + (1,), BF16),
                              jnp.zeros(v_heads.shape[:3] + (LANES - FOX_DH - 1,), BF16)], axis=-1)
    att_p = _fox_prompt(heads(q_p), heads(kv_p[:, :fd].astype(BF16)), v_ones, cum_p, cum_row, 512)
    att_p = att_p.transpose(0, 2, 1, 3).reshape(tp, fd)
    xp = _mm_res_ln(att_p, w_o, xp, ln_g[1, 0], ln_b[1, 0], tm_p)

    q_s = _matmul(xs, w_q, 0, fd, F32, ts, 512)
    n_pool = cache_k.shape[0]
    kt_cache = cache_k.transpose(0, 2, 3, 1).reshape(n_pool, fd, PAGE_SIZE)
    vt_cache = cache_v.transpose(0, 2, 3, 1).reshape(n_pool, fd, PAGE_SIZE)
    lft_cache = cache_logf.transpose(0, 2, 1)
    own_t = lambda a, w: jnp.pad(a.reshape(bs, ls, w).transpose(0, 2, 1), ((0, 0), (0, 0), (0, PAGE_SIZE - ls)))
    att_s = _fox_sample(q_s.reshape(bs, ls, d), own_t(lf_s[:, :FOX_HEADS], FOX_HEADS), own_t(kv_s[:, :fd], fd),
                        own_t(kv_s[:, fd:], fd), kt_cache, vt_cache, lft_cache, page_table)
    xs = _mm_res_ln(att_s.reshape(ts, fd), w_o, xs, ln_g[1, 0], ln_b[1, 0], ts)

    xp = _moe_ln(xp, moe[1], ln_g[1, 1], ln_b[1, 1], tc_p)
    xs = _moe_ln(xs, moe[1], ln_g[1, 1], ln_b[1, 1], ts)

    return (xp.reshape(bp, lp, d), xs.reshape(bs, ls, d), st_p[None], st_s[None],
            k_p, v_p4, lf_p[:, :FOX_HEADS].reshape(bp, lp, FOX_HEADS),
            k_s, v_s4, lf_s[:, :FOX_HEADS].reshape(bs, ls, FOX_HEADS))


def fox_heads_rows(dec_seq):
    return FOX_HEADS * dec_seq
```

```python
import functools

import jax
import jax.numpy as jnp
from jax import lax
from jax.experimental import pallas as pl
from jax.experimental.pallas import tpu as pltpu

F32 = jnp.float32
BF16 = jnp.bfloat16
I32 = jnp.int32

D_MODEL = 1024
DEPTH = 2
PAST_LEN = 8192
PAGE_SIZE = 128
N_A_LAYERS = DEPTH // 2
RET_HEADS = 4
RET_DK = D_MODEL // RET_HEADS
RET_DV = 2 * RET_DK
RET_CHUNK = 128
ROPE_BASE = 10000.0
FOX_HEADS = 16
FOX_DH = D_MODEL // FOX_HEADS
NEG_INF = -1e30
N_EXPERTS = 64
TOPK = 8
N_GROUPS = 8
GROUP_SIZE = N_EXPERTS // N_GROUPS
TOPK_GROUPS = 4
FF_EXPERT = D_MODEL // 4
ROUTED_SCALE = 2.5
DEEPNORM_ALPHA = (2.0 * DEPTH) ** 0.25
LN_EPS = 1e-5
GN_EPS = 1e-5

LANES = 128
SUBLANES = 8
ROW_VREGS = D_MODEL // LANES
EXP_BLOCK = 128
EXP_BLOCK_LOG2 = EXP_BLOCK.bit_length() - 1
EXPERTS_PER_STEP = 2
VMEM_LIMIT_BIG = 60 * 1024 * 1024
VMEM_LIMIT = 48 * 1024 * 1024

PROMPT_TILE_ROWS = 1024
PROJ_TILE_COLS = 1024
MOE_TOKEN_TILE = 1024
ATTN_TILE = 512

_NT = (((1,), (1,)), ((), ()))
_TN = (((0,), (0,)), ((), ()))


def _params(sem, vmem=VMEM_LIMIT):
    return pltpu.CompilerParams(dimension_semantics=sem, vmem_limit_bytes=vmem)


def _bdot(a, b):
    return jnp.dot(a, b, preferred_element_type=F32)


def _layer_norm_rows(z, g, b):
    mu = jnp.mean(z, axis=-1, keepdims=True)
    zc = z - mu
    var = jnp.mean(zc * zc, axis=-1, keepdims=True)
    return zc * lax.rsqrt(var + LN_EPS) * g + b


def _silu(x):
    return x * jax.nn.sigmoid(x)


def _split3(x):
    hi = x.astype(BF16)
    r1 = x - hi.astype(F32)
    mid = r1.astype(BF16)
    lo = (r1 - mid.astype(F32)).astype(BF16)
    return hi, mid, lo


def _mm_kernel(x_ref, w_ref, o_ref):
    o_ref[...] = _bdot(x_ref[...].astype(BF16), w_ref[...]).astype(o_ref.dtype)


def _matmul(x, w, col0, n, out_dtype, tm, tn):
    m, k = x.shape
    j0 = col0 // tn
    return pl.pallas_call(
        _mm_kernel,
        grid=(m // tm, n // tn),
        in_specs=[pl.BlockSpec((tm, k), lambda i, j: (i, 0)),
                  pl.BlockSpec((k, tn), lambda i, j: (0, j + j0))],
        out_specs=pl.BlockSpec((tm, tn), lambda i, j: (i, j)),
        out_shape=jax.ShapeDtypeStruct((m, n), out_dtype),
        compiler_params=_params(("parallel", "arbitrary")),
        name="proj",
    )(x, w)


def _mm_t_kernel(x_ref, wt_ref, o_ref):
    o_ref[...] = lax.dot_general(wt_ref[...], x_ref[...].astype(BF16), _NT, preferred_element_type=F32)


def _matmul_t(x, w_t, batch, tm, tn):
    m, k = x.shape
    n = w_t.shape[0]
    seq = m // batch
    nper = seq // tm
    return pl.pallas_call(
        _mm_t_kernel,
        grid=(m // tm, n // tn),
        in_specs=[pl.BlockSpec((tm, k), lambda i, j: (i, 0)),
                  pl.BlockSpec((tn, k), lambda i, j: (j, 0))],
        out_specs=pl.BlockSpec((None, tn, tm), lambda i, j: (i // nper, j, i % nper)),
        out_shape=jax.ShapeDtypeStruct((batch, n, seq), F32),
        compiler_params=_params(("parallel", "arbitrary")),
        name="proj_t",
    )(x, w_t)


def _mm_rope_kernel(x_ref, w_ref, cos_ref, sin_ref, o_ref, *, k_scale):
    j = pl.program_id(1)
    acc = _bdot(x_ref[...].astype(BF16), w_ref[...])
    half = RET_DK // 2
    c = cos_ref[...]
    s = sin_ref[...]
    sc = jnp.where(j == 0, 1.0, k_scale).astype(F32)
    for h in range(RET_HEADS):
        lo = h * RET_DK
        x1 = acc[:, lo:lo + half]
        x2 = acc[:, lo + half:lo + RET_DK]
        o_ref[:, lo:lo + half] = (x1 * c - x2 * s) * sc
        o_ref[:, lo + half:lo + RET_DK] = (x2 * c + x1 * s) * sc


def _qk_rope(x, w, cos_rows, sin_rows, tm):
    m, k = x.shape
    hq = RET_HEADS * RET_DK
    nper = cos_rows.shape[0] // tm
    kern = functools.partial(_mm_rope_kernel, k_scale=RET_DK ** -0.5)
    return pl.pallas_call(
        kern,
        grid=(m // tm, 2),
        in_specs=[pl.BlockSpec((tm, k), lambda i, j: (i, 0)),
                  pl.BlockSpec((k, hq), lambda i, j: (0, j)),
                  pl.BlockSpec((tm, RET_DK // 2), lambda i, j: (i % nper, 0)),
                  pl.BlockSpec((tm, RET_DK // 2), lambda i, j: (i % nper, 0))],
        out_specs=pl.BlockSpec((tm, hq), lambda i, j: (i, j)),
        out_shape=jax.ShapeDtypeStruct((m, 2 * hq), F32),
        compiler_params=_params(("parallel", "arbitrary")),
        name="qk_rope",
    )(x, w, cos_rows, sin_rows)


def _mm_res_ln_kernel(a_ref, w_ref, x_ref, g_ref, b_ref, o_ref):
    mix = _bdot(a_ref[...].astype(BF16), w_ref[...])
    z = DEEPNORM_ALPHA * x_ref[...] + mix
    o_ref[...] = _layer_norm_rows(z, g_ref[...], b_ref[...])


def _mm_res_ln(a, w, x, g, b, tm):
    m, k = a.shape
    d = w.shape[1]
    return pl.pallas_call(
        _mm_res_ln_kernel,
        grid=(m // tm,),
        in_specs=[pl.BlockSpec((tm, k), lambda i: (i, 0)),
                  pl.BlockSpec((k, d), lambda i: (0, 0)),
                  pl.BlockSpec((tm, d), lambda i: (i, 0)),
                  pl.BlockSpec((1, d), lambda i: (0, 0)),
                  pl.BlockSpec((1, d), lambda i: (0, 0))],
        out_specs=pl.BlockSpec((tm, d), lambda i: (i, 0)),
        out_shape=jax.ShapeDtypeStruct((m, d), F32),
        compiler_params=_params(("parallel",)),
        name="mix_out_ln",
    )(a, w, x, g.reshape(1, d), b.reshape(1, d))


def _mm_logsig_kernel(x_ref, w_ref, b_ref, o_ref):
    z = _bdot(x_ref[...].astype(BF16), w_ref[...]) + b_ref[...]
    o_ref[...] = -(jnp.maximum(-z, 0.0) + jnp.log1p(jnp.exp(-jnp.abs(z))))


def _logf_proj(x, w_pad, b_pad, tm):
    m, k = x.shape
    return pl.pallas_call(
        _mm_logsig_kernel,
        grid=(m // tm,),
        in_specs=[pl.BlockSpec((tm, k), lambda i: (i, 0)),
                  pl.BlockSpec((k, LANES), lambda i: (0, 0)),
                  pl.BlockSpec((1, LANES), lambda i: (0, 0))],
        out_specs=pl.BlockSpec((tm, LANES), lambda i: (i, 0)),
        out_shape=jax.ShapeDtypeStruct((m, LANES), F32),
        compiler_params=_params(("parallel",)),
        name="logf_proj",
    )(x, w_pad, b_pad)


def _ret_kernel(q_ref, k_ref, v_ref, g_ref, s0_ref, din_ref, dq_ref, dk_ref, dc_ref,
                o_ref, sout_ref, state_ref):
    c = pl.program_id(1)

    @pl.when(c == 0)
    def _():
        state_ref[...] = s0_ref[...]

    for h in range(RET_HEADS):
        qcols = slice(h * RET_DK, (h + 1) * RET_DK)
        vcols = slice(h * RET_DV, (h + 1) * RET_DV)
        q = q_ref[:, qcols].astype(BF16)
        k = k_ref[:, qcols]
        v = v_ref[:, vcols].astype(BF16)
        s_prev = state_ref[h]
        scores = lax.dot_general(q, k.astype(BF16), _NT, preferred_element_type=F32) * din_ref[h]
        o = _bdot(scores.astype(BF16), v) + _bdot(q, s_prev.astype(BF16)) * dq_ref[h]
        kd = (k * dk_ref[h]).astype(BF16)
        s_new = s_prev * dc_ref[h] + lax.dot_general(kd, v, _TN, preferred_element_type=F32)
        state_ref[h] = s_new

        mu = jnp.mean(o, axis=-1, keepdims=True)
        oc = o - mu
        var = jnp.mean(oc * oc, axis=-1, keepdims=True)
        on = oc * lax.rsqrt(var + GN_EPS)
        o_ref[:, vcols] = (_silu(g_ref[:, vcols]) * on).astype(o_ref.dtype)

    @pl.when(c == pl.num_programs(1) - 1)
    def _():
        sout_ref[...] = state_ref[...]


def _retention(qk, v, g, s0, decays, batch, seq, out_dtype):
    chunk = min(RET_CHUNK, seq)
    nc = seq // chunk
    din, dq, dk, dc = decays
    h_ = RET_HEADS
    hq, hv = h_ * RET_DK, h_ * RET_DV
    whole = lambda a: pl.BlockSpec(a.shape, lambda b, c: (0,) * a.ndim)
    return pl.pallas_call(
        _ret_kernel,
        grid=(batch, nc),
        in_specs=[pl.BlockSpec((chunk, hq), lambda b, c: (b * nc + c, 0)),
                  pl.BlockSpec((chunk, hq), lambda b, c: (b * nc + c, 1)),
                  pl.BlockSpec((chunk, hv), lambda b, c: (b * nc + c, 0)),
                  pl.BlockSpec((chunk, hv), lambda b, c: (b * nc + c, 0)),
                  pl.BlockSpec((None, h_, RET_DK, RET_DV), lambda b, c: (b, 0, 0, 0)),
                  whole(din), whole(dq), whole(dk), whole(dc)],
        out_specs=[pl.BlockSpec((chunk, hv), lambda b, c: (b * nc + c, 0)),
                   pl.BlockSpec((None, h_, RET_DK, RET_DV), lambda b, c: (b, 0, 0, 0))],
        out_shape=[jax.ShapeDtypeStruct((batch * seq, hv), out_dtype),
                   jax.ShapeDtypeStruct((batch, h_, RET_DK, RET_DV), F32)],
        scratch_shapes=[pltpu.VMEM((h_, RET_DK, RET_DV), F32)],
        compiler_params=_params(("parallel", "arbitrary")),
        name="retention",
    )(qk, qk, v, g, s0, din, dq, dk, dc)


def _retention_decays(chunk):
    h = jnp.arange(RET_HEADS, dtype=F32)
    log_g = jnp.log1p(-(2.0 ** (-5.0 - h)))
    idx = jnp.arange(chunk, dtype=F32)
    rel = idx[:, None] - idx[None, :]
    din = jnp.where(rel >= 0, jnp.exp(log_g[:, None, None] * jnp.maximum(rel, 0.0)), 0.0)
    dq = jnp.exp(log_g[:, None] * (idx[None, :] + 1.0))[:, :, None]
    dk = jnp.exp(log_g[:, None] * (chunk - 1.0 - idx[None, :]))[:, :, None]
    dc = jnp.exp(log_g * chunk)[:, None, None]
    return din, dq, dk, dc


def _rope_tables(pos):
    half = RET_DK // 2
    inv_freq = ROPE_BASE ** (-jnp.arange(half, dtype=F32) / half)
    ang = pos.astype(F32)[:, None] * inv_freq[None, :]
    return jnp.cos(ang), jnp.sin(ang)


def _router_kernel(x_ref, wh_ref, wl_ref, b_ref, slot_ref, wgt_ref, off_ref, cnt_ref):
    tc = x_ref.shape[0]
    x = x_ref[...]
    xh = x.astype(BF16)
    xl = (x - xh.astype(F32)).astype(BF16)
    wh = wh_ref[...]
    wl = wl_ref[...]
    logits = (lax.dot_general(wh, xh, _NT, preferred_element_type=F32)
              + lax.dot_general(wh, xl, _NT, preferred_element_type=F32)
              + lax.dot_general(wl, xh, _NT, preferred_element_type=F32))
    scores = jax.nn.sigmoid(logits)
    biased = scores + b_ref[...]

    jj = lax.broadcasted_iota(I32, (GROUP_SIZE, tc), 0)
    groups = [biased[g * GROUP_SIZE:(g + 1) * GROUP_SIZE, :] for g in range(N_GROUPS)]
    gscore = []
    for rows in groups:
        m1 = jnp.max(rows, axis=0, keepdims=True)
        j1 = jnp.min(jnp.where(rows == m1, jj, GROUP_SIZE), axis=0, keepdims=True)
        m2 = jnp.max(jnp.where(jj == j1, -jnp.inf, rows), axis=0, keepdims=True)
        gscore.append(m1 + m2)
    gsel = [jnp.zeros((1, tc), dtype=jnp.bool_) for _ in range(N_GROUPS)]
    for _ in range(TOPK_GROUPS):
        gm = gscore[0]
        for sc in gscore[1:]:
            gm = jnp.maximum(gm, sc)
        found = jnp.zeros((1, tc), dtype=jnp.bool_)
        for g in range(N_GROUPS):
            hit = jnp.logical_and(gscore[g] == gm, jnp.logical_not(found))
            found = jnp.logical_or(found, hit)
            gsel[g] = jnp.logical_or(gsel[g], hit)
            gscore[g] = jnp.where(hit, -jnp.inf, gscore[g])
    masked = jnp.concatenate(
        [jnp.where(jnp.broadcast_to(gsel[g], groups[g].shape), groups[g], -jnp.inf)
         for g in range(N_GROUPS)], axis=0)
    ei = lax.broadcasted_iota(I32, masked.shape, 0)
    hits = []
    wk = []
    for _ in range(TOPK):
        m = jnp.max(masked, axis=0, keepdims=True)
        first = jnp.min(jnp.where(masked == m, ei, N_EXPERTS), axis=0, keepdims=True)
        hit = ei == first
        hits.append(hit)
        wk.append(jnp.sum(jnp.where(hit, scores, 0.0), axis=0, keepdims=True))
        masked = jnp.where(hit, -jnp.inf, masked)
    wsum = wk[0]
    for w in wk[1:]:
        wsum = wsum + w
    wgt_ref[...] = jnp.concatenate([w / wsum * ROUTED_SCALE for w in wk], axis=0)

    chosen = hits[0]
    for hit in hits[1:]:
        chosen = jnp.logical_or(chosen, hit)
    chosen = chosen.astype(BF16)
    r = lax.broadcasted_iota(I32, (tc, tc), 0)
    c = lax.broadcasted_iota(I32, (tc, tc), 1)
    rank = _bdot(chosen, (r <= c).astype(BF16))
    er = lax.broadcasted_iota(I32, (N_EXPERTS, N_EXPERTS), 0)
    ec = lax.broadcasted_iota(I32, (N_EXPERTS, N_EXPERTS), 1)
    below = _bdot((ec < er).astype(BF16), chosen)
    off = jnp.sum(below, axis=1, keepdims=True)
    cnt = rank[:, tc - 1:tc]
    pos = (off + rank - 1.0) * ROW_VREGS
    slot_ref[...] = jnp.concatenate(
        [jnp.sum(jnp.where(hit, pos, 0.0), axis=0, keepdims=True) for hit in hits], axis=0).astype(I32)
    off_ref[...] = jnp.broadcast_to(off, off_ref.shape).astype(I32)
    cnt_ref[...] = jnp.broadcast_to(cnt, cnt_ref.shape).astype(I32)


def _router(x, wr_hi, wr_lo, b_col, tc):
    t, d = x.shape
    n = t // tc
    return pl.pallas_call(
        _router_kernel,
        grid=(n,),
        in_specs=[pl.BlockSpec((tc, d), lambda i: (i, 0)),
                  pl.BlockSpec((N_EXPERTS, d), lambda i: (0, 0)),
                  pl.BlockSpec((N_EXPERTS, d), lambda i: (0, 0)),
                  pl.BlockSpec((N_EXPERTS, 1), lambda i: (0, 0))],
        out_specs=[pl.BlockSpec((None, TOPK, tc), lambda i: (i, 0, 0)),
                   pl.BlockSpec((None, TOPK, tc), lambda i: (i, 0, 0)),
                   pl.BlockSpec((None, N_EXPERTS, LANES), lambda i: (i, 0, 0)),
                   pl.BlockSpec((None, N_EXPERTS, LANES), lambda i: (i, 0, 0))],
        out_shape=[jax.ShapeDtypeStruct((n, TOPK, tc), I32),
                   jax.ShapeDtypeStruct((n, TOPK, tc), F32),
                   jax.ShapeDtypeStruct((n, N_EXPERTS, LANES), I32),
                   jax.ShapeDtypeStruct((n, N_EXPERTS, LANES), I32)],
        compiler_params=_params(("parallel",)),
        name="router",
    )(x, wr_hi, wr_lo, b_col)


def _experts_kernel(off_ref, cnt_ref, slot_ref, wgt_ref, x_ref, wg_ref, wu_ref, wd_ref,
                    o_ref, sorted_ref):
    tc = x_ref.shape[0]
    ci = pl.program_id(0)
    e = pl.program_id(1)
    tok_per_line = LANES // TOPK

    @pl.when(e == 0)
    def _dispatch():
        for kc in range(ROW_VREGS):
            o_ref[pl.ds(kc, tc, stride=ROW_VREGS), :] = x_ref[:, kc * LANES:(kc + 1) * LANES]
        sorted_ref[pl.ds(TOPK * tc * ROW_VREGS, EXP_BLOCK * ROW_VREGS), :] = jnp.zeros(
            (EXP_BLOCK * ROW_VREGS, LANES), F32)

        def body(i, carry):
            for u in range(tok_per_line):
                t = i * tok_per_line + u
                row = o_ref[pl.ds(pl.multiple_of(t * ROW_VREGS, ROW_VREGS), ROW_VREGS), :]
                for k in range(TOPK):
                    s = slot_ref[0, i, u * TOPK + k]
                    sorted_ref[pl.ds(pl.multiple_of(s, ROW_VREGS), ROW_VREGS), :] = row
            return carry

        lax.fori_loop(0, tc // tok_per_line, body, 0)

    base = ci * N_EXPERTS + e * EXPERTS_PER_STEP
    offs = [off_ref[base + j] for j in range(EXPERTS_PER_STEP)]
    cnts = [cnt_ref[base + j] for j in range(EXPERTS_PER_STEP)]
    nblks = [lax.shift_right_logical(cnt + (EXP_BLOCK - 1), EXP_BLOCK_LOG2) for cnt in cnts]
    nmax = nblks[0]
    for nb in nblks[1:]:
        nmax = jnp.maximum(nmax, nb)
    rowid = lax.broadcasted_iota(I32, (EXP_BLOCK, 1), 0)

    def ffn_blocks(i, carry):
        loaded = []
        for j in range(EXPERTS_PER_STEP):
            active = i < nblks[j]
            row0 = jnp.where(active, offs[j] + i * EXP_BLOCK, TOPK * tc)
            nvalid = jnp.where(active, cnts[j] - i * EXP_BLOCK, 0)
            r0 = row0 * ROW_VREGS
            xb = jnp.concatenate(
                [sorted_ref[pl.ds(r0 + kc, EXP_BLOCK, stride=ROW_VREGS), :] for kc in range(ROW_VREGS)], axis=1)
            loaded.append((r0, nvalid, xb))
        outs = []
        for j, (r0, nvalid, xb) in enumerate(loaded):
            xbb = xb.astype(BF16)
            h = _silu(_bdot(xbb, wg_ref[j])) * _bdot(xbb, wu_ref[j])
            y = _bdot(h.astype(BF16), wd_ref[j])
            outs.append(jnp.where(rowid < nvalid, y, xb))
        for (r0, _, _), y in zip(loaded, outs):
            for kc in range(ROW_VREGS):
                sorted_ref[pl.ds(r0 + kc, EXP_BLOCK, stride=ROW_VREGS), :] = y[:, kc * LANES:(kc + 1) * LANES]
        return carry

    lax.fori_loop(0, nmax, ffn_blocks, 0)

    @pl.when(e == pl.num_programs(1) - 1)
    def _combine():
        def body(i, carry):
            for u in range(tok_per_line):
                t = i * tok_per_line + u
                acc = None
                for k in range(TOPK):
                    s = slot_ref[0, i, u * TOPK + k]
                    term = wgt_ref[0, i, u * TOPK + k] * sorted_ref[
                        pl.ds(pl.multiple_of(s, ROW_VREGS), ROW_VREGS), :]
                    acc = term if acc is None else acc + term
                o_ref[pl.ds(pl.multiple_of(t * ROW_VREGS, ROW_VREGS), ROW_VREGS), :] = acc
            return carry

        lax.fori_loop(0, tc // tok_per_line, body, 0)


def _routed_experts(x, slot, wgt, seg_off, seg_cnt, wg, wu, wd, tc):
    t, d = x.shape
    n = t // tc
    sorted_rows = TOPK * tc + EXP_BLOCK
    grid_spec = pltpu.PrefetchScalarGridSpec(
        num_scalar_prefetch=2,
        grid=(n, N_EXPERTS // EXPERTS_PER_STEP),
        in_specs=[pl.BlockSpec((1,) + slot.shape[1:], lambda c, e, *_: (c, 0, 0), memory_space=pltpu.SMEM),
                  pl.BlockSpec((1,) + wgt.shape[1:], lambda c, e, *_: (c, 0, 0), memory_space=pltpu.SMEM),
                  pl.BlockSpec((tc, d), lambda c, e, *_: (c, 0), pipeline_mode=pl.Buffered(1)),
                  pl.BlockSpec((EXPERTS_PER_STEP, d, FF_EXPERT), lambda c, e, *_: (e, 0, 0)),
                  pl.BlockSpec((EXPERTS_PER_STEP, d, FF_EXPERT), lambda c, e, *_: (e, 0, 0)),
                  pl.BlockSpec((EXPERTS_PER_STEP, FF_EXPERT, d), lambda c, e, *_: (e, 0, 0))],
        out_specs=pl.BlockSpec((tc * ROW_VREGS, LANES), lambda c, e, *_: (c, 0)),
        scratch_shapes=[pltpu.VMEM((sorted_rows * ROW_VREGS, LANES), F32)],
    )
    return pl.pallas_call(
        _experts_kernel,
        grid_spec=grid_spec,
        out_shape=jax.ShapeDtypeStruct((t * ROW_VREGS, LANES), F32),
        compiler_params=_params(("arbitrary", "arbitrary"), VMEM_LIMIT_BIG),
        name="routed_experts",
    )(seg_off, seg_cnt, slot, wgt, x, wg, wu, wd)


def _shared_ln_kernel(x_ref, r_ref, wgu_ref, wd_ref, g_ref, b_ref, o_ref):
    tm = x_ref.shape[0]
    x = x_ref[...]
    gu = _bdot(x.astype(BF16), wgu_ref[...])
    ff = wgu_ref.shape[1] // 2
    h = _silu(gu[:, :ff]) * gu[:, ff:]
    shared = _bdot(h.astype(BF16), wd_ref[...])
    routed = jnp.concatenate(
        [r_ref[pl.ds(kc, tm, stride=ROW_VREGS), :] for kc in range(ROW_VREGS)], axis=1)
    z = DEEPNORM_ALPHA * x + (shared + routed)
    o_ref[...] = _layer_norm_rows(z, g_ref[...], b_ref[...])


def _shared_ln(x, routed_rows, wgu, wd, g, b, tm):
    t, d = x.shape
    return pl.pallas_call(
        _shared_ln_kernel,
        grid=(t // tm,),
        in_specs=[pl.BlockSpec((tm, d), lambda i: (i, 0)),
                  pl.BlockSpec((tm * ROW_VREGS, LANES), lambda i: (i, 0)),
                  pl.BlockSpec(wgu.shape, lambda i: (0, 0)),
                  pl.BlockSpec(wd.shape, lambda i: (0, 0)),
                  pl.BlockSpec((1, d), lambda i: (0, 0)),
                  pl.BlockSpec((1, d), lambda i: (0, 0))],
        out_specs=pl.BlockSpec((tm, d), lambda i: (i, 0)),
        out_shape=jax.ShapeDtypeStruct((t, d), F32),
        compiler_params=_params(("parallel",)),
        name="shared_ln",
    )(x, routed_rows, wgu, wd, g.reshape(1, d), b.reshape(1, d))


def _moe_ln(x, p, g, b, tc):
    slot, wgt, off, cnt = _router(x, p["wr_hi"], p["wr_lo"], p["b_col"], tc)
    lines = lambda a: a.transpose(0, 2, 1).reshape(a.shape[0], tc * TOPK // LANES, LANES)
    routed = _routed_experts(x, lines(slot), lines(wgt), off[:, :, 0].reshape(-1), cnt[:, :, 0].reshape(-1),
                             p["wg"], p["wu"], p["wd"], tc)
    return _shared_ln(x, routed, p["wsgu"], p["wsd"], g, b, min(tc, PROMPT_TILE_ROWS))


def _cumsum_kernel(x_ref, o_ref):
    n = x_ref.shape[0]
    blk = LANES
    r = lax.broadcasted_iota(I32, (blk, blk), 0)
    c = lax.broadcasted_iota(I32, (blk, blk), 1)
    tri = (c <= r).astype(BF16)

    def body(i, carry):
        start = pl.multiple_of(i * blk, blk)
        hi, mid, lo = _split3(x_ref[pl.ds(start, blk), :])
        within = (_bdot(tri, lo) + _bdot(tri, mid)) + _bdot(tri, hi)
        o_ref[pl.ds(start, blk), :] = within + carry
        return carry + jnp.sum(x_ref[pl.ds(start, blk), :], axis=0, keepdims=True)

    lax.fori_loop(0, n // blk, body, jnp.zeros((1, x_ref.shape[1]), F32))


def _cumsum_rows(x, batch, seq):
    return pl.pallas_call(
        _cumsum_kernel,
        grid=(batch,),
        in_specs=[pl.BlockSpec((seq, x.shape[1]), lambda b: (b, 0))],
        out_specs=pl.BlockSpec((seq, x.shape[1]), lambda b: (b, 0)),
        out_shape=jax.ShapeDtypeStruct(x.shape, F32),
        compiler_params=_params(("parallel",)),
        name="logf_cumsum",
    )(x)


HEADS_PER_QBLOCK = LANES // FOX_DH
SUM_ROWS = 16


def _fox_prompt_kernel(q_ref, kt_ref, vt_ref, cumq_ref, cumk_ref, o_ref, *, tq):
    hp = pl.program_id(1)
    qi = pl.program_id(2)
    qf = q_ref[...].astype(F32) * (FOX_DH ** -0.5)
    lane = lax.broadcasted_iota(I32, cumq_ref.shape, 1)
    ones_rows = (lax.broadcasted_iota(I32, (SUM_ROWS, tq), 0) == 0).astype(F32)
    outs = []
    for e in range(HEADS_PER_QBLOCK):
        rows = slice(e * FOX_DH, (e + 1) * FOX_DH)
        q = qf[:, rows].astype(BF16)
        cq = jnp.sum(jnp.where(lane == hp * HEADS_PER_QBLOCK + e, cumq_ref[...], 0.0),
                     axis=1, keepdims=True)

        def block(kb, carry, diagonal, rows=rows, q=q, cq=cq, e=e):
            m, acc = carry
            start = pl.multiple_of(kb * tq, tq)
            kt = kt_ref[rows, pl.ds(start, tq)].astype(BF16)
            vt = jnp.concatenate([vt_ref[rows, pl.ds(start, tq)], ones_rows], axis=0).astype(BF16)
            ck = cumk_ref[e:e + 1, pl.ds(start, tq)]
            s = _bdot(q, kt) + cq - ck
            if diagonal:
                qpos = lax.broadcasted_iota(I32, (tq, 1), 0)
                kpos = lax.broadcasted_iota(I32, (1, tq), 1)
                s = jnp.where(kpos <= qpos, s, NEG_INF)
            m_new = jnp.maximum(m, jnp.max(s, axis=1, keepdims=True))
            a = jnp.exp(m - m_new)
            p = jnp.exp(s - m_new)
            return m_new, a * acc + lax.dot_general(p.astype(BF16), vt, _NT, preferred_element_type=F32)

        init = (jnp.full((tq, 1), -jnp.inf, F32), jnp.zeros((tq, FOX_DH + SUM_ROWS), F32))
        carry = lax.fori_loop(0, qi, lambda kb, c, block=block: block(kb, c, False), init)
        _, acc = block(qi, carry, True)
        outs.append(acc[:, :FOX_DH] / acc[:, FOX_DH:FOX_DH + 1])
    o_ref[...] = jnp.concatenate(outs, axis=1).astype(o_ref.dtype)


def _fox_prompt(q, kvt, cum_tok, cum_row, batch, tq):
    t, fd = q.shape
    l = t // batch
    nq = l // tq
    n_pairs = fd // LANES
    kern = functools.partial(_fox_prompt_kernel, tq=tq)
    return pl.pallas_call(
        kern,
        grid=(batch, n_pairs, nq),
        in_specs=[pl.BlockSpec((tq, LANES), lambda bi, hp, qi: (bi * nq + qi, hp)),
                  pl.BlockSpec((None, LANES, l), lambda bi, hp, qi: (bi, hp, 0)),
                  pl.BlockSpec((None, LANES, l), lambda bi, hp, qi: (bi, n_pairs + hp, 0)),
                  pl.BlockSpec((tq, LANES), lambda bi, hp, qi: (bi * nq + qi, 0)),
                  pl.BlockSpec((None, None, HEADS_PER_QBLOCK, l), lambda bi, hp, qi: (bi, hp, 0, 0))],
        out_specs=pl.BlockSpec((tq, LANES), lambda bi, hp, qi: (bi * nq + qi, hp)),
        out_shape=jax.ShapeDtypeStruct((t, fd), BF16),
        compiler_params=_params(("parallel", "parallel", "arbitrary")),
        name="fox_prompt",
    )(q, kvt, kvt, cum_tok, cum_row)


N_QROWS = 128
PAGES_PER_STEP = 4
TAIL_ONES_ROW0 = 3 * FOX_HEADS


def _fox_sample_kernel(pt_ref, q_ref, lfo_ref, kto_ref, vto_ref, *rest, n_steps, dec_seq):
    del pt_ref
    cache_refs = rest[:3 * PAGES_PER_STEP]
    o_ref = rest[3 * PAGES_PER_STEP]
    qbd_ref, qtail_ref, m_ref, l_ref, acc_ref, carry_ref = rest[3 * PAGES_PER_STEP + 1:]
    step = pl.program_id(1)
    n_iota = lax.broadcasted_iota(I32, (N_QROWS, 1), 0)
    row_head = n_iota // dec_seq
    row_q = n_iota % dec_seq
    pos = lax.broadcasted_iota(I32, (1, PAGE_SIZE), 1)
    r = lax.broadcasted_iota(I32, (PAGE_SIZE, PAGE_SIZE), 0)
    c = lax.broadcasted_iota(I32, (PAGE_SIZE, PAGE_SIZE), 1)

    def lane_sums(x_t, sel):
        hi, mid, lo = _split3(x_t)
        return (_bdot(lo, sel) + _bdot(mid, sel)) + _bdot(hi, sel)

    def bias_rows(key_bias_t):
        hi, mid, lo = _split3(key_bias_t)
        ones_rows = (lax.broadcasted_iota(I32, (SUBLANES, PAGE_SIZE), 0) < 3).astype(F32)
        return jnp.concatenate(
            [hi.astype(F32), mid.astype(F32), lo.astype(F32), ones_rows,
             jnp.zeros((LANES - TAIL_ONES_ROW0 - SUBLANES, PAGE_SIZE), F32)], axis=0).astype(BF16)

    def attend(kts, vts, key_biases_t, valid):
        kt = jnp.concatenate([x.astype(BF16) for x in kts], axis=1)
        vt = jnp.concatenate([x.astype(BF16) for x in vts], axis=1)
        ktail = jnp.concatenate([bias_rows(x) for x in key_biases_t], axis=1)
        s = _bdot(qbd_ref[...], kt) + _bdot(qtail_ref[...], ktail)
        if valid is not None:
            s = jnp.where(valid, s, NEG_INF)
        m_old = m_ref[...]
        m_new = jnp.maximum(m_old, jnp.max(s, axis=1, keepdims=True))
        a = jnp.exp(m_old - m_new)
        p = jnp.exp(s - m_new)
        l_ref[...] = a * l_ref[...] + jnp.sum(p, axis=1, keepdims=True)
        acc_ref[...] = a * acc_ref[...] + lax.dot_general(
            p.astype(BF16), vt, _NT, preferred_element_type=F32)
        m_ref[...] = m_new

    @pl.when(step == 0)
    def _own_rows():
        q = q_ref[0] * (FOX_DH ** -0.5)
        qt = jnp.concatenate([q] * FOX_HEADS, axis=0)
        rr = lax.broadcasted_iota(I32, qt.shape, 0)
        cc = lax.broadcasted_iota(I32, qt.shape, 1)
        qbd_ref[...] = jnp.where(cc // FOX_DH == rr // dec_seq, qt, 0.0).astype(BF16)
        prefix_t = lane_sums(lfo_ref[0], (r <= c).astype(BF16))
        rep = (lax.broadcasted_iota(I32, (N_QROWS, FOX_HEADS), 1)
               == lax.broadcasted_iota(I32, (N_QROWS, FOX_HEADS), 0) // dec_seq).astype(BF16)
        hi, mid, lo = _split3(prefix_t)
        by_row = (_bdot(rep, lo) + _bdot(rep, mid)) + _bdot(rep, hi)
        pre_col = jnp.sum(jnp.where(pos == row_q, by_row, 0.0), axis=1, keepdims=True)
        hi, mid, lo = (x.astype(F32) for x in _split3(pre_col))
        lane = lax.broadcasted_iota(I32, (N_QROWS, LANES), 1)
        tail = jnp.where(jnp.logical_and(lane < TAIL_ONES_ROW0, lane % FOX_HEADS == row_head), 1.0, 0.0)
        tail = jnp.where(lane == TAIL_ONES_ROW0, hi, tail)
        tail = jnp.where(lane == TAIL_ONES_ROW0 + 1, mid, tail)
        tail = jnp.where(lane == TAIL_ONES_ROW0 + 2, lo, tail)
        qtail_ref[...] = tail.astype(BF16)
        m_ref[...] = jnp.full(m_ref.shape, -jnp.inf, F32)
        l_ref[...] = jnp.zeros(l_ref.shape, F32)
        acc_ref[...] = jnp.zeros(acc_ref.shape, F32)
        carry_ref[...] = jnp.zeros(carry_ref.shape, F32)
        valid = jnp.logical_and(pos <= row_q, pos < dec_seq)
        attend([kto_ref[0]], [vto_ref[0]], [-prefix_t], valid)

    @pl.when(step > 0)
    def _cache_pages():
        after = (r > c).astype(BF16)
        carry = carry_ref[...]
        kts, vts, biases = [], [], []
        for u in range(PAGES_PER_STEP):
            kt_ref, vt_ref, lf_ref = cache_refs[3 * u:3 * u + 3]
            lf_t = lf_ref[0]
            kts.append(kt_ref[0])
            vts.append(vt_ref[0])
            biases.append(lane_sums(lf_t, after) + carry)
            carry = carry + jnp.sum(lf_t, axis=1, keepdims=True)
        attend(kts, vts, biases, None)
        carry_ref[...] = carry

    @pl.when(step == n_steps - 1)
    def _finish():
        col_head = lax.broadcasted_iota(I32, (dec_seq, D_MODEL), 1) // FOX_DH
        out = jnp.zeros((dec_seq, D_MODEL), F32)
        for h in range(FOX_HEADS):
            rows = slice(h * dec_seq, (h + 1) * dec_seq)
            out = out + jnp.where(col_head == h, acc_ref[rows, :] / l_ref[rows, :], 0.0)
        o_ref[0] = out


def _fox_sample(q, lft_own, kt_own, vt_own, kt_cache, vt_cache, lft_cache, page_table):
    db, dec_seq, _ = q.shape
    n_pages = page_table.shape[1]
    n_steps = 1 + n_pages // PAGES_PER_STEP
    fd = FOX_HEADS * FOX_DH
    kern = functools.partial(_fox_sample_kernel, n_steps=n_steps, dec_seq=dec_seq)

    def page(u):
        return lambda b, s, pt: (pt[b, n_pages - 1 - (jnp.maximum(s, 1) - 1) * PAGES_PER_STEP - u], 0, 0)

    cache_specs = []
    cache_args = []
    for u in range(PAGES_PER_STEP):
        cache_specs += [pl.BlockSpec((1, fd, PAGE_SIZE), page(u)),
                        pl.BlockSpec((1, fd, PAGE_SIZE), page(u)),
                        pl.BlockSpec((1, FOX_HEADS, PAGE_SIZE), page(u))]
        cache_args += [kt_cache, vt_cache, lft_cache]
    grid_spec = pltpu.PrefetchScalarGridSpec(
        num_scalar_prefetch=1,
        grid=(db, n_steps),
        in_specs=[pl.BlockSpec((1, dec_seq, D_MODEL), lambda b, s, pt: (b, 0, 0)),
                  pl.BlockSpec((1, FOX_HEADS, PAGE_SIZE), lambda b, s, pt: (b, 0, 0)),
                  pl.BlockSpec((1, fd, PAGE_SIZE), lambda b, s, pt: (b, 0, 0)),
                  pl.BlockSpec((1, fd, PAGE_SIZE), lambda b, s, pt: (b, 0, 0))] + cache_specs,
        out_specs=pl.BlockSpec((1, dec_seq, D_MODEL), lambda b, s, pt: (b, 0, 0)),
        scratch_shapes=[pltpu.VMEM((N_QROWS, fd), BF16),
                        pltpu.VMEM((N_QROWS, LANES), BF16),
                        pltpu.VMEM((N_QROWS, 1), F32),
                        pltpu.VMEM((N_QROWS, 1), F32),
                        pltpu.VMEM((N_QROWS, fd), F32),
                        pltpu.VMEM((FOX_HEADS, 1), F32)],
    )
    return pl.pallas_call(
        kern,
        grid_spec=grid_spec,
        out_shape=jax.ShapeDtypeStruct((db, dec_seq, D_MODEL), F32),
        compiler_params=_params(("parallel", "arbitrary")),
        name="fox_sample",
    )(page_table, q, lft_own, kt_own, vt_own, *cache_args)


def _moe_params(layer, w_router, b_router, w_gate, w_up, w_down, ws_gate, ws_up, ws_down):
    wr_t = w_router[layer].T
    wr_hi = wr_t.astype(BF16)
    wr_lo = (wr_t - wr_hi.astype(F32)).astype(BF16)
    return dict(
        wr_hi=wr_hi, wr_lo=wr_lo, b_col=b_router[layer].reshape(N_EXPERTS, 1),
        wg=w_gate[layer].astype(BF16), wu=w_up[layer].astype(BF16), wd=w_down[layer].astype(BF16),
        wsgu=jnp.concatenate([ws_gate[layer], ws_up[layer]], axis=1).astype(BF16),
        wsd=ws_down[layer].astype(BF16))


def kernel(x_prompt, x_sample, state_ret, cache_k, cache_v, cache_logf, page_table, ret_w_in, ret_w_out,
           fox_w_kvf, fox_b_f, fox_w_q, fox_w_out, ln_g, ln_b, moe_w_router, moe_b_router, moe_w_gate,
           moe_w_up, moe_w_down, moe_ws_gate, moe_ws_up, moe_ws_down):
    bp, lp, d = x_prompt.shape
    bs, ls, _ = x_sample.shape
    tp, ts = bp * lp, bs * ls
    assert fox_heads_rows(ls) == N_QROWS
    xp = x_prompt.reshape(tp, d)
    xs = x_sample.reshape(ts, d)
    hq = RET_HEADS * RET_DK
    hv = RET_HEADS * RET_DV
    fd = FOX_HEADS * FOX_DH
    tm_p, tn_p = PROMPT_TILE_ROWS, PROJ_TILE_COLS
    tc_p = MOE_TOKEN_TILE

    moe = [_moe_params(layer, moe_w_router, moe_b_router, moe_w_gate, moe_w_up, moe_w_down,
                       moe_ws_gate, moe_ws_up, moe_ws_down) for layer in range(DEPTH)]

    w_in = ret_w_in[0].astype(BF16)
    w_out = ret_w_out[0].astype(BF16)
    cos_p, sin_p = _rope_tables(jnp.arange(lp))
    cos_s, sin_s = _rope_tables(PAST_LEN + jnp.arange(ls))
    cos_s, sin_s = jnp.tile(cos_s, (bs, 1)), jnp.tile(sin_s, (bs, 1))

    qk_p = _qk_rope(xp, w_in, cos_p, sin_p, tm_p)
    v_p = _matmul(xp, w_in, 2 * hq, hv, BF16, tm_p, tn_p)
    g_p = _matmul(xp, w_in, 2 * hq + hv, hv, F32, tm_p, tn_p)
    o_p, st_p = _retention(qk_p, v_p, g_p, jnp.zeros((bp, RET_HEADS, RET_DK, RET_DV), F32),
                           _retention_decays(min(RET_CHUNK, lp)), bp, lp, BF16)
    xp = _mm_res_ln(o_p, w_out, xp, ln_g[0, 0], ln_b[0, 0], tm_p)

    qk_s = _qk_rope(xs, w_in, cos_s, sin_s, ts)
    v_s = _matmul(xs, w_in, 2 * hq, hv, F32, ts, tn_p)
    g_s = _matmul(xs, w_in, 2 * hq + hv, hv, F32, ts, tn_p)
    o_s, st_s = _retention(qk_s, v_s, g_s, state_ret[0], _retention_decays(min(RET_CHUNK, ls)), bs, ls, F32)
    xs = _mm_res_ln(o_s, w_out, xs, ln_g[0, 0], ln_b[0, 0], ts)

    xp = _moe_ln(xp, moe[0], ln_g[0, 1], ln_b[0, 1], tc_p)
    xs = _moe_ln(xs, moe[0], ln_g[0, 1], ln_b[0, 1], ts)

    w_kvf = fox_w_kvf.astype(BF16)
    w_kv_t = w_kvf[:, :2 * fd].T
    w_f_pad = jnp.zeros((d, LANES), BF16).at[:, :FOX_HEADS].set(w_kvf[:, 2 * fd:])
    b_f_pad = jnp.zeros((1, LANES), F32).at[0, :FOX_HEADS].set(fox_b_f)
    kvt_p = _matmul_t(xp, w_kv_t, bp, tm_p, tn_p)
    lf_p = _logf_proj(xp, w_f_pad, b_f_pad, tm_p)
    kvt_s = _matmul_t(xs, w_kv_t, 1, ts, tn_p)[0].reshape(2 * fd, bs, ls).transpose(1, 0, 2)
    lf_s = _logf_proj(xs, w_f_pad, b_f_pad, ts)

    as_blhd = lambda a_t, b, l: a_t.reshape(b, FOX_HEADS, FOX_DH, l).transpose(0, 3, 1, 2)
    k_p, v_p4 = as_blhd(kvt_p[:, :fd], bp, lp), as_blhd(kvt_p[:, fd:], bp, lp)
    k_s, v_s4 = as_blhd(kvt_s[:, :fd], bs, ls), as_blhd(kvt_s[:, fd:], bs, ls)

    w_q = fox_w_q[0].astype(BF16)
    w_o = fox_w_out[0].astype(BF16)
    q_p = _matmul(xp, w_q, 0, fd, BF16, tm_p, tn_p)
    cum_p = _cumsum_rows(lf_p, bp, lp)
    cum_row = cum_p[:, :FOX_HEADS].reshape(bp, lp, FOX_HEADS).transpose(0, 2, 1).reshape(
        bp, FOX_HEADS // HEADS_PER_QBLOCK, HEADS_PER_QBLOCK, lp)
    att_p = _fox_prompt(q_p, kvt_p, cum_p, cum_row, bp, ATTN_TILE)
    xp = _mm_res_ln(att_p, w_o, xp, ln_g[1, 0], ln_b[1, 0], tm_p)

    q_s = _matmul(xs, w_q, 0, fd, F32, ts, tn_p)
    n_pool = cache_k.shape[0]
    kt_cache = cache_k.transpose(0, 2, 3, 1).reshape(n_pool, fd, PAGE_SIZE)
    vt_cache = cache_v.transpose(0, 2, 3, 1).reshape(n_pool, fd, PAGE_SIZE)
    lft_cache = cache_logf.transpose(0, 2, 1)
    pad_pos = lambda a_t: jnp.pad(a_t, ((0, 0), (0, 0), (0, PAGE_SIZE - ls)))
    lft_own = pad_pos(lf_s[:, :FOX_HEADS].reshape(bs, ls, FOX_HEADS).transpose(0, 2, 1))
    att_s = _fox_sample(q_s.reshape(bs, ls, d), lft_own, pad_pos(kvt_s[:, :fd]), pad_pos(kvt_s[:, fd:]),
                        kt_cache, vt_cache, lft_cache, page_table)
    xs = _mm_res_ln(att_s.reshape(ts, fd), w_o, xs, ln_g[1, 0], ln_b[1, 0], ts)

    xp = _moe_ln(xp, moe[1], ln_g[1, 1], ln_b[1, 1], tc_p)
    xs = _moe_ln(xs, moe[1], ln_g[1, 1], ln_b[1, 1], ts)

    return (xp.reshape(bp, lp, d), xs.reshape(bs, ls, d), st_p[None], st_s[None],
            k_p, v_p4, lf_p[:, :FOX_HEADS].reshape(bp, lp, FOX_HEADS),
            k_s, v_s4, lf_s[:, :FOX_HEADS].reshape(bs, ls, FOX_HEADS))


def fox_heads_rows(dec_seq):
    return FOX_HEADS * dec_seq
```

```python
import functools

import jax
import jax.numpy as jnp
from jax import lax
from jax.experimental import pallas as pl
from jax.experimental.pallas import tpu as pltpu

F32 = jnp.float32
BF16 = jnp.bfloat16
I32 = jnp.int32

D_MODEL = 1024
DEPTH = 2
PAST_LEN = 8192
PAGE_SIZE = 128
N_A_LAYERS = DEPTH // 2
RET_HEADS = 4
RET_DK = D_MODEL // RET_HEADS
RET_DV = 2 * RET_DK
RET_CHUNK = 128
ROPE_BASE = 10000.0
FOX_HEADS = 16
FOX_DH = D_MODEL // FOX_HEADS
NEG_INF = -1e30
N_EXPERTS = 64
TOPK = 8
N_GROUPS = 8
GROUP_SIZE = N_EXPERTS // N_GROUPS
TOPK_GROUPS = 4
FF_EXPERT = D_MODEL // 4
ROUTED_SCALE = 2.5
DEEPNORM_ALPHA = (2.0 * DEPTH) ** 0.25
LN_EPS = 1e-5
GN_EPS = 1e-5

LANES = 128
SUBLANES = 8
ROW_VREGS = D_MODEL // LANES
EXP_BLOCK = 160
EXPERTS_PER_STEP = 2
VMEM_LIMIT_BIG = 60 * 1024 * 1024
VMEM_LIMIT = 48 * 1024 * 1024

PROMPT_TILE_ROWS = 1024
PROJ_TILE_COLS = 1024
MOE_TOKEN_TILE = 1024
ATTN_TILE = 512

_NT = (((1,), (1,)), ((), ()))
_TN = (((0,), (0,)), ((), ()))


def _params(sem, vmem=VMEM_LIMIT):
    return pltpu.CompilerParams(dimension_semantics=sem, vmem_limit_bytes=vmem)


def _bdot(a, b):
    return jnp.dot(a, b, preferred_element_type=F32)


def _layer_norm_rows(z, g, b):
    mu = jnp.mean(z, axis=-1, keepdims=True)
    zc = z - mu
    var = jnp.mean(zc * zc, axis=-1, keepdims=True)
    return zc * lax.rsqrt(var + LN_EPS) * g + b


def _silu(x):
    return x * jax.nn.sigmoid(x)


def _split3(x):
    hi = x.astype(BF16)
    r1 = x - hi.astype(F32)
    mid = r1.astype(BF16)
    lo = (r1 - mid.astype(F32)).astype(BF16)
    return hi, mid, lo


def _mm_kernel(x_ref, w_ref, o_ref):
    o_ref[...] = _bdot(x_ref[...].astype(BF16), w_ref[...]).astype(o_ref.dtype)


def _matmul(x, w, col0, n, out_dtype, tm, tn):
    m, k = x.shape
    j0 = col0 // tn
    return pl.pallas_call(
        _mm_kernel,
        grid=(m // tm, n // tn),
        in_specs=[pl.BlockSpec((tm, k), lambda i, j: (i, 0)),
                  pl.BlockSpec((k, tn), lambda i, j: (0, j + j0))],
        out_specs=pl.BlockSpec((tm, tn), lambda i, j: (i, j)),
        out_shape=jax.ShapeDtypeStruct((m, n), out_dtype),
        compiler_params=_params(("parallel", "arbitrary")),
        name="proj",
    )(x, w)


def _mm_t_pair_kernel(x_ref, wt_ref, o0_ref, o1_ref):
    j = pl.program_id(1)
    res = lax.dot_general(wt_ref[...], x_ref[...].astype(BF16), _NT, preferred_element_type=F32)

    @pl.when(j == 0)
    def _():
        o0_ref[...] = res

    @pl.when(j == 1)
    def _():
        o1_ref[...] = res


def _matmul_t_pair(x, w_t, batch, tm):
    m, k = x.shape
    n = w_t.shape[0] // 2
    seq = m // batch
    nper = seq // tm
    out_spec = pl.BlockSpec((None, n, tm), lambda i, j: (i // nper, 0, i % nper))
    return pl.pallas_call(
        _mm_t_pair_kernel,
        grid=(m // tm, 2),
        in_specs=[pl.BlockSpec((tm, k), lambda i, j: (i, 0)),
                  pl.BlockSpec((n, k), lambda i, j: (j, 0))],
        out_specs=[out_spec, out_spec],
        out_shape=[jax.ShapeDtypeStruct((batch, n, seq), F32)] * 2,
        compiler_params=_params(("parallel", "arbitrary")),
        name="proj_t",
    )(x, w_t)


def _mm_rope_kernel(x_ref, w_ref, cos_ref, sin_ref, o_ref, *, k_scale):
    j = pl.program_id(1)
    acc = _bdot(x_ref[...].astype(BF16), w_ref[...])
    half = RET_DK // 2
    c = cos_ref[...]
    s = sin_ref[...]
    sc = jnp.where(j == 0, 1.0, k_scale).astype(F32)
    for h in range(RET_HEADS):
        lo = h * RET_DK
        x1 = acc[:, lo:lo + half]
        x2 = acc[:, lo + half:lo + RET_DK]
        o_ref[:, lo:lo + half] = (x1 * c - x2 * s) * sc
        o_ref[:, lo + half:lo + RET_DK] = (x2 * c + x1 * s) * sc


def _qk_rope(x, w, cos_rows, sin_rows, tm):
    m, k = x.shape
    hq = RET_HEADS * RET_DK
    nper = cos_rows.shape[0] // tm
    kern = functools.partial(_mm_rope_kernel, k_scale=RET_DK ** -0.5)
    return pl.pallas_call(
        kern,
        grid=(m // tm, 2),
        in_specs=[pl.BlockSpec((tm, k), lambda i, j: (i, 0)),
                  pl.BlockSpec((k, hq), lambda i, j: (0, j)),
                  pl.BlockSpec((tm, RET_DK // 2), lambda i, j: (i % nper, 0)),
                  pl.BlockSpec((tm, RET_DK // 2), lambda i, j: (i % nper, 0))],
        out_specs=pl.BlockSpec((tm, hq), lambda i, j: (i, j)),
        out_shape=jax.ShapeDtypeStruct((m, 2 * hq), F32),
        compiler_params=_params(("parallel", "arbitrary")),
        name="qk_rope",
    )(x, w, cos_rows, sin_rows)


def _mm_res_ln_kernel(a_ref, w_ref, x_ref, g_ref, b_ref, o_ref):
    mix = _bdot(a_ref[...].astype(BF16), w_ref[...])
    z = DEEPNORM_ALPHA * x_ref[...] + mix
    o_ref[...] = _layer_norm_rows(z, g_ref[...], b_ref[...])


def _mm_res_ln(a, w, x, g, b, tm):
    m, k = a.shape
    d = w.shape[1]
    return pl.pallas_call(
        _mm_res_ln_kernel,
        grid=(m // tm,),
        in_specs=[pl.BlockSpec((tm, k), lambda i: (i, 0)),
                  pl.BlockSpec((k, d), lambda i: (0, 0)),
                  pl.BlockSpec((tm, d), lambda i: (i, 0)),
                  pl.BlockSpec((1, d), lambda i: (0, 0)),
                  pl.BlockSpec((1, d), lambda i: (0, 0))],
        out_specs=pl.BlockSpec((tm, d), lambda i: (i, 0)),
        out_shape=jax.ShapeDtypeStruct((m, d), F32),
        compiler_params=_params(("parallel",)),
        name="mix_out_ln",
    )(a, w, x, g.reshape(1, d), b.reshape(1, d))


def _mm_logsig_kernel(x_ref, w_ref, b_ref, o_ref):
    z = _bdot(x_ref[...].astype(BF16), w_ref[...]) + b_ref[...]
    o_ref[...] = -(jnp.maximum(-z, 0.0) + jnp.log1p(jnp.exp(-jnp.abs(z))))


def _logf_proj(x, w_pad, b_pad, tm):
    m, k = x.shape
    return pl.pallas_call(
        _mm_logsig_kernel,
        grid=(m // tm,),
        in_specs=[pl.BlockSpec((tm, k), lambda i: (i, 0)),
                  pl.BlockSpec((k, LANES), lambda i: (0, 0)),
                  pl.BlockSpec((1, LANES), lambda i: (0, 0))],
        out_specs=pl.BlockSpec((tm, LANES), lambda i: (i, 0)),
        out_shape=jax.ShapeDtypeStruct((m, LANES), F32),
        compiler_params=_params(("parallel",)),
        name="logf_proj",
    )(x, w_pad, b_pad)


def _ret_kernel(q_ref, k_ref, v_ref, g_ref, s0_ref, din_ref, dq_ref, dk_ref, dc_ref,
                o_ref, sout_ref, state_ref):
    c = pl.program_id(1)

    @pl.when(c == 0)
    def _():
        state_ref[...] = s0_ref[...]

    for h in range(RET_HEADS):
        qcols = slice(h * RET_DK, (h + 1) * RET_DK)
        vcols = slice(h * RET_DV, (h + 1) * RET_DV)
        q = q_ref[:, qcols].astype(BF16)
        k = k_ref[:, qcols]
        v = v_ref[:, vcols].astype(BF16)
        s_prev = state_ref[h]
        scores = lax.dot_general(q, k.astype(BF16), _NT, preferred_element_type=F32) * din_ref[h]
        o = _bdot(scores.astype(BF16), v) + _bdot(q, s_prev.astype(BF16)) * dq_ref[h]
        kd = (k * dk_ref[h]).astype(BF16)
        s_new = s_prev * dc_ref[h] + lax.dot_general(kd, v, _TN, preferred_element_type=F32)
        state_ref[h] = s_new

        mu = jnp.mean(o, axis=-1, keepdims=True)
        oc = o - mu
        var = jnp.mean(oc * oc, axis=-1, keepdims=True)
        on = oc * lax.rsqrt(var + GN_EPS)
        o_ref[:, vcols] = (_silu(g_ref[:, vcols]) * on).astype(o_ref.dtype)

    @pl.when(c == pl.num_programs(1) - 1)
    def _():
        sout_ref[...] = state_ref[...]


def _retention(qk, v, g, s0, decays, batch, seq, out_dtype):
    chunk = min(RET_CHUNK, seq)
    nc = seq // chunk
    din, dq, dk, dc = decays
    h_ = RET_HEADS
    hq, hv = h_ * RET_DK, h_ * RET_DV
    whole = lambda a: pl.BlockSpec(a.shape, lambda b, c: (0,) * a.ndim)
    return pl.pallas_call(
        _ret_kernel,
        grid=(batch, nc),
        in_specs=[pl.BlockSpec((chunk, hq), lambda b, c: (b * nc + c, 0)),
                  pl.BlockSpec((chunk, hq), lambda b, c: (b * nc + c, 1)),
                  pl.BlockSpec((chunk, hv), lambda b, c: (b * nc + c, 0)),
                  pl.BlockSpec((chunk, hv), lambda b, c: (b * nc + c, 0)),
                  pl.BlockSpec((None, h_, RET_DK, RET_DV), lambda b, c: (b, 0, 0, 0)),
                  whole(din), whole(dq), whole(dk), whole(dc)],
        out_specs=[pl.BlockSpec((chunk, hv), lambda b, c: (b * nc + c, 0)),
                   pl.BlockSpec((None, h_, RET_DK, RET_DV), lambda b, c: (b, 0, 0, 0))],
        out_shape=[jax.ShapeDtypeStruct((batch * seq, hv), out_dtype),
                   jax.ShapeDtypeStruct((batch, h_, RET_DK, RET_DV), F32)],
        scratch_shapes=[pltpu.VMEM((h_, RET_DK, RET_DV), F32)],
        compiler_params=_params(("parallel", "arbitrary")),
        name="retention",
    )(qk, qk, v, g, s0, din, dq, dk, dc)


def _retention_decays(chunk):
    h = jnp.arange(RET_HEADS, dtype=F32)
    log_g = jnp.log1p(-(2.0 ** (-5.0 - h)))
    idx = jnp.arange(chunk, dtype=F32)
    rel = idx[:, None] - idx[None, :]
    din = jnp.where(rel >= 0, jnp.exp(log_g[:, None, None] * jnp.maximum(rel, 0.0)), 0.0)
    dq = jnp.exp(log_g[:, None] * (idx[None, :] + 1.0))[:, :, None]
    dk = jnp.exp(log_g[:, None] * (chunk - 1.0 - idx[None, :]))[:, :, None]
    dc = jnp.exp(log_g * chunk)[:, None, None]
    return din, dq, dk, dc


def _rope_tables(pos):
    half = RET_DK // 2
    inv_freq = ROPE_BASE ** (-jnp.arange(half, dtype=F32) / half)
    ang = pos.astype(F32)[:, None] * inv_freq[None, :]
    return jnp.cos(ang), jnp.sin(ang)


def _router_kernel(x_ref, wh_ref, wl_ref, b_ref, slot_ref, wgt_ref, off_ref, cnt_ref):
    tc = x_ref.shape[0]
    x = x_ref[...]
    xh = x.astype(BF16)
    xl = (x - xh.astype(F32)).astype(BF16)
    wh = wh_ref[...]
    wl = wl_ref[...]
    logits = (lax.dot_general(wh, xh, _NT, preferred_element_type=F32)
              + lax.dot_general(wh, xl, _NT, preferred_element_type=F32)
              + lax.dot_general(wl, xh, _NT, preferred_element_type=F32))
    scores = jax.nn.sigmoid(logits)
    biased = scores + b_ref[...]

    jj = lax.broadcasted_iota(I32, (GROUP_SIZE, tc), 0)
    groups = [biased[g * GROUP_SIZE:(g + 1) * GROUP_SIZE, :] for g in range(N_GROUPS)]
    gscore = []
    for rows in groups:
        m1 = jnp.max(rows, axis=0, keepdims=True)
        j1 = jnp.min(jnp.where(rows == m1, jj, GROUP_SIZE), axis=0, keepdims=True)
        m2 = jnp.max(jnp.where(jj == j1, -jnp.inf, rows), axis=0, keepdims=True)
        gscore.append(m1 + m2)
    gsel = [jnp.zeros((1, tc), dtype=jnp.bool_) for _ in range(N_GROUPS)]
    for _ in range(TOPK_GROUPS):
        gm = gscore[0]
        for sc in gscore[1:]:
            gm = jnp.maximum(gm, sc)
        found = jnp.zeros((1, tc), dtype=jnp.bool_)
        for g in range(N_GROUPS):
            hit = jnp.logical_and(gscore[g] == gm, jnp.logical_not(found))
            found = jnp.logical_or(found, hit)
            gsel[g] = jnp.logical_or(gsel[g], hit)
            gscore[g] = jnp.where(hit, -jnp.inf, gscore[g])
    masked = jnp.concatenate(
        [jnp.where(jnp.broadcast_to(gsel[g], groups[g].shape), groups[g], -jnp.inf)
         for g in range(N_GROUPS)], axis=0)
    ei = lax.broadcasted_iota(I32, masked.shape, 0)
    hits = []
    wk = []
    for _ in range(TOPK):
        m = jnp.max(masked, axis=0, keepdims=True)
        first = jnp.min(jnp.where(masked == m, ei, N_EXPERTS), axis=0, keepdims=True)
        hit = ei == first
        hits.append(hit)
        wk.append(jnp.sum(jnp.where(hit, scores, 0.0), axis=0, keepdims=True))
        masked = jnp.where(hit, -jnp.inf, masked)
    wsum = wk[0]
    for w in wk[1:]:
        wsum = wsum + w
    wgt_ref[...] = jnp.concatenate([w / wsum * ROUTED_SCALE for w in wk], axis=0)

    chosen = hits[0]
    for hit in hits[1:]:
        chosen = jnp.logical_or(chosen, hit)
    chosen = chosen.astype(BF16)
    r = lax.broadcasted_iota(I32, (tc, tc), 0)
    c = lax.broadcasted_iota(I32, (tc, tc), 1)
    rank = _bdot(chosen, (r <= c).astype(BF16))
    er = lax.broadcasted_iota(I32, (N_EXPERTS, N_EXPERTS), 0)
    ec = lax.broadcasted_iota(I32, (N_EXPERTS, N_EXPERTS), 1)
    below = _bdot((ec < er).astype(BF16), chosen)
    off = jnp.sum(below, axis=1, keepdims=True)
    cnt = rank[:, tc - 1:tc]
    pos = (off + rank - 1.0) * ROW_VREGS
    slot_ref[...] = jnp.concatenate(
        [jnp.sum(jnp.where(hit, pos, 0.0), axis=0, keepdims=True) for hit in hits], axis=0).astype(I32)
    off_ref[...] = jnp.broadcast_to(off, off_ref.shape).astype(I32)
    cnt_ref[...] = jnp.broadcast_to(cnt, cnt_ref.shape).astype(I32)


def _router(x, wr_hi, wr_lo, b_col, tc):
    t, d = x.shape
    n = t // tc
    return pl.pallas_call(
        _router_kernel,
        grid=(n,),
        in_specs=[pl.BlockSpec((tc, d), lambda i: (i, 0)),
                  pl.BlockSpec((N_EXPERTS, d), lambda i: (0, 0)),
                  pl.BlockSpec((N_EXPERTS, d), lambda i: (0, 0)),
                  pl.BlockSpec((N_EXPERTS, 1), lambda i: (0, 0))],
        out_specs=[pl.BlockSpec((None, TOPK, tc), lambda i: (i, 0, 0)),
                   pl.BlockSpec((None, TOPK, tc), lambda i: (i, 0, 0)),
                   pl.BlockSpec((None, N_EXPERTS, LANES), lambda i: (i, 0, 0)),
                   pl.BlockSpec((None, N_EXPERTS, LANES), lambda i: (i, 0, 0))],
        out_shape=[jax.ShapeDtypeStruct((n, TOPK, tc), I32),
                   jax.ShapeDtypeStruct((n, TOPK, tc), F32),
                   jax.ShapeDtypeStruct((n, N_EXPERTS, LANES), I32),
                   jax.ShapeDtypeStruct((n, N_EXPERTS, LANES), I32)],
        compiler_params=_params(("parallel",)),
        name="router",
    )(x, wr_hi, wr_lo, b_col)


def _experts_kernel(off_ref, cnt_ref, slot_ref, wgt_ref, x_ref, wg_ref, wu_ref, wd_ref,
                    o_ref, sorted_ref):
    tc = x_ref.shape[0]
    ci = pl.program_id(0)
    e = pl.program_id(1)
    tok_per_line = LANES // TOPK

    @pl.when(e == 0)
    def _dispatch():
        for kc in range(ROW_VREGS):
            o_ref[pl.ds(kc, tc, stride=ROW_VREGS), :] = x_ref[:, kc * LANES:(kc + 1) * LANES]
        sorted_ref[pl.ds(TOPK * tc * ROW_VREGS, EXP_BLOCK * ROW_VREGS), :] = jnp.zeros(
            (EXP_BLOCK * ROW_VREGS, LANES), F32)

        def body(i, carry):
            for u in range(tok_per_line):
                t = i * tok_per_line + u
                row = o_ref[pl.ds(pl.multiple_of(t * ROW_VREGS, ROW_VREGS), ROW_VREGS), :]
                for k in range(TOPK):
                    s = slot_ref[0, i, u * TOPK + k]
                    sorted_ref[pl.ds(pl.multiple_of(s, ROW_VREGS), ROW_VREGS), :] = row
            return carry

        lax.fori_loop(0, tc // tok_per_line, body, 0)

    base = ci * N_EXPERTS + e * EXPERTS_PER_STEP
    offs = [off_ref[base + j] for j in range(EXPERTS_PER_STEP)]
    cnts = [cnt_ref[base + j] for j in range(EXPERTS_PER_STEP)]
    nblks = [lax.div(cnt + (EXP_BLOCK - 1), EXP_BLOCK) for cnt in cnts]
    nmax = nblks[0]
    for nb in nblks[1:]:
        nmax = jnp.maximum(nmax, nb)
    rowid = lax.broadcasted_iota(I32, (EXP_BLOCK, 1), 0)

    def ffn_blocks(i, carry):
        loaded = []
        for j in range(EXPERTS_PER_STEP):
            active = i < nblks[j]
            row0 = jnp.where(active, offs[j] + i * EXP_BLOCK, TOPK * tc)
            nvalid = jnp.where(active, cnts[j] - i * EXP_BLOCK, 0)
            r0 = row0 * ROW_VREGS
            xb = jnp.concatenate(
                [sorted_ref[pl.ds(r0 + kc, EXP_BLOCK, stride=ROW_VREGS), :] for kc in range(ROW_VREGS)], axis=1)
            loaded.append((r0, nvalid, xb))
        outs = []
        for j, (r0, nvalid, xb) in enumerate(loaded):
            xbb = xb.astype(BF16)
            h = _silu(_bdot(xbb, wg_ref[j])) * _bdot(xbb, wu_ref[j])
            y = _bdot(h.astype(BF16), wd_ref[j])
            outs.append(jnp.where(rowid < nvalid, y, xb))
        for (r0, _, _), y in zip(loaded, outs):
            for kc in range(ROW_VREGS):
                sorted_ref[pl.ds(r0 + kc, EXP_BLOCK, stride=ROW_VREGS), :] = y[:, kc * LANES:(kc + 1) * LANES]
        return carry

    lax.fori_loop(0, nmax, ffn_blocks, 0)

    @pl.when(e == pl.num_programs(1) - 1)
    def _combine():
        def body(i, carry):
            for u in range(tok_per_line):
                t = i * tok_per_line + u
                acc = None
                for k in range(TOPK):
                    s = slot_ref[0, i, u * TOPK + k]
                    term = wgt_ref[0, i, u * TOPK + k] * sorted_ref[
                        pl.ds(pl.multiple_of(s, ROW_VREGS), ROW_VREGS), :]
                    acc = term if acc is None else acc + term
                o_ref[pl.ds(pl.multiple_of(t * ROW_VREGS, ROW_VREGS), ROW_VREGS), :] = acc
            return carry

        lax.fori_loop(0, tc // tok_per_line, body, 0)


def _routed_experts(x, slot, wgt, seg_off, seg_cnt, wg, wu, wd, layer, tc):
    t, d = x.shape
    n = t // tc
    sorted_rows = TOPK * tc + EXP_BLOCK
    grid_spec = pltpu.PrefetchScalarGridSpec(
        num_scalar_prefetch=2,
        grid=(n, N_EXPERTS // EXPERTS_PER_STEP),
        in_specs=[pl.BlockSpec((1,) + slot.shape[1:], lambda c, e, *_: (c, 0, 0), memory_space=pltpu.SMEM),
                  pl.BlockSpec((1,) + wgt.shape[1:], lambda c, e, *_: (c, 0, 0), memory_space=pltpu.SMEM),
                  pl.BlockSpec((tc, d), lambda c, e, *_: (c, 0), pipeline_mode=pl.Buffered(1)),
                  pl.BlockSpec((None, EXPERTS_PER_STEP, d, FF_EXPERT), lambda c, e, *_: (layer, e, 0, 0)),
                  pl.BlockSpec((None, EXPERTS_PER_STEP, d, FF_EXPERT), lambda c, e, *_: (layer, e, 0, 0)),
                  pl.BlockSpec((None, EXPERTS_PER_STEP, FF_EXPERT, d), lambda c, e, *_: (layer, e, 0, 0))],
        out_specs=pl.BlockSpec((tc * ROW_VREGS, LANES), lambda c, e, *_: (c, 0)),
        scratch_shapes=[pltpu.VMEM((sorted_rows * ROW_VREGS, LANES), F32)],
    )
    return pl.pallas_call(
        _experts_kernel,
        grid_spec=grid_spec,
        out_shape=jax.ShapeDtypeStruct((t * ROW_VREGS, LANES), F32),
        compiler_params=_params(("arbitrary", "arbitrary"), VMEM_LIMIT_BIG),
        name="routed_experts",
    )(seg_off, seg_cnt, slot, wgt, x, wg, wu, wd)


def _shared_ln_kernel(x_ref, r_ref, wgu_ref, wd_ref, g_ref, b_ref, o_ref):
    tm = x_ref.shape[0]
    x = x_ref[...]
    gu = _bdot(x.astype(BF16), wgu_ref[...])
    ff = wgu_ref.shape[1] // 2
    h = _silu(gu[:, :ff]) * gu[:, ff:]
    shared = _bdot(h.astype(BF16), wd_ref[...])
    routed = jnp.concatenate(
        [r_ref[pl.ds(kc, tm, stride=ROW_VREGS), :] for kc in range(ROW_VREGS)], axis=1)
    z = DEEPNORM_ALPHA * x + (shared + routed)
    o_ref[...] = _layer_norm_rows(z, g_ref[...], b_ref[...])


def _shared_ln(x, routed_rows, wgu, wd, g, b, tm):
    t, d = x.shape
    return pl.pallas_call(
        _shared_ln_kernel,
        grid=(t // tm,),
        in_specs=[pl.BlockSpec((tm, d), lambda i: (i, 0)),
                  pl.BlockSpec((tm * ROW_VREGS, LANES), lambda i: (i, 0)),
                  pl.BlockSpec(wgu.shape, lambda i: (0, 0)),
                  pl.BlockSpec(wd.shape, lambda i: (0, 0)),
                  pl.BlockSpec((1, d), lambda i: (0, 0)),
                  pl.BlockSpec((1, d), lambda i: (0, 0))],
        out_specs=pl.BlockSpec((tm, d), lambda i: (i, 0)),
        out_shape=jax.ShapeDtypeStruct((t, d), F32),
        compiler_params=_params(("parallel",)),
        name="shared_ln",
    )(x, routed_rows, wgu, wd, g.reshape(1, d), b.reshape(1, d))


def _moe_ln(x, p, g, b, tc):
    slot, wgt, off, cnt = _router(x, p["wr_hi"], p["wr_lo"], p["b_col"], tc)
    lines = lambda a: a.transpose(0, 2, 1).reshape(a.shape[0], tc * TOPK // LANES, LANES)
    routed = _routed_experts(x, lines(slot), lines(wgt), off[:, :, 0].reshape(-1), cnt[:, :, 0].reshape(-1),
                             p["wg"], p["wu"], p["wd"], p["layer"], tc)
    return _shared_ln(x, routed, p["wsgu"], p["wsd"], g, b, min(tc, PROMPT_TILE_ROWS))


def _cumsum_kernel(x_ref, o_ref):
    n = x_ref.shape[0]
    blk = LANES
    r = lax.broadcasted_iota(I32, (blk, blk), 0)
    c = lax.broadcasted_iota(I32, (blk, blk), 1)
    tri = (c <= r).astype(BF16)

    def body(i, carry):
        start = pl.multiple_of(i * blk, blk)
        hi, mid, lo = _split3(x_ref[pl.ds(start, blk), :])
        within = (_bdot(tri, lo) + _bdot(tri, mid)) + _bdot(tri, hi)
        o_ref[pl.ds(start, blk), :] = within + carry
        return carry + jnp.sum(x_ref[pl.ds(start, blk), :], axis=0, keepdims=True)

    lax.fori_loop(0, n // blk, body, jnp.zeros((1, x_ref.shape[1]), F32))


def _cumsum_rows(x, batch, seq):
    return pl.pallas_call(
        _cumsum_kernel,
        grid=(batch,),
        in_specs=[pl.BlockSpec((seq, x.shape[1]), lambda b: (b, 0))],
        out_specs=pl.BlockSpec((seq, x.shape[1]), lambda b: (b, 0)),
        out_shape=jax.ShapeDtypeStruct(x.shape, F32),
        compiler_params=_params(("parallel",)),
        name="logf_cumsum",
    )(x)


HEADS_PER_QBLOCK = LANES // FOX_DH
SUM_ROWS = 16


def _fox_prompt_kernel(q_ref, kt_ref, vt_ref, cumq_ref, cumk_ref, o_ref, *, tq):
    hp = pl.program_id(1)
    qi = pl.program_id(2)
    qf = q_ref[...].astype(F32) * (FOX_DH ** -0.5)
    lane = lax.broadcasted_iota(I32, cumq_ref.shape, 1)
    ones_rows = (lax.broadcasted_iota(I32, (SUM_ROWS, tq), 0) == 0).astype(F32)
    heads = range(HEADS_PER_QBLOCK)
    qs = [qf[:, e * FOX_DH:(e + 1) * FOX_DH].astype(BF16) for e in heads]
    cqs = [jnp.sum(jnp.where(lane == hp * HEADS_PER_QBLOCK + e, cumq_ref[...], 0.0), axis=1, keepdims=True)
           for e in heads]

    def block(kb, carry, diagonal):
        start = pl.multiple_of(kb * tq, tq)
        new = []
        for e in heads:
            m, acc = carry[e]
            rows = slice(e * FOX_DH, (e + 1) * FOX_DH)
            kt = kt_ref[rows, pl.ds(start, tq)].astype(BF16)
            vt = jnp.concatenate([vt_ref[rows, pl.ds(start, tq)], ones_rows], axis=0).astype(BF16)
            ck = cumk_ref[e:e + 1, pl.ds(start, tq)]
            s = _bdot(qs[e], kt) + cqs[e] - ck
            if diagonal:
                qpos = lax.broadcasted_iota(I32, (tq, 1), 0)
                kpos = lax.broadcasted_iota(I32, (1, tq), 1)
                s = jnp.where(kpos <= qpos, s, NEG_INF)
            m_new = jnp.maximum(m, jnp.max(s, axis=1, keepdims=True))
            a = jnp.exp(m - m_new)
            p = jnp.exp(s - m_new)
            new.append((m_new, a * acc + lax.dot_general(p.astype(BF16), vt, _NT, preferred_element_type=F32)))
        return tuple(new)

    init = tuple((jnp.full((tq, 1), -jnp.inf, F32), jnp.zeros((tq, FOX_DH + SUM_ROWS), F32)) for _ in heads)
    carry = lax.fori_loop(0, qi, lambda kb, c: block(kb, c, False), init)
    final = block(qi, carry, True)
    o_ref[...] = jnp.concatenate([acc[:, :FOX_DH] / acc[:, FOX_DH:FOX_DH + 1] for _, acc in final],
                                 axis=1).astype(o_ref.dtype)


def _fox_prompt(q, kt, vt, cum_tok, cum_row, batch, tq):
    t, fd = q.shape
    l = t // batch
    nq = l // tq
    n_pairs = fd // LANES
    kern = functools.partial(_fox_prompt_kernel, tq=tq)
    return pl.pallas_call(
        kern,
        grid=(batch, n_pairs, nq),
        in_specs=[pl.BlockSpec((tq, LANES), lambda bi, hp, qi: (bi * nq + qi, hp)),
                  pl.BlockSpec((None, LANES, l), lambda bi, hp, qi: (bi, hp, 0)),
                  pl.BlockSpec((None, LANES, l), lambda bi, hp, qi: (bi, hp, 0)),
                  pl.BlockSpec((tq, LANES), lambda bi, hp, qi: (bi * nq + qi, 0)),
                  pl.BlockSpec((None, None, HEADS_PER_QBLOCK, l), lambda bi, hp, qi: (bi, hp, 0, 0))],
        out_specs=pl.BlockSpec((tq, LANES), lambda bi, hp, qi: (bi * nq + qi, hp)),
        out_shape=jax.ShapeDtypeStruct((t, fd), BF16),
        compiler_params=_params(("parallel", "parallel", "arbitrary")),
        name="fox_prompt",
    )(q, kt, vt, cum_tok, cum_row)


N_QROWS = 128
PAGES_PER_STEP = 8
TAIL_ONES_ROW0 = 3 * FOX_HEADS


def _fox_sample_kernel(pt_ref, q_ref, lfo_ref, kto_ref, vto_ref, *rest, n_steps, dec_seq):
    del pt_ref
    cache_refs = rest[:3 * PAGES_PER_STEP]
    o_ref = rest[3 * PAGES_PER_STEP]
    qbd_ref, qtail_ref, m_ref, l_ref, acc_ref, carry_ref = rest[3 * PAGES_PER_STEP + 1:]
    step = pl.program_id(1)
    n_iota = lax.broadcasted_iota(I32, (N_QROWS, 1), 0)
    row_head = n_iota // dec_seq
    row_q = n_iota % dec_seq
    pos = lax.broadcasted_iota(I32, (1, PAGE_SIZE), 1)
    r = lax.broadcasted_iota(I32, (PAGE_SIZE, PAGE_SIZE), 0)
    c = lax.broadcasted_iota(I32, (PAGE_SIZE, PAGE_SIZE), 1)

    def lane_sums(x_t, sel):
        hi, mid, lo = _split3(x_t)
        return (_bdot(lo, sel) + _bdot(mid, sel)) + _bdot(hi, sel)

    def bias_rows(key_bias_t):
        hi, mid, lo = _split3(key_bias_t)
        ones_rows = (lax.broadcasted_iota(I32, (SUBLANES, PAGE_SIZE), 0) < 3).astype(F32)
        return jnp.concatenate(
            [hi.astype(F32), mid.astype(F32), lo.astype(F32), ones_rows,
             jnp.zeros((LANES - TAIL_ONES_ROW0 - SUBLANES, PAGE_SIZE), F32)], axis=0).astype(BF16)

    def attend(kts, vts, key_biases_t, valid):
        kt = jnp.concatenate([x.astype(BF16) for x in kts], axis=1)
        vt = jnp.concatenate([x.astype(BF16) for x in vts], axis=1)
        ktail = jnp.concatenate([bias_rows(x) for x in key_biases_t], axis=1)
        s = _bdot(qbd_ref[...], kt) + _bdot(qtail_ref[...], ktail)
        if valid is not None:
            s = jnp.where(valid, s, NEG_INF)
        m_old = m_ref[...]
        m_new = jnp.maximum(m_old, jnp.max(s, axis=1, keepdims=True))
        a = jnp.exp(m_old - m_new)
        p = jnp.exp(s - m_new)
        l_ref[...] = a * l_ref[...] + jnp.sum(p, axis=1, keepdims=True)
        acc_ref[...] = a * acc_ref[...] + lax.dot_general(
            p.astype(BF16), vt, _NT, preferred_element_type=F32)
        m_ref[...] = m_new

    @pl.when(step == 0)
    def _own_rows():
        q = q_ref[0] * (FOX_DH ** -0.5)
        qt = jnp.concatenate([q] * FOX_HEADS, axis=0)
        rr = lax.broadcasted_iota(I32, qt.shape, 0)
        cc = lax.broadcasted_iota(I32, qt.shape, 1)
        qbd_ref[...] = jnp.where(cc // FOX_DH == rr // dec_seq, qt, 0.0).astype(BF16)
        prefix_t = lane_sums(lfo_ref[0], (r <= c).astype(BF16))
        rep = (lax.broadcasted_iota(I32, (N_QROWS, FOX_HEADS), 1)
               == lax.broadcasted_iota(I32, (N_QROWS, FOX_HEADS), 0) // dec_seq).astype(BF16)
        hi, mid, lo = _split3(prefix_t)
        by_row = (_bdot(rep, lo) + _bdot(rep, mid)) + _bdot(rep, hi)
        pre_col = jnp.sum(jnp.where(pos == row_q, by_row, 0.0), axis=1, keepdims=True)
        hi, mid, lo = (x.astype(F32) for x in _split3(pre_col))
        lane = lax.broadcasted_iota(I32, (N_QROWS, LANES), 1)
        tail = jnp.where(jnp.logical_and(lane < TAIL_ONES_ROW0, lane % FOX_HEADS == row_head), 1.0, 0.0)
        tail = jnp.where(lane == TAIL_ONES_ROW0, hi, tail)
        tail = jnp.where(lane == TAIL_ONES_ROW0 + 1, mid, tail)
        tail = jnp.where(lane == TAIL_ONES_ROW0 + 2, lo, tail)
        qtail_ref[...] = tail.astype(BF16)
        m_ref[...] = jnp.full(m_ref.shape, -jnp.inf, F32)
        l_ref[...] = jnp.zeros(l_ref.shape, F32)
        acc_ref[...] = jnp.zeros(acc_ref.shape, F32)
        carry_ref[...] = jnp.zeros(carry_ref.shape, F32)
        valid = jnp.logical_and(pos <= row_q, pos < dec_seq)
        attend([kto_ref[0]], [vto_ref[0]], [-prefix_t], valid)

    @pl.when(step > 0)
    def _cache_pages():
        after = (r > c).astype(BF16)
        carry = carry_ref[...]
        kts, vts, biases = [], [], []
        for u in range(PAGES_PER_STEP):
            kt_ref, vt_ref, lf_ref = cache_refs[3 * u:3 * u + 3]
            lf_t = lf_ref[0]
            kts.append(kt_ref[0])
            vts.append(vt_ref[0])
            biases.append(lane_sums(lf_t, after) + carry)
            carry = carry + jnp.sum(lf_t, axis=1, keepdims=True)
        attend(kts, vts, biases, None)
        carry_ref[...] = carry

    @pl.when(step == n_steps - 1)
    def _finish():
        col_head = lax.broadcasted_iota(I32, (dec_seq, D_MODEL), 1) // FOX_DH
        out = jnp.zeros((dec_seq, D_MODEL), F32)
        for h in range(FOX_HEADS):
            rows = slice(h * dec_seq, (h + 1) * dec_seq)
            out = out + jnp.where(col_head == h, acc_ref[rows, :] / l_ref[rows, :], 0.0)
        o_ref[0] = out


def _fox_sample(q, lft_own, kt_own, vt_own, kt_cache, vt_cache, lft_cache, page_table):
    db, dec_seq, _ = q.shape
    n_pages = page_table.shape[1]
    n_steps = 1 + n_pages // PAGES_PER_STEP
    fd = FOX_HEADS * FOX_DH
    kern = functools.partial(_fox_sample_kernel, n_steps=n_steps, dec_seq=dec_seq)

    def page(u):
        return lambda b, s, pt: (pt[b, n_pages - 1 - (jnp.maximum(s, 1) - 1) * PAGES_PER_STEP - u], 0, 0)

    cache_specs = []
    cache_args = []
    for u in range(PAGES_PER_STEP):
        cache_specs += [pl.BlockSpec((1, fd, PAGE_SIZE), page(u)),
                        pl.BlockSpec((1, fd, PAGE_SIZE), page(u)),
                        pl.BlockSpec((1, FOX_HEADS, PAGE_SIZE), page(u))]
        cache_args += [kt_cache, vt_cache, lft_cache]
    grid_spec = pltpu.PrefetchScalarGridSpec(
        num_scalar_prefetch=1,
        grid=(db, n_steps),
        in_specs=[pl.BlockSpec((1, dec_seq, D_MODEL), lambda b, s, pt: (b, 0, 0)),
                  pl.BlockSpec((1, FOX_HEADS, PAGE_SIZE), lambda b, s, pt: (b, 0, 0)),
                  pl.BlockSpec((1, fd, PAGE_SIZE), lambda b, s, pt: (b, 0, 0)),
                  pl.BlockSpec((1, fd, PAGE_SIZE), lambda b, s, pt: (b, 0, 0))] + cache_specs,
        out_specs=pl.BlockSpec((1, dec_seq, D_MODEL), lambda b, s, pt: (b, 0, 0)),
        scratch_shapes=[pltpu.VMEM((N_QROWS, fd), BF16),
                        pltpu.VMEM((N_QROWS, LANES), BF16),
                        pltpu.VMEM((N_QROWS, 1), F32),
                        pltpu.VMEM((N_QROWS, 1), F32),
                        pltpu.VMEM((N_QROWS, fd), F32),
                        pltpu.VMEM((FOX_HEADS, 1), F32)],
    )
    return pl.pallas_call(
        kern,
        grid_spec=grid_spec,
        out_shape=jax.ShapeDtypeStruct((db, dec_seq, D_MODEL), F32),
        compiler_params=_params(("parallel", "arbitrary")),
        name="fox_sample",
    )(page_table, q, lft_own, kt_own, vt_own, *cache_args)


def _moe_params(layer, w_router, b_router, w_gate, w_up, w_down, ws_gate, ws_up, ws_down):
    wr_t = w_router[layer].T
    wr_hi = wr_t.astype(BF16)
    wr_lo = (wr_t - wr_hi.astype(F32)).astype(BF16)
    return dict(
        wr_hi=wr_hi, wr_lo=wr_lo, b_col=b_router[layer].reshape(N_EXPERTS, 1),
        layer=layer, wg=w_gate, wu=w_up, wd=w_down,
        wsgu=jnp.concatenate([ws_gate[layer], ws_up[layer]], axis=1).astype(BF16),
        wsd=ws_down[layer].astype(BF16))


def kernel(x_prompt, x_sample, state_ret, cache_k, cache_v, cache_logf, page_table, ret_w_in, ret_w_out,
           fox_w_kvf, fox_b_f, fox_w_q, fox_w_out, ln_g, ln_b, moe_w_router, moe_b_router, moe_w_gate,
           moe_w_up, moe_w_down, moe_ws_gate, moe_ws_up, moe_ws_down):
    bp, lp, d = x_prompt.shape
    bs, ls, _ = x_sample.shape
    tp, ts = bp * lp, bs * ls
    assert fox_heads_rows(ls) == N_QROWS
    xp = x_prompt.reshape(tp, d)
    xs = x_sample.reshape(ts, d)
    hq = RET_HEADS * RET_DK
    hv = RET_HEADS * RET_DV
    fd = FOX_HEADS * FOX_DH
    tm_p, tn_p = PROMPT_TILE_ROWS, PROJ_TILE_COLS
    tc_p = MOE_TOKEN_TILE

    wg_all, wu_all, wd_all = moe_w_gate.astype(BF16), moe_w_up.astype(BF16), moe_w_down.astype(BF16)
    moe = [_moe_params(layer, moe_w_router, moe_b_router, wg_all, wu_all, wd_all,
                       moe_ws_gate, moe_ws_up, moe_ws_down) for layer in range(DEPTH)]

    w_in = ret_w_in[0].astype(BF16)
    w_out = ret_w_out[0].astype(BF16)
    cos_p, sin_p = _rope_tables(jnp.arange(lp))
    cos_s, sin_s = _rope_tables(PAST_LEN + jnp.arange(ls))
    cos_s, sin_s = jnp.tile(cos_s, (bs, 1)), jnp.tile(sin_s, (bs, 1))

    qk_p = _qk_rope(xp, w_in, cos_p, sin_p, tm_p)
    v_p = _matmul(xp, w_in, 2 * hq, hv, BF16, tm_p, tn_p)
    g_p = _matmul(xp, w_in, 2 * hq + hv, hv, F32, tm_p, tn_p)
    o_p, st_p = _retention(qk_p, v_p, g_p, jnp.zeros((bp, RET_HEADS, RET_DK, RET_DV), F32),
                           _retention_decays(min(RET_CHUNK, lp)), bp, lp, BF16)
    xp = _mm_res_ln(o_p, w_out, xp, ln_g[0, 0], ln_b[0, 0], tm_p)

    qk_s = _qk_rope(xs, w_in, cos_s, sin_s, ts)
    v_s = _matmul(xs, w_in, 2 * hq, hv, F32, ts, tn_p)
    g_s = _matmul(xs, w_in, 2 * hq + hv, hv, F32, ts, tn_p)
    o_s, st_s = _retention(qk_s, v_s, g_s, state_ret[0], _retention_decays(min(RET_CHUNK, ls)), bs, ls, F32)
    xs = _mm_res_ln(o_s, w_out, xs, ln_g[0, 0], ln_b[0, 0], ts)

    xp = _moe_ln(xp, moe[0], ln_g[0, 1], ln_b[0, 1], tc_p)
    xs = _moe_ln(xs, moe[0], ln_g[0, 1], ln_b[0, 1], ts)

    w_kvf = fox_w_kvf.astype(BF16)
    w_kv_t = w_kvf[:, :2 * fd].T
    w_f_pad = jnp.zeros((d, LANES), BF16).at[:, :FOX_HEADS].set(w_kvf[:, 2 * fd:])
    b_f_pad = jnp.zeros((1, LANES), F32).at[0, :FOX_HEADS].set(fox_b_f)
    kt_p, vt_p = _matmul_t_pair(xp, w_kv_t, bp, tm_p)
    lf_p = _logf_proj(xp, w_f_pad, b_f_pad, tm_p)
    per_seq = lambda a_t: a_t[0].reshape(fd, bs, ls).transpose(1, 0, 2)
    kt_s, vt_s = (per_seq(a) for a in _matmul_t_pair(xs, w_kv_t, 1, ts))
    lf_s = _logf_proj(xs, w_f_pad, b_f_pad, ts)

    as_blhd = lambda a_t, b, l: a_t.reshape(b, FOX_HEADS, FOX_DH, l).transpose(0, 3, 1, 2)
    k_p, v_p4 = as_blhd(kt_p, bp, lp), as_blhd(vt_p, bp, lp)
    k_s, v_s4 = as_blhd(kt_s, bs, ls), as_blhd(vt_s, bs, ls)

    w_q = fox_w_q[0].astype(BF16)
    w_o = fox_w_out[0].astype(BF16)
    q_p = _matmul(xp, w_q, 0, fd, BF16, tm_p, tn_p)
    cum_p = _cumsum_rows(lf_p, bp, lp)
    cum_row = cum_p[:, :FOX_HEADS].reshape(bp, lp, FOX_HEADS).transpose(0, 2, 1).reshape(
        bp, FOX_HEADS // HEADS_PER_QBLOCK, HEADS_PER_QBLOCK, lp)
    att_p = _fox_prompt(q_p, kt_p, vt_p, cum_p, cum_row, bp, ATTN_TILE)
    xp = _mm_res_ln(att_p, w_o, xp, ln_g[1, 0], ln_b[1, 0], tm_p)

    q_s = _matmul(xs, w_q, 0, fd, F32, ts, tn_p)
    n_pool = cache_k.shape[0]
    kt_cache = cache_k.transpose(0, 2, 3, 1).reshape(n_pool, fd, PAGE_SIZE)
    vt_cache = cache_v.transpose(0, 2, 3, 1).reshape(n_pool, fd, PAGE_SIZE)
    lft_cache = cache_logf.transpose(0, 2, 1)
    pad_pos = lambda a_t: jnp.pad(a_t, ((0, 0), (0, 0), (0, PAGE_SIZE - ls)))
    lft_own = pad_pos(lf_s[:, :FOX_HEADS].reshape(bs, ls, FOX_HEADS).transpose(0, 2, 1))
    att_s = _fox_sample(q_s.reshape(bs, ls, d), lft_own, pad_pos(kt_s), pad_pos(vt_s),
                        kt_cache, vt_cache, lft_cache, page_table)
    xs = _mm_res_ln(att_s.reshape(ts, fd), w_o, xs, ln_g[1, 0], ln_b[1, 0], ts)

    xp = _moe_ln(xp, moe[1], ln_g[1, 1], ln_b[1, 1], tc_p)
    xs = _moe_ln(xs, moe[1], ln_g[1, 1], ln_b[1, 1], ts)

    return (xp.reshape(bp, lp, d), xs.reshape(bs, ls, d), st_p[None], st_s[None],
            k_p, v_p4, lf_p[:, :FOX_HEADS].reshape(bp, lp, FOX_HEADS),
            k_s, v_s4, lf_s[:, :FOX_HEADS].reshape(bs, ls, FOX_HEADS))


def fox_heads_rows(dec_seq):
    return FOX_HEADS * dec_seq
```

```python
import functools

import jax
import jax.numpy as jnp
from jax import lax
from jax.experimental import pallas as pl
from jax.experimental.pallas import tpu as pltpu

F32 = jnp.float32
BF16 = jnp.bfloat16
I32 = jnp.int32

D_MODEL = 1024
DEPTH = 2
PAST_LEN = 8192
PAGE_SIZE = 128
N_A_LAYERS = DEPTH // 2
RET_HEADS = 4
RET_DK = D_MODEL // RET_HEADS
RET_DV = 2 * RET_DK
RET_CHUNK = 128
ROPE_BASE = 10000.0
FOX_HEADS = 16
FOX_DH = D_MODEL // FOX_HEADS
NEG_INF = -1e30
N_EXPERTS = 64
TOPK = 8
N_GROUPS = 8
GROUP_SIZE = N_EXPERTS // N_GROUPS
TOPK_GROUPS = 4
FF_EXPERT = D_MODEL // 4
ROUTED_SCALE = 2.5
DEEPNORM_ALPHA = (2.0 * DEPTH) ** 0.25
LN_EPS = 1e-5
GN_EPS = 1e-5

LANES = 128
SUBLANES = 8
ROW_VREGS = D_MODEL // LANES
EXP_BLOCK = 160
EXPERTS_PER_STEP = 4
VMEM_LIMIT_BIG = 60 * 1024 * 1024
VMEM_LIMIT = 48 * 1024 * 1024

PROMPT_TILE_ROWS = 1024
PROJ_TILE_COLS = 1024
MOE_TOKEN_TILE = 1024
ATTN_TILE = 512

_NT = (((1,), (1,)), ((), ()))
_TN = (((0,), (0,)), ((), ()))


def _params(sem, vmem=VMEM_LIMIT):
    return pltpu.CompilerParams(dimension_semantics=sem, vmem_limit_bytes=vmem)


def _bdot(a, b):
    return jnp.dot(a, b, preferred_element_type=F32)


def _layer_norm_rows(z, g, b):
    mu = jnp.mean(z, axis=-1, keepdims=True)
    zc = z - mu
    var = jnp.mean(zc * zc, axis=-1, keepdims=True)
    return zc * lax.rsqrt(var + LN_EPS) * g + b


def _silu(x):
    return x * jax.nn.sigmoid(x)


def _split3(x):
    hi = x.astype(BF16)
    r1 = x - hi.astype(F32)
    mid = r1.astype(BF16)
    lo = (r1 - mid.astype(F32)).astype(BF16)
    return hi, mid, lo


def _mm_kernel(x_ref, w_ref, o_ref):
    o_ref[...] = _bdot(x_ref[...].astype(BF16), w_ref[...]).astype(o_ref.dtype)


def _matmul(x, w, col0, n, out_dtype, tm, tn):
    m, k = x.shape
    j0 = col0 // tn
    return pl.pallas_call(
        _mm_kernel,
        grid=(m // tm, n // tn),
        in_specs=[pl.BlockSpec((tm, k), lambda i, j: (i, 0)),
                  pl.BlockSpec((k, tn), lambda i, j: (0, j + j0))],
        out_specs=pl.BlockSpec((tm, tn), lambda i, j: (i, j)),
        out_shape=jax.ShapeDtypeStruct((m, n), out_dtype),
        compiler_params=_params(("parallel", "arbitrary")),
        name="proj",
    )(x, w)


def _mm_t_pair_kernel(x_ref, wt_ref, o0_ref, o1_ref):
    j = pl.program_id(1)
    res = lax.dot_general(wt_ref[...], x_ref[...].astype(BF16), _NT, preferred_element_type=F32)

    @pl.when(j == 0)
    def _():
        o0_ref[...] = res

    @pl.when(j == 1)
    def _():
        o1_ref[...] = res


def _matmul_t_pair(x, w_t, batch, tm):
    m, k = x.shape
    n = w_t.shape[0] // 2
    seq = m // batch
    nper = seq // tm
    out_spec = pl.BlockSpec((None, n, tm), lambda i, j: (i // nper, 0, i % nper))
    return pl.pallas_call(
        _mm_t_pair_kernel,
        grid=(m // tm, 2),
        in_specs=[pl.BlockSpec((tm, k), lambda i, j: (i, 0)),
                  pl.BlockSpec((n, k), lambda i, j: (j, 0))],
        out_specs=[out_spec, out_spec],
        out_shape=[jax.ShapeDtypeStruct((batch, n, seq), F32)] * 2,
        compiler_params=_params(("parallel", "arbitrary")),
        name="proj_t",
    )(x, w_t)


def _mm_rope_kernel(x_ref, w_ref, cos_ref, sin_ref, o_ref, *, k_scale):
    j = pl.program_id(1)
    acc = _bdot(x_ref[...].astype(BF16), w_ref[...])
    half = RET_DK // 2
    c = cos_ref[...]
    s = sin_ref[...]
    sc = jnp.where(j == 0, 1.0, k_scale).astype(F32)
    for h in range(RET_HEADS):
        lo = h * RET_DK
        x1 = acc[:, lo:lo + half]
        x2 = acc[:, lo + half:lo + RET_DK]
        o_ref[:, lo:lo + half] = (x1 * c - x2 * s) * sc
        o_ref[:, lo + half:lo + RET_DK] = (x2 * c + x1 * s) * sc


def _qk_rope(x, w, cos_rows, sin_rows, tm):
    m, k = x.shape
    hq = RET_HEADS * RET_DK
    nper = cos_rows.shape[0] // tm
    kern = functools.partial(_mm_rope_kernel, k_scale=RET_DK ** -0.5)
    return pl.pallas_call(
        kern,
        grid=(m // tm, 2),
        in_specs=[pl.BlockSpec((tm, k), lambda i, j: (i, 0)),
                  pl.BlockSpec((k, hq), lambda i, j: (0, j)),
                  pl.BlockSpec((tm, RET_DK // 2), lambda i, j: (i % nper, 0)),
                  pl.BlockSpec((tm, RET_DK // 2), lambda i, j: (i % nper, 0))],
        out_specs=pl.BlockSpec((tm, hq), lambda i, j: (i, j)),
        out_shape=jax.ShapeDtypeStruct((m, 2 * hq), F32),
        compiler_params=_params(("parallel", "arbitrary")),
        name="qk_rope",
    )(x, w, cos_rows, sin_rows)


def _mm_res_ln_kernel(a_ref, w_ref, x_ref, g_ref, b_ref, o_ref):
    mix = _bdot(a_ref[...].astype(BF16), w_ref[...])
    z = DEEPNORM_ALPHA * x_ref[...] + mix
    o_ref[...] = _layer_norm_rows(z, g_ref[...], b_ref[...])


def _mm_res_ln(a, w, x, g, b, tm):
    m, k = a.shape
    d = w.shape[1]
    return pl.pallas_call(
        _mm_res_ln_kernel,
        grid=(m // tm,),
        in_specs=[pl.BlockSpec((tm, k), lambda i: (i, 0)),
                  pl.BlockSpec((k, d), lambda i: (0, 0)),
                  pl.BlockSpec((tm, d), lambda i: (i, 0)),
                  pl.BlockSpec((1, d), lambda i: (0, 0)),
                  pl.BlockSpec((1, d), lambda i: (0, 0))],
        out_specs=pl.BlockSpec((tm, d), lambda i: (i, 0)),
        out_shape=jax.ShapeDtypeStruct((m, d), F32),
        compiler_params=_params(("parallel",)),
        name="mix_out_ln",
    )(a, w, x, g.reshape(1, d), b.reshape(1, d))


def _mm_logsig_kernel(x_ref, w_ref, b_ref, o_ref):
    z = _bdot(x_ref[...].astype(BF16), w_ref[...]) + b_ref[...]
    o_ref[...] = -(jnp.maximum(-z, 0.0) + jnp.log1p(jnp.exp(-jnp.abs(z))))


def _logf_proj(x, w_pad, b_pad, tm):
    m, k = x.shape
    return pl.pallas_call(
        _mm_logsig_kernel,
        grid=(m // tm,),
        in_specs=[pl.BlockSpec((tm, k), lambda i: (i, 0)),
                  pl.BlockSpec((k, LANES), lambda i: (0, 0)),
                  pl.BlockSpec((1, LANES), lambda i: (0, 0))],
        out_specs=pl.BlockSpec((tm, LANES), lambda i: (i, 0)),
        out_shape=jax.ShapeDtypeStruct((m, LANES), F32),
        compiler_params=_params(("parallel",)),
        name="logf_proj",
    )(x, w_pad, b_pad)


def _ret_kernel(q_ref, k_ref, v_ref, g_ref, s0_ref, din_ref, dq_ref, dk_ref, dc_ref,
                o_ref, sout_ref, state_ref):
    c = pl.program_id(1)

    @pl.when(c == 0)
    def _():
        state_ref[...] = s0_ref[...]

    for h in range(RET_HEADS):
        qcols = slice(h * RET_DK, (h + 1) * RET_DK)
        vcols = slice(h * RET_DV, (h + 1) * RET_DV)
        q = q_ref[:, qcols].astype(BF16)
        k = k_ref[:, qcols]
        v = v_ref[:, vcols].astype(BF16)
        s_prev = state_ref[h]
        scores = lax.dot_general(q, k.astype(BF16), _NT, preferred_element_type=F32) * din_ref[h]
        o = _bdot(scores.astype(BF16), v) + _bdot(q, s_prev.astype(BF16)) * dq_ref[h]
        kd = (k * dk_ref[h]).astype(BF16)
        s_new = s_prev * dc_ref[h] + lax.dot_general(kd, v, _TN, preferred_element_type=F32)
        state_ref[h] = s_new

        mu = jnp.mean(o, axis=-1, keepdims=True)
        oc = o - mu
        var = jnp.mean(oc * oc, axis=-1, keepdims=True)
        on = oc * lax.rsqrt(var + GN_EPS)
        o_ref[:, vcols] = (_silu(g_ref[:, vcols]) * on).astype(o_ref.dtype)

    @pl.when(c == pl.num_programs(1) - 1)
    def _():
        sout_ref[...] = state_ref[...]


def _retention(qk, v, g, s0, decays, batch, seq, out_dtype):
    chunk = min(RET_CHUNK, seq)
    nc = seq // chunk
    din, dq, dk, dc = decays
    h_ = RET_HEADS
    hq, hv = h_ * RET_DK, h_ * RET_DV
    whole = lambda a: pl.BlockSpec(a.shape, lambda b, c: (0,) * a.ndim)
    return pl.pallas_call(
        _ret_kernel,
        grid=(batch, nc),
        in_specs=[pl.BlockSpec((chunk, hq), lambda b, c: (b * nc + c, 0)),
                  pl.BlockSpec((chunk, hq), lambda b, c: (b * nc + c, 1)),
                  pl.BlockSpec((chunk, hv), lambda b, c: (b * nc + c, 0)),
                  pl.BlockSpec((chunk, hv), lambda b, c: (b * nc + c, 0)),
                  pl.BlockSpec((None, h_, RET_DK, RET_DV), lambda b, c: (b, 0, 0, 0)),
                  whole(din), whole(dq), whole(dk), whole(dc)],
        out_specs=[pl.BlockSpec((chunk, hv), lambda b, c: (b * nc + c, 0)),
                   pl.BlockSpec((None, h_, RET_DK, RET_DV), lambda b, c: (b, 0, 0, 0))],
        out_shape=[jax.ShapeDtypeStruct((batch * seq, hv), out_dtype),
                   jax.ShapeDtypeStruct((batch, h_, RET_DK, RET_DV), F32)],
        scratch_shapes=[pltpu.VMEM((h_, RET_DK, RET_DV), F32)],
        compiler_params=_params(("parallel", "arbitrary")),
        name="retention",
    )(qk, qk, v, g, s0, din, dq, dk, dc)


def _retention_decays(chunk):
    h = jnp.arange(RET_HEADS, dtype=F32)
    log_g = jnp.log1p(-(2.0 ** (-5.0 - h)))
    idx = jnp.arange(chunk, dtype=F32)
    rel = idx[:, None] - idx[None, :]
    din = jnp.where(rel >= 0, jnp.exp(log_g[:, None, None] * jnp.maximum(rel, 0.0)), 0.0)
    dq = jnp.exp(log_g[:, None] * (idx[None, :] + 1.0))[:, :, None]
    dk = jnp.exp(log_g[:, None] * (chunk - 1.0 - idx[None, :]))[:, :, None]
    dc = jnp.exp(log_g * chunk)[:, None, None]
    return din, dq, dk, dc


def _rope_tables(pos):
    half = RET_DK // 2
    inv_freq = ROPE_BASE ** (-jnp.arange(half, dtype=F32) / half)
    ang = pos.astype(F32)[:, None] * inv_freq[None, :]
    return jnp.cos(ang), jnp.sin(ang)


def _router_kernel(x_ref, wh_ref, wl_ref, b_ref, slot_ref, wgt_ref, off_ref, cnt_ref):
    tc = x_ref.shape[0]
    x = x_ref[...]
    xh = x.astype(BF16)
    xl = (x - xh.astype(F32)).astype(BF16)
    wh = wh_ref[...]
    wl = wl_ref[...]
    logits = (lax.dot_general(wh, xh, _NT, preferred_element_type=F32)
              + lax.dot_general(wh, xl, _NT, preferred_element_type=F32)
              + lax.dot_general(wl, xh, _NT, preferred_element_type=F32))
    scores = jax.nn.sigmoid(logits)
    biased = scores + b_ref[...]

    jj = lax.broadcasted_iota(I32, (GROUP_SIZE, tc), 0)
    groups = [biased[g * GROUP_SIZE:(g + 1) * GROUP_SIZE, :] for g in range(N_GROUPS)]
    gscore = []
    for rows in groups:
        m1 = jnp.max(rows, axis=0, keepdims=True)
        j1 = jnp.min(jnp.where(rows == m1, jj, GROUP_SIZE), axis=0, keepdims=True)
        m2 = jnp.max(jnp.where(jj == j1, -jnp.inf, rows), axis=0, keepdims=True)
        gscore.append(m1 + m2)
    gsel = [jnp.zeros((1, tc), dtype=jnp.bool_) for _ in range(N_GROUPS)]
    for _ in range(TOPK_GROUPS):
        gm = gscore[0]
        for sc in gscore[1:]:
            gm = jnp.maximum(gm, sc)
        found = jnp.zeros((1, tc), dtype=jnp.bool_)
        for g in range(N_GROUPS):
            hit = jnp.logical_and(gscore[g] == gm, jnp.logical_not(found))
            found = jnp.logical_or(found, hit)
            gsel[g] = jnp.logical_or(gsel[g], hit)
            gscore[g] = jnp.where(hit, -jnp.inf, gscore[g])
    masked = jnp.concatenate(
        [jnp.where(jnp.broadcast_to(gsel[g], groups[g].shape), groups[g], -jnp.inf)
         for g in range(N_GROUPS)], axis=0)
    ei = lax.broadcasted_iota(I32, masked.shape, 0)
    hits = []
    wk = []
    for _ in range(TOPK):
        m = jnp.max(masked, axis=0, keepdims=True)
        first = jnp.min(jnp.where(masked == m, ei, N_EXPERTS), axis=0, keepdims=True)
        hit = ei == first
        hits.append(hit)
        wk.append(jnp.sum(jnp.where(hit, scores, 0.0), axis=0, keepdims=True))
        masked = jnp.where(hit, -jnp.inf, masked)
    wsum = wk[0]
    for w in wk[1:]:
        wsum = wsum + w
    wgt_ref[...] = jnp.concatenate([w / wsum * ROUTED_SCALE for w in wk], axis=0)

    chosen = hits[0]
    for hit in hits[1:]:
        chosen = jnp.logical_or(chosen, hit)
    chosen = chosen.astype(BF16)
    r = lax.broadcasted_iota(I32, (tc, tc), 0)
    c = lax.broadcasted_iota(I32, (tc, tc), 1)
    rank = _bdot(chosen, (r <= c).astype(BF16))
    er = lax.broadcasted_iota(I32, (N_EXPERTS, N_EXPERTS), 0)
    ec = lax.broadcasted_iota(I32, (N_EXPERTS, N_EXPERTS), 1)
    below = _bdot((ec < er).astype(BF16), chosen)
    off = jnp.sum(below, axis=1, keepdims=True)
    cnt = rank[:, tc - 1:tc]
    pos = (off + rank - 1.0) * ROW_VREGS
    slot_ref[...] = jnp.concatenate(
        [jnp.sum(jnp.where(hit, pos, 0.0), axis=0, keepdims=True) for hit in hits], axis=0).astype(I32)
    off_ref[...] = jnp.broadcast_to(off, off_ref.shape).astype(I32)
    cnt_ref[...] = jnp.broadcast_to(cnt, cnt_ref.shape).astype(I32)


def _router(x, wr_hi, wr_lo, b_col, tc):
    t, d = x.shape
    n = t // tc
    return pl.pallas_call(
        _router_kernel,
        grid=(n,),
        in_specs=[pl.BlockSpec((tc, d), lambda i: (i, 0)),
                  pl.BlockSpec((N_EXPERTS, d), lambda i: (0, 0)),
                  pl.BlockSpec((N_EXPERTS, d), lambda i: (0, 0)),
                  pl.BlockSpec((N_EXPERTS, 1), lambda i: (0, 0))],
        out_specs=[pl.BlockSpec((None, TOPK, tc), lambda i: (i, 0, 0)),
                   pl.BlockSpec((None, TOPK, tc), lambda i: (i, 0, 0)),
                   pl.BlockSpec((None, N_EXPERTS, LANES), lambda i: (i, 0, 0)),
                   pl.BlockSpec((None, N_EXPERTS, LANES), lambda i: (i, 0, 0))],
        out_shape=[jax.ShapeDtypeStruct((n, TOPK, tc), I32),
                   jax.ShapeDtypeStruct((n, TOPK, tc), F32),
                   jax.ShapeDtypeStruct((n, N_EXPERTS, LANES), I32),
                   jax.ShapeDtypeStruct((n, N_EXPERTS, LANES), I32)],
        compiler_params=_params(("parallel",)),
        name="router",
    )(x, wr_hi, wr_lo, b_col)


def _experts_kernel(off_ref, cnt_ref, slot_ref, wgt_ref, x_ref, wg_ref, wu_ref, wd_ref,
                    o_ref, sorted_ref):
    tc = x_ref.shape[0]
    ci = pl.program_id(0)
    e = pl.program_id(1)
    tok_per_line = LANES // TOPK

    @pl.when(e == 0)
    def _dispatch():
        for kc in range(ROW_VREGS):
            o_ref[pl.ds(kc, tc, stride=ROW_VREGS), :] = x_ref[:, kc * LANES:(kc + 1) * LANES]
        sorted_ref[pl.ds(TOPK * tc * ROW_VREGS, EXP_BLOCK * ROW_VREGS), :] = jnp.zeros(
            (EXP_BLOCK * ROW_VREGS, LANES), F32)

        def body(i, carry):
            slot_line = slot_ref.at[0, i]
            for u in range(tok_per_line):
                t = i * tok_per_line + u
                row = o_ref[pl.ds(pl.multiple_of(t * ROW_VREGS, ROW_VREGS), ROW_VREGS), :]
                for k in range(TOPK):
                    s = slot_line[u * TOPK + k]
                    sorted_ref[pl.ds(pl.multiple_of(s, ROW_VREGS), ROW_VREGS), :] = row
            return carry

        lax.fori_loop(0, tc // tok_per_line, body, 0)

    base = ci * N_EXPERTS + e * EXPERTS_PER_STEP
    offs = [off_ref[base + j] for j in range(EXPERTS_PER_STEP)]
    cnts = [cnt_ref[base + j] for j in range(EXPERTS_PER_STEP)]
    nblks = [lax.div(cnt + (EXP_BLOCK - 1), EXP_BLOCK) for cnt in cnts]
    nmax = nblks[0]
    for nb in nblks[1:]:
        nmax = jnp.maximum(nmax, nb)
    rowid = lax.broadcasted_iota(I32, (EXP_BLOCK, 1), 0)

    def ffn_blocks(i, carry):
        loaded = []
        for j in range(EXPERTS_PER_STEP):
            active = i < nblks[j]
            row0 = jnp.where(active, offs[j] + i * EXP_BLOCK, TOPK * tc)
            nvalid = jnp.where(active, cnts[j] - i * EXP_BLOCK, 0)
            r0 = row0 * ROW_VREGS
            xbb = jnp.concatenate(
                [sorted_ref[pl.ds(r0 + kc, EXP_BLOCK, stride=ROW_VREGS), :].astype(BF16)
                 for kc in range(ROW_VREGS)], axis=1)
            loaded.append((r0, nvalid, xbb))
        outs = []
        for j, (r0, nvalid, xbb) in enumerate(loaded):
            h = _silu(_bdot(xbb, wg_ref[j])) * _bdot(xbb, wu_ref[j])
            y = _bdot(h.astype(BF16), wd_ref[j])
            outs.append(jnp.where(rowid < nvalid, y, xbb.astype(F32)))
        for (r0, _, _), y in zip(loaded, outs):
            for kc in range(ROW_VREGS):
                sorted_ref[pl.ds(r0 + kc, EXP_BLOCK, stride=ROW_VREGS), :] = y[:, kc * LANES:(kc + 1) * LANES]
        return carry

    lax.fori_loop(0, nmax, ffn_blocks, 0)

    @pl.when(e == pl.num_programs(1) - 1)
    def _combine():
        def body(i, carry):
            slot_line = slot_ref.at[0, i]
            wgt_line = wgt_ref.at[0, i]
            for u in range(tok_per_line):
                t = i * tok_per_line + u
                acc = None
                for k in range(TOPK):
                    s = slot_line[u * TOPK + k]
                    term = wgt_line[u * TOPK + k] * sorted_ref[
                        pl.ds(pl.multiple_of(s, ROW_VREGS), ROW_VREGS), :]
                    acc = term if acc is None else acc + term
                o_ref[pl.ds(pl.multiple_of(t * ROW_VREGS, ROW_VREGS), ROW_VREGS), :] = acc
            return carry

        lax.fori_loop(0, tc // tok_per_line, body, 0)


def _routed_experts(x, slot, wgt, seg_off, seg_cnt, wg, wu, wd, layer, tc):
    t, d = x.shape
    n = t // tc
    sorted_rows = TOPK * tc + EXP_BLOCK
    grid_spec = pltpu.PrefetchScalarGridSpec(
        num_scalar_prefetch=2,
        grid=(n, N_EXPERTS // EXPERTS_PER_STEP),
        in_specs=[pl.BlockSpec((1,) + slot.shape[1:], lambda c, e, *_: (c, 0, 0), memory_space=pltpu.SMEM),
                  pl.BlockSpec((1,) + wgt.shape[1:], lambda c, e, *_: (c, 0, 0), memory_space=pltpu.SMEM),
                  pl.BlockSpec((tc, d), lambda c, e, *_: (c, 0), pipeline_mode=pl.Buffered(1)),
                  pl.BlockSpec((None, EXPERTS_PER_STEP, d, FF_EXPERT), lambda c, e, *_: (layer, e, 0, 0)),
                  pl.BlockSpec((None, EXPERTS_PER_STEP, d, FF_EXPERT), lambda c, e, *_: (layer, e, 0, 0)),
                  pl.BlockSpec((None, EXPERTS_PER_STEP, FF_EXPERT, d), lambda c, e, *_: (layer, e, 0, 0))],
        out_specs=pl.BlockSpec((tc * ROW_VREGS, LANES), lambda c, e, *_: (c, 0), pipeline_mode=pl.Buffered(1)),
        scratch_shapes=[pltpu.VMEM((sorted_rows * ROW_VREGS, LANES), F32)],
    )
    return pl.pallas_call(
        _experts_kernel,
        grid_spec=grid_spec,
        out_shape=jax.ShapeDtypeStruct((t * ROW_VREGS, LANES), F32),
        compiler_params=_params(("arbitrary", "arbitrary"), VMEM_LIMIT_BIG),
        name="routed_experts",
    )(seg_off, seg_cnt, slot, wgt, x, wg, wu, wd)


def _shared_ln_kernel(x_ref, r_ref, wgu_ref, wd_ref, g_ref, b_ref, o_ref):
    tm = x_ref.shape[0]
    x = x_ref[...]
    gu = _bdot(x.astype(BF16), wgu_ref[...])
    ff = wgu_ref.shape[1] // 2
    h = _silu(gu[:, :ff]) * gu[:, ff:]
    shared = _bdot(h.astype(BF16), wd_ref[...])
    routed = jnp.concatenate(
        [r_ref[pl.ds(kc, tm, stride=ROW_VREGS), :] for kc in range(ROW_VREGS)], axis=1)
    z = DEEPNORM_ALPHA * x + (shared + routed)
    o_ref[...] = _layer_norm_rows(z, g_ref[...], b_ref[...])


def _shared_ln(x, routed_rows, wgu, wd, g, b, tm):
    t, d = x.shape
    return pl.pallas_call(
        _shared_ln_kernel,
        grid=(t // tm,),
        in_specs=[pl.BlockSpec((tm, d), lambda i: (i, 0)),
                  pl.BlockSpec((tm * ROW_VREGS, LANES), lambda i: (i, 0)),
                  pl.BlockSpec(wgu.shape, lambda i: (0, 0)),
                  pl.BlockSpec(wd.shape, lambda i: (0, 0)),
                  pl.BlockSpec((1, d), lambda i: (0, 0)),
                  pl.BlockSpec((1, d), lambda i: (0, 0))],
        out_specs=pl.BlockSpec((tm, d), lambda i: (i, 0)),
        out_shape=jax.ShapeDtypeStruct((t, d), F32),
        compiler_params=_params(("parallel",)),
        name="shared_ln",
    )(x, routed_rows, wgu, wd, g.reshape(1, d), b.reshape(1, d))


def _moe_ln(x, p, g, b, tc):
    slot, wgt, off, cnt = _router(x, p["wr_hi"], p["wr_lo"], p["b_col"], tc)
    lines = lambda a: a.transpose(0, 2, 1).reshape(a.shape[0], tc * TOPK // LANES, LANES)
    routed = _routed_experts(x, lines(slot), lines(wgt), off[:, :, 0].reshape(-1), cnt[:, :, 0].reshape(-1),
                             p["wg"], p["wu"], p["wd"], p["layer"], tc)
    return _shared_ln(x, routed, p["wsgu"], p["wsd"], g, b, min(tc, PROMPT_TILE_ROWS))


def _cumsum_kernel(x_ref, o_ref):
    n = x_ref.shape[0]
    blk = LANES
    r = lax.broadcasted_iota(I32, (blk, blk), 0)
    c = lax.broadcasted_iota(I32, (blk, blk), 1)
    tri = (c <= r).astype(BF16)

    def body(i, carry):
        start = pl.multiple_of(i * blk, blk)
        hi, mid, lo = _split3(x_ref[pl.ds(start, blk), :])
        within = (_bdot(tri, lo) + _bdot(tri, mid)) + _bdot(tri, hi)
        o_ref[pl.ds(start, blk), :] = within + carry
        return carry + jnp.sum(x_ref[pl.ds(start, blk), :], axis=0, keepdims=True)

    lax.fori_loop(0, n // blk, body, jnp.zeros((1, x_ref.shape[1]), F32))


def _cumsum_rows(x, batch, seq):
    return pl.pallas_call(
        _cumsum_kernel,
        grid=(batch,),
        in_specs=[pl.BlockSpec((seq, x.shape[1]), lambda b: (b, 0))],
        out_specs=pl.BlockSpec((seq, x.shape[1]), lambda b: (b, 0)),
        out_shape=jax.ShapeDtypeStruct(x.shape, F32),
        compiler_params=_params(("parallel",)),
        name="logf_cumsum",
    )(x)


HEADS_PER_QBLOCK = LANES // FOX_DH
SUM_ROWS = 16


def _fox_prompt_kernel(q_ref, kt_ref, vt_ref, cumq_ref, cumk_ref, o_ref, *, tq):
    hp = pl.program_id(1)
    qi = pl.program_id(2)
    qf = q_ref[...].astype(F32) * (FOX_DH ** -0.5)
    lane = lax.broadcasted_iota(I32, cumq_ref.shape, 1)
    ones_rows = (lax.broadcasted_iota(I32, (SUM_ROWS, tq), 0) == 0).astype(F32)
    heads = range(HEADS_PER_QBLOCK)
    qs = [qf[:, e * FOX_DH:(e + 1) * FOX_DH].astype(BF16) for e in heads]
    cqs = [jnp.sum(jnp.where(lane == hp * HEADS_PER_QBLOCK + e, cumq_ref[...], 0.0), axis=1, keepdims=True)
           for e in heads]

    def block(kb, carry, diagonal):
        start = pl.multiple_of(kb * tq, tq)
        new = []
        for e in heads:
            m, acc = carry[e]
            rows = slice(e * FOX_DH, (e + 1) * FOX_DH)
            kt = kt_ref[rows, pl.ds(start, tq)].astype(BF16)
            vt = jnp.concatenate([vt_ref[rows, pl.ds(start, tq)], ones_rows], axis=0).astype(BF16)
            ck = cumk_ref[e:e + 1, pl.ds(start, tq)]
            s = _bdot(qs[e], kt) + cqs[e] - ck
            if diagonal:
                qpos = lax.broadcasted_iota(I32, (tq, 1), 0)
                kpos = lax.broadcasted_iota(I32, (1, tq), 1)
                s = jnp.where(kpos <= qpos, s, NEG_INF)
            m_new = jnp.maximum(m, jnp.max(s, axis=1, keepdims=True))
            a = jnp.exp(m - m_new)
            p = jnp.exp(s - m_new)
            new.append((m_new, a * acc + lax.dot_general(p.astype(BF16), vt, _NT, preferred_element_type=F32)))
        return tuple(new)

    init = tuple((jnp.full((tq, 1), -jnp.inf, F32), jnp.zeros((tq, FOX_DH + SUM_ROWS), F32)) for _ in heads)
    carry = lax.fori_loop(0, qi, lambda kb, c: block(kb, c, False), init)
    final = block(qi, carry, True)
    o_ref[...] = jnp.concatenate([acc[:, :FOX_DH] / acc[:, FOX_DH:FOX_DH + 1] for _, acc in final],
                                 axis=1).astype(o_ref.dtype)


def _fox_prompt(q, kt, vt, cum_tok, cum_row, batch, tq):
    t, fd = q.shape
    l = t // batch
    nq = l // tq
    n_pairs = fd // LANES
    kern = functools.partial(_fox_prompt_kernel, tq=tq)
    return pl.pallas_call(
        kern,
        grid=(batch, n_pairs, nq),
        in_specs=[pl.BlockSpec((tq, LANES), lambda bi, hp, qi: (bi * nq + qi, hp)),
                  pl.BlockSpec((None, LANES, l), lambda bi, hp, qi: (bi, hp, 0)),
                  pl.BlockSpec((None, LANES, l), lambda bi, hp, qi: (bi, hp, 0)),
                  pl.BlockSpec((tq, LANES), lambda bi, hp, qi: (bi * nq + qi, 0)),
                  pl.BlockSpec((None, None, HEADS_PER_QBLOCK, l), lambda bi, hp, qi: (bi, hp, 0, 0))],
        out_specs=pl.BlockSpec((tq, LANES), lambda bi, hp, qi: (bi * nq + qi, hp)),
        out_shape=jax.ShapeDtypeStruct((t, fd), BF16),
        compiler_params=_params(("parallel", "parallel", "arbitrary")),
        name="fox_prompt",
    )(q, kt, vt, cum_tok, cum_row)


N_QROWS = 128
PAGES_PER_STEP = 8
TAIL_ONES_ROW0 = 3 * FOX_HEADS


def _fox_sample_kernel(pt_ref, q_ref, lfo_ref, kto_ref, vto_ref, *rest, n_steps, dec_seq):
    del pt_ref
    cache_refs = rest[:3 * PAGES_PER_STEP]
    o_ref = rest[3 * PAGES_PER_STEP]
    qbd_ref, qtail_ref, m_ref, l_ref, acc_ref, carry_ref = rest[3 * PAGES_PER_STEP + 1:]
    step = pl.program_id(1)
    n_iota = lax.broadcasted_iota(I32, (N_QROWS, 1), 0)
    row_head = n_iota // dec_seq
    row_q = n_iota % dec_seq
    pos = lax.broadcasted_iota(I32, (1, PAGE_SIZE), 1)
    r = lax.broadcasted_iota(I32, (PAGE_SIZE, PAGE_SIZE), 0)
    c = lax.broadcasted_iota(I32, (PAGE_SIZE, PAGE_SIZE), 1)

    def lane_sums(x_t, sel):
        hi, mid, lo = _split3(x_t)
        return (_bdot(lo, sel) + _bdot(mid, sel)) + _bdot(hi, sel)

    def bias_rows(key_bias_t):
        hi, mid, lo = _split3(key_bias_t)
        ones_rows = (lax.broadcasted_iota(I32, (SUBLANES, PAGE_SIZE), 0) < 3).astype(F32)
        return jnp.concatenate(
            [hi.astype(F32), mid.astype(F32), lo.astype(F32), ones_rows,
             jnp.zeros((LANES - TAIL_ONES_ROW0 - SUBLANES, PAGE_SIZE), F32)], axis=0).astype(BF16)

    def attend(kts, vts, key_biases_t, valid):
        kt = jnp.concatenate([x.astype(BF16) for x in kts], axis=1)
        vt = jnp.concatenate([x.astype(BF16) for x in vts], axis=1)
        ktail = jnp.concatenate([bias_rows(x) for x in key_biases_t], axis=1)
        s = _bdot(qbd_ref[...], kt) + _bdot(qtail_ref[...], ktail)
        if valid is not None:
            s = jnp.where(valid, s, NEG_INF)
        m_old = m_ref[...]
        m_new = jnp.maximum(m_old, jnp.max(s, axis=1, keepdims=True))
        a = jnp.exp(m_old - m_new)
        p = jnp.exp(s - m_new)
        l_ref[...] = a * l_ref[...] + jnp.sum(p, axis=1, keepdims=True)
        acc_ref[...] = a * acc_ref[...] + lax.dot_general(
            p.astype(BF16), vt, _NT, preferred_element_type=F32)
        m_ref[...] = m_new

    @pl.when(step == 0)
    def _own_rows():
        q = q_ref[0] * (FOX_DH ** -0.5)
        qt = jnp.concatenate([q] * FOX_HEADS, axis=0)
        rr = lax.broadcasted_iota(I32, qt.shape, 0)
        cc = lax.broadcasted_iota(I32, qt.shape, 1)
        qbd_ref[...] = jnp.where(cc // FOX_DH == rr // dec_seq, qt, 0.0).astype(BF16)
        prefix_t = lane_sums(lfo_ref[0], (r <= c).astype(BF16))
        rep = (lax.broadcasted_iota(I32, (N_QROWS, FOX_HEADS), 1)
               == lax.broadcasted_iota(I32, (N_QROWS, FOX_HEADS), 0) // dec_seq).astype(BF16)
        hi, mid, lo = _split3(prefix_t)
        by_row = (_bdot(rep, lo) + _bdot(rep, mid)) + _bdot(rep, hi)
        pre_col = jnp.sum(jnp.where(pos == row_q, by_row, 0.0), axis=1, keepdims=True)
        hi, mid, lo = (x.astype(F32) for x in _split3(pre_col))
        lane = lax.broadcasted_iota(I32, (N_QROWS, LANES), 1)
        tail = jnp.where(jnp.logical_and(lane < TAIL_ONES_ROW0, lane % FOX_HEADS == row_head), 1.0, 0.0)
        tail = jnp.where(lane == TAIL_ONES_ROW0, hi, tail)
        tail = jnp.where(lane == TAIL_ONES_ROW0 + 1, mid, tail)
        tail = jnp.where(lane == TAIL_ONES_ROW0 + 2, lo, tail)
        qtail_ref[...] = tail.astype(BF16)
        m_ref[...] = jnp.full(m_ref.shape, -jnp.inf, F32)
        l_ref[...] = jnp.zeros(l_ref.shape, F32)
        acc_ref[...] = jnp.zeros(acc_ref.shape, F32)
        carry_ref[...] = jnp.zeros(carry_ref.shape, F32)
        valid = jnp.logical_and(pos <= row_q, pos < dec_seq)
        attend([kto_ref[0]], [vto_ref[0]], [-prefix_t], valid)

    @pl.when(step > 0)
    def _cache_pages():
        after = (r > c).astype(BF16)
        carry = carry_ref[...]
        kts, vts, biases = [], [], []
        for u in range(PAGES_PER_STEP):
            kt_ref, vt_ref, lf_ref = cache_refs[3 * u:3 * u + 3]
            lf_t = lf_ref[0]
            kts.append(kt_ref[0])
            vts.append(vt_ref[0])
            biases.append(lane_sums(lf_t, after) + carry)
            carry = carry + jnp.sum(lf_t, axis=1, keepdims=True)
        attend(kts, vts, biases, None)
        carry_ref[...] = carry

    @pl.when(step == n_steps - 1)
    def _finish():
        col_head = lax.broadcasted_iota(I32, (dec_seq, D_MODEL), 1) // FOX_DH
        out = jnp.zeros((dec_seq, D_MODEL), F32)
        for h in range(FOX_HEADS):
            rows = slice(h * dec_seq, (h + 1) * dec_seq)
            out = out + jnp.where(col_head == h, acc_ref[rows, :] / l_ref[rows, :], 0.0)
        o_ref[0] = out


def _fox_sample(q, lft_own, kt_own, vt_own, kt_cache, vt_cache, lft_cache, page_table):
    db, dec_seq, _ = q.shape
    n_pages = page_table.shape[1]
    n_steps = 1 + n_pages // PAGES_PER_STEP
    fd = FOX_HEADS * FOX_DH
    kern = functools.partial(_fox_sample_kernel, n_steps=n_steps, dec_seq=dec_seq)

    def page(u):
        return lambda b, s, pt: (pt[b, n_pages - 1 - (jnp.maximum(s, 1) - 1) * PAGES_PER_STEP - u], 0, 0)

    cache_specs = []
    cache_args = []
    for u in range(PAGES_PER_STEP):
        cache_specs += [pl.BlockSpec((1, fd, PAGE_SIZE), page(u)),
                        pl.BlockSpec((1, fd, PAGE_SIZE), page(u)),
                        pl.BlockSpec((1, FOX_HEADS, PAGE_SIZE), page(u))]
        cache_args += [kt_cache, vt_cache, lft_cache]
    grid_spec = pltpu.PrefetchScalarGridSpec(
        num_scalar_prefetch=1,
        grid=(db, n_steps),
        in_specs=[pl.BlockSpec((1, dec_seq, D_MODEL), lambda b, s, pt: (b, 0, 0)),
                  pl.BlockSpec((1, FOX_HEADS, PAGE_SIZE), lambda b, s, pt: (b, 0, 0)),
                  pl.BlockSpec((1, fd, PAGE_SIZE), lambda b, s, pt: (b, 0, 0)),
                  pl.BlockSpec((1, fd, PAGE_SIZE), lambda b, s, pt: (b, 0, 0))] + cache_specs,
        out_specs=pl.BlockSpec((1, dec_seq, D_MODEL), lambda b, s, pt: (b, 0, 0)),
        scratch_shapes=[pltpu.VMEM((N_QROWS, fd), BF16),
                        pltpu.VMEM((N_QROWS, LANES), BF16),
                        pltpu.VMEM((N_QROWS, 1), F32),
                        pltpu.VMEM((N_QROWS, 1), F32),
                        pltpu.VMEM((N_QROWS, fd), F32),
                        pltpu.VMEM((FOX_HEADS, 1), F32)],
    )
    return pl.pallas_call(
        kern,
        grid_spec=grid_spec,
        out_shape=jax.ShapeDtypeStruct((db, dec_seq, D_MODEL), F32),
        compiler_params=_params(("parallel", "arbitrary")),
        name="fox_sample",
    )(page_table, q, lft_own, kt_own, vt_own, *cache_args)


def _moe_params(layer, w_router, b_router, w_gate, w_up, w_down, ws_gate, ws_up, ws_down):
    wr_t = w_router[layer].T
    wr_hi = wr_t.astype(BF16)
    wr_lo = (wr_t - wr_hi.astype(F32)).astype(BF16)
    return dict(
        wr_hi=wr_hi, wr_lo=wr_lo, b_col=b_router[layer].reshape(N_EXPERTS, 1),
        layer=layer, wg=w_gate, wu=w_up, wd=w_down,
        wsgu=jnp.concatenate([ws_gate[layer], ws_up[layer]], axis=1).astype(BF16),
        wsd=ws_down[layer].astype(BF16))


def kernel(x_prompt, x_sample, state_ret, cache_k, cache_v, cache_logf, page_table, ret_w_in, ret_w_out,
           fox_w_kvf, fox_b_f, fox_w_q, fox_w_out, ln_g, ln_b, moe_w_router, moe_b_router, moe_w_gate,
           moe_w_up, moe_w_down, moe_ws_gate, moe_ws_up, moe_ws_down):
    bp, lp, d = x_prompt.shape
    bs, ls, _ = x_sample.shape
    tp, ts = bp * lp, bs * ls
    assert fox_heads_rows(ls) == N_QROWS
    xp = x_prompt.reshape(tp, d)
    xs = x_sample.reshape(ts, d)
    hq = RET_HEADS * RET_DK
    hv = RET_HEADS * RET_DV
    fd = FOX_HEADS * FOX_DH
    tm_p, tn_p = PROMPT_TILE_ROWS, PROJ_TILE_COLS
    tc_p = MOE_TOKEN_TILE

    wg_all, wu_all, wd_all = moe_w_gate.astype(BF16), moe_w_up.astype(BF16), moe_w_down.astype(BF16)
    moe = [_moe_params(layer, moe_w_router, moe_b_router, wg_all, wu_all, wd_all,
                       moe_ws_gate, moe_ws_up, moe_ws_down) for layer in range(DEPTH)]

    w_in = ret_w_in[0].astype(BF16)
    w_out = ret_w_out[0].astype(BF16)
    cos_p, sin_p = _rope_tables(jnp.arange(lp))
    cos_s, sin_s = _rope_tables(PAST_LEN + jnp.arange(ls))
    cos_s, sin_s = jnp.tile(cos_s, (bs, 1)), jnp.tile(sin_s, (bs, 1))

    qk_p = _qk_rope(xp, w_in, cos_p, sin_p, tm_p)
    v_p = _matmul(xp, w_in, 2 * hq, hv, BF16, tm_p, tn_p)
    g_p = _matmul(xp, w_in, 2 * hq + hv, hv, F32, tm_p, tn_p)
    o_p, st_p = _retention(qk_p, v_p, g_p, jnp.zeros((bp, RET_HEADS, RET_DK, RET_DV), F32),
                           _retention_decays(min(RET_CHUNK, lp)), bp, lp, BF16)
    xp = _mm_res_ln(o_p, w_out, xp, ln_g[0, 0], ln_b[0, 0], tm_p)

    qk_s = _qk_rope(xs, w_in, cos_s, sin_s, ts)
    v_s = _matmul(xs, w_in, 2 * hq, hv, F32, ts, tn_p)
    g_s = _matmul(xs, w_in, 2 * hq + hv, hv, F32, ts, tn_p)
    o_s, st_s = _retention(qk_s, v_s, g_s, state_ret[0], _retention_decays(min(RET_CHUNK, ls)), bs, ls, F32)
    xs = _mm_res_ln(o_s, w_out, xs, ln_g[0, 0], ln_b[0, 0], ts)

    xp = _moe_ln(xp, moe[0], ln_g[0, 1], ln_b[0, 1], tc_p)
    xs = _moe_ln(xs, moe[0], ln_g[0, 1], ln_b[0, 1], ts)

    w_kvf = fox_w_kvf.astype(BF16)
    w_kv_t = w_kvf[:, :2 * fd].T
    w_f_pad = jnp.zeros((d, LANES), BF16).at[:, :FOX_HEADS].set(w_kvf[:, 2 * fd:])
    b_f_pad = jnp.zeros((1, LANES), F32).at[0, :FOX_HEADS].set(fox_b_f)
    kt_p, vt_p = _matmul_t_pair(xp, w_kv_t, bp, tm_p)
    lf_p = _logf_proj(xp, w_f_pad, b_f_pad, tm_p)
    per_seq = lambda a_t: a_t[0].reshape(fd, bs, ls).transpose(1, 0, 2)
    kt_s, vt_s = (per_seq(a) for a in _matmul_t_pair(xs, w_kv_t, 1, ts))
    lf_s = _logf_proj(xs, w_f_pad, b_f_pad, ts)

    as_blhd = lambda a_t, b, l: a_t.reshape(b, FOX_HEADS, FOX_DH, l).transpose(0, 3, 1, 2)
    k_p, v_p4 = as_blhd(kt_p, bp, lp), as_blhd(vt_p, bp, lp)
    k_s, v_s4 = as_blhd(kt_s, bs, ls), as_blhd(vt_s, bs, ls)

    w_q = fox_w_q[0].astype(BF16)
    w_o = fox_w_out[0].astype(BF16)
    q_p = _matmul(xp, w_q, 0, fd, BF16, tm_p, tn_p)
    cum_p = _cumsum_rows(lf_p, bp, lp)
    cum_row = cum_p[:, :FOX_HEADS].reshape(bp, lp, FOX_HEADS).transpose(0, 2, 1).reshape(
        bp, FOX_HEADS // HEADS_PER_QBLOCK, HEADS_PER_QBLOCK, lp)
    att_p = _fox_prompt(q_p, kt_p, vt_p, cum_p, cum_row, bp, ATTN_TILE)
    xp = _mm_res_ln(att_p, w_o, xp, ln_g[1, 0], ln_b[1, 0], tm_p)

    q_s = _matmul(xs, w_q, 0, fd, F32, ts, tn_p)
    n_pool = cache_k.shape[0]
    kt_cache = cache_k.transpose(0, 2, 3, 1).reshape(n_pool, fd, PAGE_SIZE)
    vt_cache = cache_v.transpose(0, 2, 3, 1).reshape(n_pool, fd, PAGE_SIZE)
    lft_cache = cache_logf.transpose(0, 2, 1)
    pad_pos = lambda a_t: jnp.pad(a_t, ((0, 0), (0, 0), (0, PAGE_SIZE - ls)))
    lft_own = pad_pos(lf_s[:, :FOX_HEADS].reshape(bs, ls, FOX_HEADS).transpose(0, 2, 1))
    att_s = _fox_sample(q_s.reshape(bs, ls, d), lft_own, pad_pos(kt_s), pad_pos(vt_s),
                        kt_cache, vt_cache, lft_cache, page_table)
    xs = _mm_res_ln(att_s.reshape(ts, fd), w_o, xs, ln_g[1, 0], ln_b[1, 0], ts)

    xp = _moe_ln(xp, moe[1], ln_g[1, 1], ln_b[1, 1], tc_p)
    xs = _moe_ln(xs, moe[1], ln_g[1, 1], ln_b[1, 1], ts)

    return (xp.reshape(bp, lp, d), xs.reshape(bs, ls, d), st_p[None], st_s[None],
            k_p, v_p4, lf_p[:, :FOX_HEADS].reshape(bp, lp, FOX_HEADS),
            k_s, v_s4, lf_s[:, :FOX_HEADS].reshape(bs, ls, FOX_HEADS))


def fox_heads_rows(dec_seq):
    return FOX_HEADS * dec_seq
```

```python
import functools

import jax
import jax.numpy as jnp
from jax import lax
from jax.experimental import pallas as pl
from jax.experimental.pallas import tpu as pltpu

F32 = jnp.float32
BF16 = jnp.bfloat16
I32 = jnp.int32

D_MODEL = 1024
DEPTH = 2
PAST_LEN = 8192
PAGE_SIZE = 128
N_A_LAYERS = DEPTH // 2
RET_HEADS = 4
RET_DK = D_MODEL // RET_HEADS
RET_DV = 2 * RET_DK
RET_CHUNK = 128
ROPE_BASE = 10000.0
FOX_HEADS = 16
FOX_DH = D_MODEL // FOX_HEADS
NEG_INF = -1e30
N_EXPERTS = 64
TOPK = 8
N_GROUPS = 8
GROUP_SIZE = N_EXPERTS // N_GROUPS
TOPK_GROUPS = 4
FF_EXPERT = D_MODEL // 4
ROUTED_SCALE = 2.5
DEEPNORM_ALPHA = (2.0 * DEPTH) ** 0.25
LN_EPS = 1e-5
GN_EPS = 1e-5

LANES = 128
SUBLANES = 8
ROW_VREGS = D_MODEL // LANES
MXU_COLS = 256
EXP_BLOCK = 160
EXPERTS_PER_STEP = 4
VMEM_LIMIT_BIG = 60 * 1024 * 1024
VMEM_LIMIT = 48 * 1024 * 1024

PROMPT_TILE_ROWS = 1024
PROJ_TILE_COLS = 1024
MOE_TOKEN_TILE = 1024
ATTN_TILE = 512

_NT = (((1,), (1,)), ((), ()))
_TN = (((0,), (0,)), ((), ()))


def _params(sem, vmem=VMEM_LIMIT):
    return pltpu.CompilerParams(dimension_semantics=sem, vmem_limit_bytes=vmem)


def _bdot(a, b):
    return jnp.dot(a, b, preferred_element_type=F32)


def _layer_norm_rows(z, g, b):
    mu = jnp.mean(z, axis=-1, keepdims=True)
    zc = z - mu
    var = jnp.mean(zc * zc, axis=-1, keepdims=True)
    return zc * lax.rsqrt(var + LN_EPS) * g + b


def _silu(x):
    return x * jax.nn.sigmoid(x)


def _split3(x):
    hi = x.astype(BF16)
    r1 = x - hi.astype(F32)
    mid = r1.astype(BF16)
    lo = (r1 - mid.astype(F32)).astype(BF16)
    return hi, mid, lo


def _mm_kernel(x_ref, w_ref, o_ref):
    o_ref[...] = _bdot(x_ref[...].astype(BF16), w_ref[...]).astype(o_ref.dtype)


def _matmul(x, w, col0, n, out_dtype, tm, tn):
    m, k = x.shape
    j0 = col0 // tn
    return pl.pallas_call(
        _mm_kernel,
        grid=(m // tm, n // tn),
        in_specs=[pl.BlockSpec((tm, k), lambda i, j: (i, 0)),
                  pl.BlockSpec((k, tn), lambda i, j: (0, j + j0))],
        out_specs=pl.BlockSpec((tm, tn), lambda i, j: (i, j)),
        out_shape=jax.ShapeDtypeStruct((m, n), out_dtype),
        compiler_params=_params(("parallel", "arbitrary")),
        name="proj",
    )(x, w)


def _mm_t_pair_kernel(x_ref, wt_ref, o0_ref, o1_ref):
    j = pl.program_id(1)
    res = lax.dot_general(wt_ref[...], x_ref[...].astype(BF16), _NT, preferred_element_type=F32)

    @pl.when(j == 0)
    def _():
        o0_ref[...] = res

    @pl.when(j == 1)
    def _():
        o1_ref[...] = res


def _matmul_t_pair(x, w_t, batch, tm):
    m, k = x.shape
    n = w_t.shape[0] // 2
    seq = m // batch
    nper = seq // tm
    out_spec = pl.BlockSpec((None, n, tm), lambda i, j: (i // nper, 0, i % nper))
    return pl.pallas_call(
        _mm_t_pair_kernel,
        grid=(m // tm, 2),
        in_specs=[pl.BlockSpec((tm, k), lambda i, j: (i, 0)),
                  pl.BlockSpec((n, k), lambda i, j: (j, 0))],
        out_specs=[out_spec, out_spec],
        out_shape=[jax.ShapeDtypeStruct((batch, n, seq), F32)] * 2,
        compiler_params=_params(("parallel", "arbitrary")),
        name="proj_t",
    )(x, w_t)


def _mm_rope_kernel(x_ref, w_ref, cos_ref, sin_ref, o_ref, *, k_scale):
    j = pl.program_id(1)
    acc = _bdot(x_ref[...].astype(BF16), w_ref[...])
    half = RET_DK // 2
    c = cos_ref[...]
    s = sin_ref[...]
    sc = jnp.where(j == 0, 1.0, k_scale).astype(F32)
    for h in range(RET_HEADS):
        lo = h * RET_DK
        x1 = acc[:, lo:lo + half]
        x2 = acc[:, lo + half:lo + RET_DK]
        o_ref[:, lo:lo + half] = (x1 * c - x2 * s) * sc
        o_ref[:, lo + half:lo + RET_DK] = (x2 * c + x1 * s) * sc


def _qk_rope(x, w, cos_rows, sin_rows, tm):
    m, k = x.shape
    hq = RET_HEADS * RET_DK
    nper = cos_rows.shape[0] // tm
    kern = functools.partial(_mm_rope_kernel, k_scale=RET_DK ** -0.5)
    return pl.pallas_call(
        kern,
        grid=(m // tm, 2),
        in_specs=[pl.BlockSpec((tm, k), lambda i, j: (i, 0)),
                  pl.BlockSpec((k, hq), lambda i, j: (0, j)),
                  pl.BlockSpec((tm, RET_DK // 2), lambda i, j: (i % nper, 0)),
                  pl.BlockSpec((tm, RET_DK // 2), lambda i, j: (i % nper, 0))],
        out_specs=pl.BlockSpec((tm, hq), lambda i, j: (i, j)),
        out_shape=jax.ShapeDtypeStruct((m, 2 * hq), F32),
        compiler_params=_params(("parallel", "arbitrary")),
        name="qk_rope",
    )(x, w, cos_rows, sin_rows)


def _mm_res_ln_kernel(a_ref, w_ref, x_ref, g_ref, b_ref, o_ref):
    mix = _bdot(a_ref[...].astype(BF16), w_ref[...])
    z = DEEPNORM_ALPHA * x_ref[...] + mix
    o_ref[...] = _layer_norm_rows(z, g_ref[...], b_ref[...])


def _mm_res_ln(a, w, x, g, b, tm):
    m, k = a.shape
    d = w.shape[1]
    return pl.pallas_call(
        _mm_res_ln_kernel,
        grid=(m // tm,),
        in_specs=[pl.BlockSpec((tm, k), lambda i: (i, 0)),
                  pl.BlockSpec((k, d), lambda i: (0, 0)),
                  pl.BlockSpec((tm, d), lambda i: (i, 0)),
                  pl.BlockSpec((1, d), lambda i: (0, 0)),
                  pl.BlockSpec((1, d), lambda i: (0, 0))],
        out_specs=pl.BlockSpec((tm, d), lambda i: (i, 0)),
        out_shape=jax.ShapeDtypeStruct((m, d), F32),
        compiler_params=_params(("parallel",)),
        name="mix_out_ln",
    )(a, w, x, g.reshape(1, d), b.reshape(1, d))


def _mm_logsig_kernel(x_ref, w_ref, b_ref, o_ref):
    z = _bdot(x_ref[...].astype(BF16), w_ref[...]) + b_ref[...]
    o_ref[...] = -(jnp.maximum(-z, 0.0) + jnp.log1p(jnp.exp(-jnp.abs(z))))


def _logf_proj(x, w_pad, b_pad, tm):
    m, k = x.shape
    return pl.pallas_call(
        _mm_logsig_kernel,
        grid=(m // tm,),
        in_specs=[pl.BlockSpec((tm, k), lambda i: (i, 0)),
                  pl.BlockSpec((k, LANES), lambda i: (0, 0)),
                  pl.BlockSpec((1, LANES), lambda i: (0, 0))],
        out_specs=pl.BlockSpec((tm, LANES), lambda i: (i, 0)),
        out_shape=jax.ShapeDtypeStruct((m, LANES), F32),
        compiler_params=_params(("parallel",)),
        name="logf_proj",
    )(x, w_pad, b_pad)


def _ret_kernel(q_ref, k_ref, v_ref, g_ref, s0_ref, din_ref, dq_ref, dk_ref, dc_ref,
                o_ref, sout_ref, state_ref):
    c = pl.program_id(1)

    @pl.when(c == 0)
    def _():
        state_ref[...] = s0_ref[...]

    for h in range(RET_HEADS):
        qcols = slice(h * RET_DK, (h + 1) * RET_DK)
        vcols = slice(h * RET_DV, (h + 1) * RET_DV)
        q = q_ref[:, qcols].astype(BF16)
        k = k_ref[:, qcols]
        v = v_ref[:, vcols].astype(BF16)
        s_prev = state_ref[h]
        scores = lax.dot_general(q, k.astype(BF16), _NT, preferred_element_type=F32) * din_ref[h]
        o = _bdot(scores.astype(BF16), v) + _bdot(q, s_prev.astype(BF16)) * dq_ref[h]
        kd = (k * dk_ref[h]).astype(BF16)
        s_new = s_prev * dc_ref[h] + lax.dot_general(kd, v, _TN, preferred_element_type=F32)
        state_ref[h] = s_new

        mu = jnp.mean(o, axis=-1, keepdims=True)
        oc = o - mu
        var = jnp.mean(oc * oc, axis=-1, keepdims=True)
        on = oc * lax.rsqrt(var + GN_EPS)
        o_ref[:, vcols] = (_silu(g_ref[:, vcols]) * on).astype(o_ref.dtype)

    @pl.when(c == pl.num_programs(1) - 1)
    def _():
        sout_ref[...] = state_ref[...]


def _retention(qk, v, g, s0, decays, batch, seq, out_dtype):
    chunk = min(RET_CHUNK, seq)
    nc = seq // chunk
    din, dq, dk, dc = decays
    h_ = RET_HEADS
    hq, hv = h_ * RET_DK, h_ * RET_DV
    whole = lambda a: pl.BlockSpec(a.shape, lambda b, c: (0,) * a.ndim)
    return pl.pallas_call(
        _ret_kernel,
        grid=(batch, nc),
        in_specs=[pl.BlockSpec((chunk, hq), lambda b, c: (b * nc + c, 0)),
                  pl.BlockSpec((chunk, hq), lambda b, c: (b * nc + c, 1)),
                  pl.BlockSpec((chunk, hv), lambda b, c: (b * nc + c, 0)),
                  pl.BlockSpec((chunk, hv), lambda b, c: (b * nc + c, 0)),
                  pl.BlockSpec((None, h_, RET_DK, RET_DV), lambda b, c: (b, 0, 0, 0)),
                  whole(din), whole(dq), whole(dk), whole(dc)],
        out_specs=[pl.BlockSpec((chunk, hv), lambda b, c: (b * nc + c, 0)),
                   pl.BlockSpec((None, h_, RET_DK, RET_DV), lambda b, c: (b, 0, 0, 0))],
        out_shape=[jax.ShapeDtypeStruct((batch * seq, hv), out_dtype),
                   jax.ShapeDtypeStruct((batch, h_, RET_DK, RET_DV), F32)],
        scratch_shapes=[pltpu.VMEM((h_, RET_DK, RET_DV), F32)],
        compiler_params=_params(("parallel", "arbitrary")),
        name="retention",
    )(qk, qk, v, g, s0, din, dq, dk, dc)


def _retention_decays(chunk):
    h = jnp.arange(RET_HEADS, dtype=F32)
    log_g = jnp.log1p(-(2.0 ** (-5.0 - h)))
    idx = jnp.arange(chunk, dtype=F32)
    rel = idx[:, None] - idx[None, :]
    din = jnp.where(rel >= 0, jnp.exp(log_g[:, None, None] * jnp.maximum(rel, 0.0)), 0.0)
    dq = jnp.exp(log_g[:, None] * (idx[None, :] + 1.0))[:, :, None]
    dk = jnp.exp(log_g[:, None] * (chunk - 1.0 - idx[None, :]))[:, :, None]
    dc = jnp.exp(log_g * chunk)[:, None, None]
    return din, dq, dk, dc


def _rope_tables(pos):
    half = RET_DK // 2
    inv_freq = ROPE_BASE ** (-jnp.arange(half, dtype=F32) / half)
    ang = pos.astype(F32)[:, None] * inv_freq[None, :]
    return jnp.cos(ang), jnp.sin(ang)


def _router_kernel(x_ref, wh_ref, wl_ref, b_ref, slot_ref, wgt_ref, off_ref, cnt_ref):
    tc = x_ref.shape[0]
    x = x_ref[...]
    xh = x.astype(BF16)
    xl = (x - xh.astype(F32)).astype(BF16)
    wh = wh_ref[...]
    wl = wl_ref[...]
    logits = (lax.dot_general(wh, xh, _NT, preferred_element_type=F32)
              + lax.dot_general(wh, xl, _NT, preferred_element_type=F32)
              + lax.dot_general(wl, xh, _NT, preferred_element_type=F32))
    scores = jax.nn.sigmoid(logits)
    biased = scores + b_ref[...]

    jj = lax.broadcasted_iota(I32, (GROUP_SIZE, tc), 0)
    groups = [biased[g * GROUP_SIZE:(g + 1) * GROUP_SIZE, :] for g in range(N_GROUPS)]
    gscore = []
    for rows in groups:
        m1 = jnp.max(rows, axis=0, keepdims=True)
        j1 = jnp.min(jnp.where(rows == m1, jj, GROUP_SIZE), axis=0, keepdims=True)
        m2 = jnp.max(jnp.where(jj == j1, -jnp.inf, rows), axis=0, keepdims=True)
        gscore.append(m1 + m2)
    gsel = [jnp.zeros((1, tc), dtype=jnp.bool_) for _ in range(N_GROUPS)]
    for _ in range(TOPK_GROUPS):
        gm = gscore[0]
        for sc in gscore[1:]:
            gm = jnp.maximum(gm, sc)
        found = jnp.zeros((1, tc), dtype=jnp.bool_)
        for g in range(N_GROUPS):
            hit = jnp.logical_and(gscore[g] == gm, jnp.logical_not(found))
            found = jnp.logical_or(found, hit)
            gsel[g] = jnp.logical_or(gsel[g], hit)
            gscore[g] = jnp.where(hit, -jnp.inf, gscore[g])
    masked = jnp.concatenate(
        [jnp.where(jnp.broadcast_to(gsel[g], groups[g].shape), groups[g], -jnp.inf)
         for g in range(N_GROUPS)], axis=0)
    ei = lax.broadcasted_iota(I32, masked.shape, 0)
    hits = []
    wk = []
    for _ in range(TOPK):
        m = jnp.max(masked, axis=0, keepdims=True)
        first = jnp.min(jnp.where(masked == m, ei, N_EXPERTS), axis=0, keepdims=True)
        hit = ei == first
        hits.append(hit)
        wk.append(jnp.sum(jnp.where(hit, scores, 0.0), axis=0, keepdims=True))
        masked = jnp.where(hit, -jnp.inf, masked)
    wsum = wk[0]
    for w in wk[1:]:
        wsum = wsum + w
    wgt_ref[...] = jnp.concatenate([w / wsum * ROUTED_SCALE for w in wk], axis=0)

    chosen = hits[0]
    for hit in hits[1:]:
        chosen = jnp.logical_or(chosen, hit)
    chosen = chosen.astype(BF16)
    r = lax.broadcasted_iota(I32, (tc, tc), 0)
    c = lax.broadcasted_iota(I32, (tc, tc), 1)
    rank = _bdot(chosen, (r <= c).astype(BF16))
    er = lax.broadcasted_iota(I32, (N_EXPERTS, N_EXPERTS), 0)
    ec = lax.broadcasted_iota(I32, (N_EXPERTS, N_EXPERTS), 1)
    below = _bdot((ec < er).astype(BF16), chosen)
    off = jnp.sum(below, axis=1, keepdims=True)
    cnt = rank[:, tc - 1:tc]
    pos = (off + rank - 1.0) * ROW_VREGS
    slot_ref[...] = jnp.concatenate(
        [jnp.sum(jnp.where(hit, pos, 0.0), axis=0, keepdims=True) for hit in hits], axis=0).astype(I32)
    off_ref[...] = jnp.broadcast_to(off, off_ref.shape).astype(I32)
    cnt_ref[...] = jnp.broadcast_to(cnt, cnt_ref.shape).astype(I32)


def _router(x, wr_hi, wr_lo, b_col, tc):
    t, d = x.shape
    n = t // tc
    return pl.pallas_call(
        _router_kernel,
        grid=(n,),
        in_specs=[pl.BlockSpec((tc, d), lambda i: (i, 0)),
                  pl.BlockSpec((N_EXPERTS, d), lambda i: (0, 0)),
                  pl.BlockSpec((N_EXPERTS, d), lambda i: (0, 0)),
                  pl.BlockSpec((N_EXPERTS, 1), lambda i: (0, 0))],
        out_specs=[pl.BlockSpec((None, TOPK, tc), lambda i: (i, 0, 0)),
                   pl.BlockSpec((None, TOPK, tc), lambda i: (i, 0, 0)),
                   pl.BlockSpec((None, N_EXPERTS, LANES), lambda i: (i, 0, 0)),
                   pl.BlockSpec((None, N_EXPERTS, LANES), lambda i: (i, 0, 0))],
        out_shape=[jax.ShapeDtypeStruct((n, TOPK, tc), I32),
                   jax.ShapeDtypeStruct((n, TOPK, tc), F32),
                   jax.ShapeDtypeStruct((n, N_EXPERTS, LANES), I32),
                   jax.ShapeDtypeStruct((n, N_EXPERTS, LANES), I32)],
        compiler_params=_params(("parallel",)),
        name="router",
    )(x, wr_hi, wr_lo, b_col)


def _experts_kernel(off_ref, cnt_ref, slot_ref, wgt_ref, x_ref, wg_ref, wu_ref, wd_ref,
                    o_ref, sorted_ref):
    tc = x_ref.shape[0]
    ci = pl.program_id(0)
    e = pl.program_id(1)
    tok_per_line = LANES // TOPK

    @pl.when(e == 0)
    def _dispatch():
        for kc in range(ROW_VREGS):
            o_ref[pl.ds(kc, tc, stride=ROW_VREGS), :] = x_ref[:, kc * LANES:(kc + 1) * LANES]
        sorted_ref[pl.ds(TOPK * tc * ROW_VREGS, EXP_BLOCK * ROW_VREGS), :] = jnp.zeros(
            (EXP_BLOCK * ROW_VREGS, LANES), F32)

        def body(i, carry):
            slot_line = slot_ref.at[0, i]
            for u in range(tok_per_line):
                t = i * tok_per_line + u
                row = o_ref[pl.ds(pl.multiple_of(t * ROW_VREGS, ROW_VREGS), ROW_VREGS), :]
                for k in range(TOPK):
                    s = slot_line[u * TOPK + k]
                    sorted_ref[pl.ds(pl.multiple_of(s, ROW_VREGS), ROW_VREGS), :] = row
            return carry

        lax.fori_loop(0, tc // tok_per_line, body, 0)

    base = ci * N_EXPERTS + e * EXPERTS_PER_STEP
    offs = [off_ref[base + j] for j in range(EXPERTS_PER_STEP)]
    cnts = [cnt_ref[base + j] for j in range(EXPERTS_PER_STEP)]
    nblks = [lax.div(cnt + (EXP_BLOCK - 1), EXP_BLOCK) for cnt in cnts]
    nmax = nblks[0]
    for nb in nblks[1:]:
        nmax = jnp.maximum(nmax, nb)
    rowid = lax.broadcasted_iota(I32, (EXP_BLOCK, 1), 0)

    def ffn_blocks(i, carry):
        loaded = []
        for j in range(EXPERTS_PER_STEP):
            active = i < nblks[j]
            row0 = jnp.where(active, offs[j] + i * EXP_BLOCK, TOPK * tc)
            nvalid = jnp.where(active, cnts[j] - i * EXP_BLOCK, 0)
            r0 = row0 * ROW_VREGS
            xbb = jnp.concatenate(
                [sorted_ref[pl.ds(r0 + kc, EXP_BLOCK, stride=ROW_VREGS), :].astype(BF16)
                 for kc in range(ROW_VREGS)], axis=1)
            loaded.append((r0, nvalid, xbb))
        hidden = [(_silu(_bdot(xbb, wg_ref[j])) * _bdot(xbb, wu_ref[j])).astype(BF16)
                  for j, (_, _, xbb) in enumerate(loaded)]
        for j, (r0, nvalid, xbb) in enumerate(loaded):
            for c0 in range(0, D_MODEL, MXU_COLS):
                y = _bdot(hidden[j], wd_ref[j, :, c0:c0 + MXU_COLS])
                y = jnp.where(rowid < nvalid, y, xbb[:, c0:c0 + MXU_COLS].astype(F32))
                for kc in range(c0 // LANES, (c0 + MXU_COLS) // LANES):
                    sorted_ref[pl.ds(r0 + kc, EXP_BLOCK, stride=ROW_VREGS), :] = (
                        y[:, kc * LANES - c0:(kc + 1) * LANES - c0])
        return carry

    lax.fori_loop(0, nmax, ffn_blocks, 0)

    @pl.when(e == pl.num_programs(1) - 1)
    def _combine():
        def body(i, carry):
            slot_line = slot_ref.at[0, i]
            wgt_line = wgt_ref.at[0, i]
            for u in range(tok_per_line):
                t = i * tok_per_line + u
                acc = None
                for k in range(TOPK):
                    s = slot_line[u * TOPK + k]
                    term = wgt_line[u * TOPK + k] * sorted_ref[
                        pl.ds(pl.multiple_of(s, ROW_VREGS), ROW_VREGS), :]
                    acc = term if acc is None else acc + term
                o_ref[pl.ds(pl.multiple_of(t * ROW_VREGS, ROW_VREGS), ROW_VREGS), :] = acc
            return carry

        lax.fori_loop(0, tc // tok_per_line, body, 0)


def _routed_experts(x, slot, wgt, seg_off, seg_cnt, wg, wu, wd, layer, tc):
    t, d = x.shape
    n = t // tc
    sorted_rows = TOPK * tc + EXP_BLOCK
    grid_spec = pltpu.PrefetchScalarGridSpec(
        num_scalar_prefetch=2,
        grid=(n, N_EXPERTS // EXPERTS_PER_STEP),
        in_specs=[pl.BlockSpec((1,) + slot.shape[1:], lambda c, e, *_: (c, 0, 0), memory_space=pltpu.SMEM),
                  pl.BlockSpec((1,) + wgt.shape[1:], lambda c, e, *_: (c, 0, 0), memory_space=pltpu.SMEM),
                  pl.BlockSpec((tc, d), lambda c, e, *_: (c, 0), pipeline_mode=pl.Buffered(1)),
                  pl.BlockSpec((None, EXPERTS_PER_STEP, d, FF_EXPERT), lambda c, e, *_: (layer, e, 0, 0)),
                  pl.BlockSpec((None, EXPERTS_PER_STEP, d, FF_EXPERT), lambda c, e, *_: (layer, e, 0, 0)),
                  pl.BlockSpec((None, EXPERTS_PER_STEP, FF_EXPERT, d), lambda c, e, *_: (layer, e, 0, 0))],
        out_specs=pl.BlockSpec((tc * ROW_VREGS, LANES), lambda c, e, *_: (c, 0), pipeline_mode=pl.Buffered(1)),
        scratch_shapes=[pltpu.VMEM((sorted_rows * ROW_VREGS, LANES), F32)],
    )
    return pl.pallas_call(
        _experts_kernel,
        grid_spec=grid_spec,
        out_shape=jax.ShapeDtypeStruct((t * ROW_VREGS, LANES), F32),
        compiler_params=_params(("arbitrary", "arbitrary"), VMEM_LIMIT_BIG),
        name="routed_experts",
    )(seg_off, seg_cnt, slot, wgt, x, wg, wu, wd)


def _shared_ln_kernel(x_ref, r_ref, wgu_ref, wd_ref, g_ref, b_ref, o_ref):
    tm = x_ref.shape[0]
    x = x_ref[...]
    gu = _bdot(x.astype(BF16), wgu_ref[...])
    ff = wgu_ref.shape[1] // 2
    h = _silu(gu[:, :ff]) * gu[:, ff:]
    shared = _bdot(h.astype(BF16), wd_ref[...])
    routed = jnp.concatenate(
        [r_ref[pl.ds(kc, tm, stride=ROW_VREGS), :] for kc in range(ROW_VREGS)], axis=1)
    z = DEEPNORM_ALPHA * x + (shared + routed)
    o_ref[...] = _layer_norm_rows(z, g_ref[...], b_ref[...])


def _shared_ln(x, routed_rows, wgu, wd, g, b, tm):
    t, d = x.shape
    return pl.pallas_call(
        _shared_ln_kernel,
        grid=(t // tm,),
        in_specs=[pl.BlockSpec((tm, d), lambda i: (i, 0)),
                  pl.BlockSpec((tm * ROW_VREGS, LANES), lambda i: (i, 0)),
                  pl.BlockSpec(wgu.shape, lambda i: (0, 0)),
                  pl.BlockSpec(wd.shape, lambda i: (0, 0)),
                  pl.BlockSpec((1, d), lambda i: (0, 0)),
                  pl.BlockSpec((1, d), lambda i: (0, 0))],
        out_specs=pl.BlockSpec((tm, d), lambda i: (i, 0)),
        out_shape=jax.ShapeDtypeStruct((t, d), F32),
        compiler_params=_params(("parallel",)),
        name="shared_ln",
    )(x, routed_rows, wgu, wd, g.reshape(1, d), b.reshape(1, d))


def _moe_ln(x, p, g, b, tc):
    slot, wgt, off, cnt = _router(x, p["wr_hi"], p["wr_lo"], p["b_col"], tc)
    lines = lambda a: a.transpose(0, 2, 1).reshape(a.shape[0], tc * TOPK // LANES, LANES)
    routed = _routed_experts(x, lines(slot), lines(wgt), off[:, :, 0].reshape(-1), cnt[:, :, 0].reshape(-1),
                             p["wg"], p["wu"], p["wd"], p["layer"], tc)
    return _shared_ln(x, routed, p["wsgu"], p["wsd"], g, b, min(tc, PROMPT_TILE_ROWS))


def _cumsum_kernel(x_ref, o_ref):
    n = x_ref.shape[0]
    blk = LANES
    r = lax.broadcasted_iota(I32, (blk, blk), 0)
    c = lax.broadcasted_iota(I32, (blk, blk), 1)
    tri = (c <= r).astype(BF16)

    def body(i, carry):
        start = pl.multiple_of(i * blk, blk)
        hi, mid, lo = _split3(x_ref[pl.ds(start, blk), :])
        within = (_bdot(tri, lo) + _bdot(tri, mid)) + _bdot(tri, hi)
        o_ref[pl.ds(start, blk), :] = within + carry
        return carry + jnp.sum(x_ref[pl.ds(start, blk), :], axis=0, keepdims=True)

    lax.fori_loop(0, n // blk, body, jnp.zeros((1, x_ref.shape[1]), F32))


def _cumsum_rows(x, batch, seq):
    return pl.pallas_call(
        _cumsum_kernel,
        grid=(batch,),
        in_specs=[pl.BlockSpec((seq, x.shape[1]), lambda b: (b, 0))],
        out_specs=pl.BlockSpec((seq, x.shape[1]), lambda b: (b, 0)),
        out_shape=jax.ShapeDtypeStruct(x.shape, F32),
        compiler_params=_params(("parallel",)),
        name="logf_cumsum",
    )(x)


HEADS_PER_QBLOCK = LANES // FOX_DH
SUM_ROWS = 16


def _fox_prompt_kernel(q_ref, kt_ref, vt_ref, cumq_ref, cumk_ref, o_ref, *, tq):
    hp = pl.program_id(1)
    qi = pl.program_id(2)
    qf = q_ref[...].astype(F32) * (FOX_DH ** -0.5)
    lane = lax.broadcasted_iota(I32, cumq_ref.shape, 1)
    ones_rows = (lax.broadcasted_iota(I32, (SUM_ROWS, tq), 0) == 0).astype(F32)
    heads = range(HEADS_PER_QBLOCK)
    qs = [qf[:, e * FOX_DH:(e + 1) * FOX_DH].astype(BF16) for e in heads]
    cqs = [jnp.sum(jnp.where(lane == hp * HEADS_PER_QBLOCK + e, cumq_ref[...], 0.0), axis=1, keepdims=True)
           for e in heads]

    def block(kb, carry, diagonal):
        start = pl.multiple_of(kb * tq, tq)
        new = []
        for e in heads:
            m, acc = carry[e]
            rows = slice(e * FOX_DH, (e + 1) * FOX_DH)
            kt = kt_ref[rows, pl.ds(start, tq)].astype(BF16)
            vt = jnp.concatenate([vt_ref[rows, pl.ds(start, tq)], ones_rows], axis=0).astype(BF16)
            ck = cumk_ref[e:e + 1, pl.ds(start, tq)]
            s = _bdot(qs[e], kt) + cqs[e] - ck
            if diagonal:
                qpos = lax.broadcasted_iota(I32, (tq, 1), 0)
                kpos = lax.broadcasted_iota(I32, (1, tq), 1)
                s = jnp.where(kpos <= qpos, s, NEG_INF)
            m_new = jnp.maximum(m, jnp.max(s, axis=1, keepdims=True))
            a = jnp.exp(m - m_new)
            p = jnp.exp(s - m_new)
            new.append((m_new, a * acc + lax.dot_general(p.astype(BF16), vt, _NT, preferred_element_type=F32)))
        return tuple(new)

    init = tuple((jnp.full((tq, 1), -jnp.inf, F32), jnp.zeros((tq, FOX_DH + SUM_ROWS), F32)) for _ in heads)
    carry = lax.fori_loop(0, qi, lambda kb, c: block(kb, c, False), init)
    final = block(qi, carry, True)
    o_ref[...] = jnp.concatenate([acc[:, :FOX_DH] / acc[:, FOX_DH:FOX_DH + 1] for _, acc in final],
                                 axis=1).astype(o_ref.dtype)


def _fox_prompt(q, kt, vt, cum_tok, cum_row, batch, tq):
    t, fd = q.shape
    l = t // batch
    nq = l // tq
    n_pairs = fd // LANES
    kern = functools.partial(_fox_prompt_kernel, tq=tq)
    return pl.pallas_call(
        kern,
        grid=(batch, n_pairs, nq),
        in_specs=[pl.BlockSpec((tq, LANES), lambda bi, hp, qi: (bi * nq + qi, hp)),
                  pl.BlockSpec((None, LANES, l), lambda bi, hp, qi: (bi, hp, 0)),
                  pl.BlockSpec((None, LANES, l), lambda bi, hp, qi: (bi, hp, 0)),
                  pl.BlockSpec((tq, LANES), lambda bi, hp, qi: (bi * nq + qi, 0)),
                  pl.BlockSpec((None, None, HEADS_PER_QBLOCK, l), lambda bi, hp, qi: (bi, hp, 0, 0))],
        out_specs=pl.BlockSpec((tq, LANES), lambda bi, hp, qi: (bi * nq + qi, hp)),
        out_shape=jax.ShapeDtypeStruct((t, fd), BF16),
        compiler_params=_params(("parallel", "parallel", "arbitrary")),
        name="fox_prompt",
    )(q, kt, vt, cum_tok, cum_row)


N_QROWS = 128
PAGES_PER_STEP = 8
TAIL_ONES_ROW0 = 3 * FOX_HEADS


def _fox_sample_kernel(pt_ref, q_ref, lfo_ref, kto_ref, vto_ref, *rest, n_steps, dec_seq):
    del pt_ref
    cache_refs = rest[:3 * PAGES_PER_STEP]
    o_ref = rest[3 * PAGES_PER_STEP]
    qbd_ref, qtail_ref, m_ref, l_ref, acc_ref, carry_ref = rest[3 * PAGES_PER_STEP + 1:]
    step = pl.program_id(1)
    n_iota = lax.broadcasted_iota(I32, (N_QROWS, 1), 0)
    row_head = n_iota // dec_seq
    row_q = n_iota % dec_seq
    pos = lax.broadcasted_iota(I32, (1, PAGE_SIZE), 1)
    r = lax.broadcasted_iota(I32, (PAGE_SIZE, PAGE_SIZE), 0)
    c = lax.broadcasted_iota(I32, (PAGE_SIZE, PAGE_SIZE), 1)

    def lane_sums(x_t, sel):
        hi, mid, lo = _split3(x_t)
        return (_bdot(lo, sel) + _bdot(mid, sel)) + _bdot(hi, sel)

    def bias_rows(key_bias_t):
        hi, mid, lo = _split3(key_bias_t)
        ones_rows = (lax.broadcasted_iota(I32, (SUBLANES, PAGE_SIZE), 0) < 3).astype(F32)
        return jnp.concatenate(
            [hi.astype(F32), mid.astype(F32), lo.astype(F32), ones_rows,
             jnp.zeros((LANES - TAIL_ONES_ROW0 - SUBLANES, PAGE_SIZE), F32)], axis=0).astype(BF16)

    def attend(kts, vts, key_biases_t, valid):
        kt = jnp.concatenate([x.astype(BF16) for x in kts], axis=1)
        vt = jnp.concatenate([x.astype(BF16) for x in vts], axis=1)
        ktail = jnp.concatenate([bias_rows(x) for x in key_biases_t], axis=1)
        s = _bdot(qbd_ref[...], kt) + _bdot(qtail_ref[...], ktail)
        if valid is not None:
            s = jnp.where(valid, s, NEG_INF)
        m_old = m_ref[...]
        m_new = jnp.maximum(m_old, jnp.max(s, axis=1, keepdims=True))
        a = jnp.exp(m_old - m_new)
        p = jnp.exp(s - m_new)
        l_ref[...] = a * l_ref[...] + jnp.sum(p, axis=1, keepdims=True)
        acc_ref[...] = a * acc_ref[...] + lax.dot_general(
            p.astype(BF16), vt, _NT, preferred_element_type=F32)
        m_ref[...] = m_new

    @pl.when(step == 0)
    def _own_rows():
        q = q_ref[0] * (FOX_DH ** -0.5)
        qt = jnp.concatenate([q] * FOX_HEADS, axis=0)
        rr = lax.broadcasted_iota(I32, qt.shape, 0)
        cc = lax.broadcasted_iota(I32, qt.shape, 1)
        qbd_ref[...] = jnp.where(cc // FOX_DH == rr // dec_seq, qt, 0.0).astype(BF16)
        prefix_t = lane_sums(lfo_ref[0], (r <= c).astype(BF16))
        rep = (lax.broadcasted_iota(I32, (N_QROWS, FOX_HEADS), 1)
               == lax.broadcasted_iota(I32, (N_QROWS, FOX_HEADS), 0) // dec_seq).astype(BF16)
        hi, mid, lo = _split3(prefix_t)
        by_row = (_bdot(rep, lo) + _bdot(rep, mid)) + _bdot(rep, hi)
        pre_col = jnp.sum(jnp.where(pos == row_q, by_row, 0.0), axis=1, keepdims=True)
        hi, mid, lo = (x.astype(F32) for x in _split3(pre_col))
        lane = lax.broadcasted_iota(I32, (N_QROWS, LANES), 1)
        tail = jnp.where(jnp.logical_and(lane < TAIL_ONES_ROW0, lane % FOX_HEADS == row_head), 1.0, 0.0)
        tail = jnp.where(lane == TAIL_ONES_ROW0, hi, tail)
        tail = jnp.where(lane == TAIL_ONES_ROW0 + 1, mid, tail)
        tail = jnp.where(lane == TAIL_ONES_ROW0 + 2, lo, tail)
        qtail_ref[...] = tail.astype(BF16)
        m_ref[...] = jnp.full(m_ref.shape, -jnp.inf, F32)
        l_ref[...] = jnp.zeros(l_ref.shape, F32)
        acc_ref[...] = jnp.zeros(acc_ref.shape, F32)
        carry_ref[...] = jnp.zeros(carry_ref.shape, F32)
        valid = jnp.logical_and(pos <= row_q, pos < dec_seq)
        attend([kto_ref[0]], [vto_ref[0]], [-prefix_t], valid)

    @pl.when(step > 0)
    def _cache_pages():
        after = (r > c).astype(BF16)
        carry = carry_ref[...]
        kts, vts, biases = [], [], []
        for u in range(PAGES_PER_STEP):
            kt_ref, vt_ref, lf_ref = cache_refs[3 * u:3 * u + 3]
            lf_t = lf_ref[0]
            kts.append(kt_ref[0])
            vts.append(vt_ref[0])
            biases.append(lane_sums(lf_t, after) + carry)
            carry = carry + jnp.sum(lf_t, axis=1, keepdims=True)
        attend(kts, vts, biases, None)
        carry_ref[...] = carry

    @pl.when(step == n_steps - 1)
    def _finish():
        col_head = lax.broadcasted_iota(I32, (dec_seq, D_MODEL), 1) // FOX_DH
        out = jnp.zeros((dec_seq, D_MODEL), F32)
        for h in range(FOX_HEADS):
            rows = slice(h * dec_seq, (h + 1) * dec_seq)
            out = out + jnp.where(col_head == h, acc_ref[rows, :] / l_ref[rows, :], 0.0)
        o_ref[0] = out


def _fox_sample(q, lft_own, kt_own, vt_own, kt_cache, vt_cache, lft_cache, page_table):
    db, dec_seq, _ = q.shape
    n_pages = page_table.shape[1]
    n_steps = 1 + n_pages // PAGES_PER_STEP
    fd = FOX_HEADS * FOX_DH
    kern = functools.partial(_fox_sample_kernel, n_steps=n_steps, dec_seq=dec_seq)

    def page(u):
        return lambda b, s, pt: (pt[b, n_pages - 1 - (jnp.maximum(s, 1) - 1) * PAGES_PER_STEP - u], 0, 0)

    cache_specs = []
    cache_args = []
    for u in range(PAGES_PER_STEP):
        cache_specs += [pl.BlockSpec((1, fd, PAGE_SIZE), page(u)),
                        pl.BlockSpec((1, fd, PAGE_SIZE), page(u)),
                        pl.BlockSpec((1, FOX_HEADS, PAGE_SIZE), page(u))]
        cache_args += [kt_cache, vt_cache, lft_cache]
    grid_spec = pltpu.PrefetchScalarGridSpec(
        num_scalar_prefetch=1,
        grid=(db, n_steps),
        in_specs=[pl.BlockSpec((1, dec_seq, D_MODEL), lambda b, s, pt: (b, 0, 0)),
                  pl.BlockSpec((1, FOX_HEADS, PAGE_SIZE), lambda b, s, pt: (b, 0, 0)),
                  pl.BlockSpec((1, fd, PAGE_SIZE), lambda b, s, pt: (b, 0, 0)),
                  pl.BlockSpec((1, fd, PAGE_SIZE), lambda b, s, pt: (b, 0, 0))] + cache_specs,
        out_specs=pl.BlockSpec((1, dec_seq, D_MODEL), lambda b, s, pt: (b, 0, 0)),
        scratch_shapes=[pltpu.VMEM((N_QROWS, fd), BF16),
                        pltpu.VMEM((N_QROWS, LANES), BF16),
                        pltpu.VMEM((N_QROWS, 1), F32),
                        pltpu.VMEM((N_QROWS, 1), F32),
                        pltpu.VMEM((N_QROWS, fd), F32),
                        pltpu.VMEM((FOX_HEADS, 1), F32)],
    )
    return pl.pallas_call(
        kern,
        grid_spec=grid_spec,
        out_shape=jax.ShapeDtypeStruct((db, dec_seq, D_MODEL), F32),
        compiler_params=_params(("parallel", "arbitrary")),
        name="fox_sample",
    )(page_table, q, lft_own, kt_own, vt_own, *cache_args)


def _moe_params(layer, w_router, b_router, w_gate, w_up, w_down, ws_gate, ws_up, ws_down):
    wr_t = w_router[layer].T
    wr_hi = wr_t.astype(BF16)
    wr_lo = (wr_t - wr_hi.astype(F32)).astype(BF16)
    return dict(
        wr_hi=wr_hi, wr_lo=wr_lo, b_col=b_router[layer].reshape(N_EXPERTS, 1),
        layer=layer, wg=w_gate, wu=w_up, wd=w_down,
        wsgu=jnp.concatenate([ws_gate[layer], ws_up[layer]], axis=1).astype(BF16),
        wsd=ws_down[layer].astype(BF16))


def kernel(x_prompt, x_sample, state_ret, cache_k, cache_v, cache_logf, page_table, ret_w_in, ret_w_out,
           fox_w_kvf, fox_b_f, fox_w_q, fox_w_out, ln_g, ln_b, moe_w_router, moe_b_router, moe_w_gate,
           moe_w_up, moe_w_down, moe_ws_gate, moe_ws_up, moe_ws_down):
    bp, lp, d = x_prompt.shape
    bs, ls, _ = x_sample.shape
    tp, ts = bp * lp, bs * ls
    assert fox_heads_rows(ls) == N_QROWS
    xp = x_prompt.reshape(tp, d)
    xs = x_sample.reshape(ts, d)
    hq = RET_HEADS * RET_DK
    hv = RET_HEADS * RET_DV
    fd = FOX_HEADS * FOX_DH
    tm_p, tn_p = PROMPT_TILE_ROWS, PROJ_TILE_COLS
    tc_p = MOE_TOKEN_TILE

    wg_all, wu_all, wd_all = moe_w_gate.astype(BF16), moe_w_up.astype(BF16), moe_w_down.astype(BF16)
    moe = [_moe_params(layer, moe_w_router, moe_b_router, wg_all, wu_all, wd_all,
                       moe_ws_gate, moe_ws_up, moe_ws_down) for layer in range(DEPTH)]

    w_in = ret_w_in[0].astype(BF16)
    w_out = ret_w_out[0].astype(BF16)
    cos_p, sin_p = _rope_tables(jnp.arange(lp))
    cos_s, sin_s = _rope_tables(PAST_LEN + jnp.arange(ls))
    cos_s, sin_s = jnp.tile(cos_s, (bs, 1)), jnp.tile(sin_s, (bs, 1))

    qk_p = _qk_rope(xp, w_in, cos_p, sin_p, tm_p)
    v_p = _matmul(xp, w_in, 2 * hq, hv, BF16, tm_p, tn_p)
    g_p = _matmul(xp, w_in, 2 * hq + hv, hv, F32, tm_p, tn_p)
    o_p, st_p = _retention(qk_p, v_p, g_p, jnp.zeros((bp, RET_HEADS, RET_DK, RET_DV), F32),
                           _retention_decays(min(RET_CHUNK, lp)), bp, lp, BF16)
    xp = _mm_res_ln(o_p, w_out, xp, ln_g[0, 0], ln_b[0, 0], tm_p)

    qk_s = _qk_rope(xs, w_in, cos_s, sin_s, ts)
    v_s = _matmul(xs, w_in, 2 * hq, hv, F32, ts, tn_p)
    g_s = _matmul(xs, w_in, 2 * hq + hv, hv, F32, ts, tn_p)
    o_s, st_s = _retention(qk_s, v_s, g_s, state_ret[0], _retention_decays(min(RET_CHUNK, ls)), bs, ls, F32)
    xs = _mm_res_ln(o_s, w_out, xs, ln_g[0, 0], ln_b[0, 0], ts)

    xp = _moe_ln(xp, moe[0], ln_g[0, 1], ln_b[0, 1], tc_p)
    xs = _moe_ln(xs, moe[0], ln_g[0, 1], ln_b[0, 1], ts)

    w_kvf = fox_w_kvf.astype(BF16)
    w_kv_t = w_kvf[:, :2 * fd].T
    w_f_pad = jnp.zeros((d, LANES), BF16).at[:, :FOX_HEADS].set(w_kvf[:, 2 * fd:])
    b_f_pad = jnp.zeros((1, LANES), F32).at[0, :FOX_HEADS].set(fox_b_f)
    kt_p, vt_p = _matmul_t_pair(xp, w_kv_t, bp, tm_p)
    lf_p = _logf_proj(xp, w_f_pad, b_f_pad, tm_p)
    per_seq = lambda a_t: a_t[0].reshape(fd, bs, ls).transpose(1, 0, 2)
    kt_s, vt_s = (per_seq(a) for a in _matmul_t_pair(xs, w_kv_t, 1, ts))
    lf_s = _logf_proj(xs, w_f_pad, b_f_pad, ts)

    as_blhd = lambda a_t, b, l: a_t.reshape(b, FOX_HEADS, FOX_DH, l).transpose(0, 3, 1, 2)
    k_p, v_p4 = as_blhd(kt_p, bp, lp), as_blhd(vt_p, bp, lp)
    k_s, v_s4 = as_blhd(kt_s, bs, ls), as_blhd(vt_s, bs, ls)

    w_q = fox_w_q[0].astype(BF16)
    w_o = fox_w_out[0].astype(BF16)
    q_p = _matmul(xp, w_q, 0, fd, BF16, tm_p, tn_p)
    cum_p = _cumsum_rows(lf_p, bp, lp)
    cum_row = cum_p[:, :FOX_HEADS].reshape(bp, lp, FOX_HEADS).transpose(0, 2, 1).reshape(
        bp, FOX_HEADS // HEADS_PER_QBLOCK, HEADS_PER_QBLOCK, lp)
    att_p = _fox_prompt(q_p, kt_p, vt_p, cum_p, cum_row, bp, ATTN_TILE)
    xp = _mm_res_ln(att_p, w_o, xp, ln_g[1, 0], ln_b[1, 0], tm_p)

    q_s = _matmul(xs, w_q, 0, fd, F32, ts, tn_p)
    n_pool = cache_k.shape[0]
    kt_cache = cache_k.transpose(0, 2, 3, 1).reshape(n_pool, fd, PAGE_SIZE)
    vt_cache = cache_v.transpose(0, 2, 3, 1).reshape(n_pool, fd, PAGE_SIZE)
    lft_cache = cache_logf.transpose(0, 2, 1)
    pad_pos = lambda a_t: jnp.pad(a_t, ((0, 0), (0, 0), (0, PAGE_SIZE - ls)))
    lft_own = pad_pos(lf_s[:, :FOX_HEADS].reshape(bs, ls, FOX_HEADS).transpose(0, 2, 1))
    att_s = _fox_sample(q_s.reshape(bs, ls, d), lft_own, pad_pos(kt_s), pad_pos(vt_s),
                        kt_cache, vt_cache, lft_cache, page_table)
    xs = _mm_res_ln(att_s.reshape(ts, fd), w_o, xs, ln_g[1, 0], ln_b[1, 0], ts)

    xp = _moe_ln(xp, moe[1], ln_g[1, 1], ln_b[1, 1], tc_p)
    xs = _moe_ln(xs, moe[1], ln_g[1, 1], ln_b[1, 1], ts)

    return (xp.reshape(bp, lp, d), xs.reshape(bs, ls, d), st_p[None], st_s[None],
            k_p, v_p4, lf_p[:, :FOX_HEADS].reshape(bp, lp, FOX_HEADS),
            k_s, v_s4, lf_s[:, :FOX_HEADS].reshape(bs, ls, FOX_HEADS))


def fox_heads_rows(dec_seq):
    return FOX_HEADS * dec_seq
```

```python
import functools

import jax
import jax.numpy as jnp
from jax import lax
from jax.experimental import pallas as pl
from jax.experimental.pallas import tpu as pltpu

F32 = jnp.float32
BF16 = jnp.bfloat16
I32 = jnp.int32

D_MODEL = 1024
DEPTH = 2
PAST_LEN = 8192
PAGE_SIZE = 128
N_A_LAYERS = DEPTH // 2
RET_HEADS = 4
RET_DK = D_MODEL // RET_HEADS
RET_DV = 2 * RET_DK
RET_CHUNK = 128
ROPE_BASE = 10000.0
FOX_HEADS = 16
FOX_DH = D_MODEL // FOX_HEADS
NEG_INF = -1e30
N_EXPERTS = 64
TOPK = 8
N_GROUPS = 8
GROUP_SIZE = N_EXPERTS // N_GROUPS
TOPK_GROUPS = 4
FF_EXPERT = D_MODEL // 4
ROUTED_SCALE = 2.5
DEEPNORM_ALPHA = (2.0 * DEPTH) ** 0.25
LN_EPS = 1e-5
GN_EPS = 1e-5

LANES = 128
SUBLANES = 8
ROW_VREGS = D_MODEL // LANES
MXU_COLS = 256
EXP_BLOCK = 160
EXPERTS_PER_STEP = 4
VMEM_LIMIT_BIG = 60 * 1024 * 1024
VMEM_LIMIT = 48 * 1024 * 1024

PROMPT_TILE_ROWS = 1024
PROJ_TILE_COLS = 1024
MOE_TOKEN_TILE = 1024
ATTN_TILE = 512

_NT = (((1,), (1,)), ((), ()))
_TN = (((0,), (0,)), ((), ()))


def _params(sem, vmem=VMEM_LIMIT):
    return pltpu.CompilerParams(dimension_semantics=sem, vmem_limit_bytes=vmem)


def _bdot(a, b):
    return jnp.dot(a, b, preferred_element_type=F32)


def _layer_norm_rows(z, g, b):
    mu = jnp.mean(z, axis=-1, keepdims=True)
    zc = z - mu
    var = jnp.mean(zc * zc, axis=-1, keepdims=True)
    return zc * lax.rsqrt(var + LN_EPS) * g + b


def _silu(x):
    return x * jax.nn.sigmoid(x)


def _split3(x):
    hi = x.astype(BF16)
    r1 = x - hi.astype(F32)
    mid = r1.astype(BF16)
    lo = (r1 - mid.astype(F32)).astype(BF16)
    return hi, mid, lo


def _mm_kernel(x_ref, w_ref, o_ref):
    o_ref[...] = _bdot(x_ref[...].astype(BF16), w_ref[...]).astype(o_ref.dtype)


def _matmul(x, w, col0, n, out_dtype, tm, tn):
    m, k = x.shape
    j0 = col0 // tn
    return pl.pallas_call(
        _mm_kernel,
        grid=(m // tm, n // tn),
        in_specs=[pl.BlockSpec((tm, k), lambda i, j: (i, 0)),
                  pl.BlockSpec((k, tn), lambda i, j: (0, j + j0))],
        out_specs=pl.BlockSpec((tm, tn), lambda i, j: (i, j)),
        out_shape=jax.ShapeDtypeStruct((m, n), out_dtype),
        compiler_params=_params(("parallel", "arbitrary")),
        name="proj",
    )(x, w)


def _mm_t_pair_kernel(x_ref, wt_ref, o0_ref, o1_ref):
    j = pl.program_id(1)
    res = lax.dot_general(wt_ref[...], x_ref[...].astype(BF16), _NT, preferred_element_type=F32)

    @pl.when(j == 0)
    def _():
        o0_ref[...] = res

    @pl.when(j == 1)
    def _():
        o1_ref[...] = res


def _matmul_t_pair(x, w_t, batch, tm):
    m, k = x.shape
    n = w_t.shape[0] // 2
    seq = m // batch
    nper = seq // tm
    out_spec = pl.BlockSpec((None, n, tm), lambda i, j: (i // nper, 0, i % nper))
    return pl.pallas_call(
        _mm_t_pair_kernel,
        grid=(m // tm, 2),
        in_specs=[pl.BlockSpec((tm, k), lambda i, j: (i, 0)),
                  pl.BlockSpec((n, k), lambda i, j: (j, 0))],
        out_specs=[out_spec, out_spec],
        out_shape=[jax.ShapeDtypeStruct((batch, n, seq), F32)] * 2,
        compiler_params=_params(("parallel", "arbitrary")),
        name="proj_t",
    )(x, w_t)


def _mm_rope_kernel(x_ref, w_ref, cos_ref, sin_ref, o_ref, *, k_scale):
    j = pl.program_id(1)
    acc = _bdot(x_ref[...].astype(BF16), w_ref[...])
    half = RET_DK // 2
    c = cos_ref[...]
    s = sin_ref[...]
    sc = jnp.where(j == 0, 1.0, k_scale).astype(F32)
    for h in range(RET_HEADS):
        lo = h * RET_DK
        x1 = acc[:, lo:lo + half]
        x2 = acc[:, lo + half:lo + RET_DK]
        o_ref[:, lo:lo + half] = (x1 * c - x2 * s) * sc
        o_ref[:, lo + half:lo + RET_DK] = (x2 * c + x1 * s) * sc


def _qk_rope(x, w, cos_rows, sin_rows, tm):
    m, k = x.shape
    hq = RET_HEADS * RET_DK
    nper = cos_rows.shape[0] // tm
    kern = functools.partial(_mm_rope_kernel, k_scale=RET_DK ** -0.5)
    return pl.pallas_call(
        kern,
        grid=(m // tm, 2),
        in_specs=[pl.BlockSpec((tm, k), lambda i, j: (i, 0)),
                  pl.BlockSpec((k, hq), lambda i, j: (0, j)),
                  pl.BlockSpec((tm, RET_DK // 2), lambda i, j: (i % nper, 0)),
                  pl.BlockSpec((tm, RET_DK // 2), lambda i, j: (i % nper, 0))],
        out_specs=pl.BlockSpec((tm, hq), lambda i, j: (i, j)),
        out_shape=jax.ShapeDtypeStruct((m, 2 * hq), F32),
        compiler_params=_params(("parallel", "arbitrary")),
        name="qk_rope",
    )(x, w, cos_rows, sin_rows)


def _mm_res_ln_kernel(a_ref, w_ref, x_ref, g_ref, b_ref, o_ref):
    mix = _bdot(a_ref[...].astype(BF16), w_ref[...])
    z = DEEPNORM_ALPHA * x_ref[...] + mix
    o_ref[...] = _layer_norm_rows(z, g_ref[...], b_ref[...])


def _mm_res_ln(a, w, x, g, b, tm):
    m, k = a.shape
    d = w.shape[1]
    return pl.pallas_call(
        _mm_res_ln_kernel,
        grid=(m // tm,),
        in_specs=[pl.BlockSpec((tm, k), lambda i: (i, 0)),
                  pl.BlockSpec((k, d), lambda i: (0, 0)),
                  pl.BlockSpec((tm, d), lambda i: (i, 0)),
                  pl.BlockSpec((1, d), lambda i: (0, 0)),
                  pl.BlockSpec((1, d), lambda i: (0, 0))],
        out_specs=pl.BlockSpec((tm, d), lambda i: (i, 0)),
        out_shape=jax.ShapeDtypeStruct((m, d), F32),
        compiler_params=_params(("parallel",)),
        name="mix_out_ln",
    )(a, w, x, g.reshape(1, d), b.reshape(1, d))


def _mm_logsig_kernel(x_ref, w_ref, b_ref, o_ref):
    z = _bdot(x_ref[...].astype(BF16), w_ref[...]) + b_ref[...]
    o_ref[...] = -(jnp.maximum(-z, 0.0) + jnp.log1p(jnp.exp(-jnp.abs(z))))


def _logf_proj(x, w_pad, b_pad, tm):
    m, k = x.shape
    return pl.pallas_call(
        _mm_logsig_kernel,
        grid=(m // tm,),
        in_specs=[pl.BlockSpec((tm, k), lambda i: (i, 0)),
                  pl.BlockSpec((k, LANES), lambda i: (0, 0)),
                  pl.BlockSpec((1, LANES), lambda i: (0, 0))],
        out_specs=pl.BlockSpec((tm, LANES), lambda i: (i, 0)),
        out_shape=jax.ShapeDtypeStruct((m, LANES), F32),
        compiler_params=_params(("parallel",)),
        name="logf_proj",
    )(x, w_pad, b_pad)


def _ret_kernel(q_ref, k_ref, v_ref, g_ref, s0_ref, din_ref, dq_ref, dk_ref, dc_ref,
                o_ref, sout_ref, state_ref):
    c = pl.program_id(1)

    @pl.when(c == 0)
    def _():
        state_ref[...] = s0_ref[...]

    for h in range(RET_HEADS):
        qcols = slice(h * RET_DK, (h + 1) * RET_DK)
        vcols = slice(h * RET_DV, (h + 1) * RET_DV)
        q = q_ref[:, qcols].astype(BF16)
        k = k_ref[:, qcols]
        v = v_ref[:, vcols].astype(BF16)
        s_prev = state_ref[h]
        scores = lax.dot_general(q, k.astype(BF16), _NT, preferred_element_type=F32) * din_ref[h]
        o = _bdot(scores.astype(BF16), v) + _bdot(q, s_prev.astype(BF16)) * dq_ref[h]
        kd = (k * dk_ref[h]).astype(BF16)
        s_new = s_prev * dc_ref[h] + lax.dot_general(kd, v, _TN, preferred_element_type=F32)
        state_ref[h] = s_new

        mu = jnp.mean(o, axis=-1, keepdims=True)
        oc = o - mu
        var = jnp.mean(oc * oc, axis=-1, keepdims=True)
        on = oc * lax.rsqrt(var + GN_EPS)
        o_ref[:, vcols] = (_silu(g_ref[:, vcols]) * on).astype(o_ref.dtype)

    @pl.when(c == pl.num_programs(1) - 1)
    def _():
        sout_ref[...] = state_ref[...]


def _retention(qk, v, g, s0, decays, batch, seq, out_dtype):
    chunk = min(RET_CHUNK, seq)
    nc = seq // chunk
    din, dq, dk, dc = decays
    h_ = RET_HEADS
    hq, hv = h_ * RET_DK, h_ * RET_DV
    whole = lambda a: pl.BlockSpec(a.shape, lambda b, c: (0,) * a.ndim)
    return pl.pallas_call(
        _ret_kernel,
        grid=(batch, nc),
        in_specs=[pl.BlockSpec((chunk, hq), lambda b, c: (b * nc + c, 0)),
                  pl.BlockSpec((chunk, hq), lambda b, c: (b * nc + c, 1)),
                  pl.BlockSpec((chunk, hv), lambda b, c: (b * nc + c, 0)),
                  pl.BlockSpec((chunk, hv), lambda b, c: (b * nc + c, 0)),
                  pl.BlockSpec((None, h_, RET_DK, RET_DV), lambda b, c: (b, 0, 0, 0)),
                  whole(din), whole(dq), whole(dk), whole(dc)],
        out_specs=[pl.BlockSpec((chunk, hv), lambda b, c: (b * nc + c, 0)),
                   pl.BlockSpec((None, h_, RET_DK, RET_DV), lambda b, c: (b, 0, 0, 0))],
        out_shape=[jax.ShapeDtypeStruct((batch * seq, hv), out_dtype),
                   jax.ShapeDtypeStruct((batch, h_, RET_DK, RET_DV), F32)],
        scratch_shapes=[pltpu.VMEM((h_, RET_DK, RET_DV), F32)],
        compiler_params=_params(("parallel", "arbitrary")),
        name="retention",
    )(qk, qk, v, g, s0, din, dq, dk, dc)


def _retention_decays(chunk):
    h = jnp.arange(RET_HEADS, dtype=F32)
    log_g = jnp.log1p(-(2.0 ** (-5.0 - h)))
    idx = jnp.arange(chunk, dtype=F32)
    rel = idx[:, None] - idx[None, :]
    din = jnp.where(rel >= 0, jnp.exp(log_g[:, None, None] * jnp.maximum(rel, 0.0)), 0.0)
    dq = jnp.exp(log_g[:, None] * (idx[None, :] + 1.0))[:, :, None]
    dk = jnp.exp(log_g[:, None] * (chunk - 1.0 - idx[None, :]))[:, :, None]
    dc = jnp.exp(log_g * chunk)[:, None, None]
    return din, dq, dk, dc


def _rope_tables(pos):
    half = RET_DK // 2
    inv_freq = ROPE_BASE ** (-jnp.arange(half, dtype=F32) / half)
    ang = pos.astype(F32)[:, None] * inv_freq[None, :]
    return jnp.cos(ang), jnp.sin(ang)


def _router_kernel(x_ref, wh_ref, wl_ref, b_ref, slot_ref, wgt_ref, off_ref, cnt_ref):
    tc = x_ref.shape[0]
    x = x_ref[...]
    xh = x.astype(BF16)
    xl = (x - xh.astype(F32)).astype(BF16)
    wh = wh_ref[...]
    wl = wl_ref[...]
    logits = (lax.dot_general(wh, xh, _NT, preferred_element_type=F32)
              + lax.dot_general(wh, xl, _NT, preferred_element_type=F32)
              + lax.dot_general(wl, xh, _NT, preferred_element_type=F32))
    scores = jax.nn.sigmoid(logits)
    biased = scores + b_ref[...]

    jj = lax.broadcasted_iota(I32, (GROUP_SIZE, tc), 0)
    groups = [biased[g * GROUP_SIZE:(g + 1) * GROUP_SIZE, :] for g in range(N_GROUPS)]
    gscore = []
    for rows in groups:
        m1 = jnp.max(rows, axis=0, keepdims=True)
        j1 = jnp.min(jnp.where(rows == m1, jj, GROUP_SIZE), axis=0, keepdims=True)
        m2 = jnp.max(jnp.where(jj == j1, -jnp.inf, rows), axis=0, keepdims=True)
        gscore.append(m1 + m2)
    gsel = [jnp.zeros((1, tc), dtype=jnp.bool_) for _ in range(N_GROUPS)]
    for _ in range(TOPK_GROUPS):
        gm = gscore[0]
        for sc in gscore[1:]:
            gm = jnp.maximum(gm, sc)
        found = jnp.zeros((1, tc), dtype=jnp.bool_)
        for g in range(N_GROUPS):
            hit = jnp.logical_and(gscore[g] == gm, jnp.logical_not(found))
            found = jnp.logical_or(found, hit)
            gsel[g] = jnp.logical_or(gsel[g], hit)
            gscore[g] = jnp.where(hit, -jnp.inf, gscore[g])
    masked = jnp.concatenate(
        [jnp.where(jnp.broadcast_to(gsel[g], groups[g].shape), groups[g], -jnp.inf)
         for g in range(N_GROUPS)], axis=0)
    ei = lax.broadcasted_iota(I32, masked.shape, 0)
    hits = []
    wk = []
    for _ in range(TOPK):
        m = jnp.max(masked, axis=0, keepdims=True)
        first = jnp.min(jnp.where(masked == m, ei, N_EXPERTS), axis=0, keepdims=True)
        hit = ei == first
        hits.append(hit)
        wk.append(jnp.sum(jnp.where(hit, scores, 0.0), axis=0, keepdims=True))
        masked = jnp.where(hit, -jnp.inf, masked)
    wsum = wk[0]
    for w in wk[1:]:
        wsum = wsum + w
    wgt_ref[...] = jnp.concatenate([w / wsum * ROUTED_SCALE for w in wk], axis=0)

    chosen = hits[0]
    for hit in hits[1:]:
        chosen = jnp.logical_or(chosen, hit)
    chosen = chosen.astype(BF16)
    r = lax.broadcasted_iota(I32, (tc, tc), 0)
    c = lax.broadcasted_iota(I32, (tc, tc), 1)
    rank = _bdot(chosen, (r <= c).astype(BF16))
    er = lax.broadcasted_iota(I32, (N_EXPERTS, N_EXPERTS), 0)
    ec = lax.broadcasted_iota(I32, (N_EXPERTS, N_EXPERTS), 1)
    below = _bdot((ec < er).astype(BF16), chosen)
    off = jnp.sum(below, axis=1, keepdims=True)
    cnt = rank[:, tc - 1:tc]
    pos = (off + rank - 1.0) * ROW_VREGS
    slot_ref[...] = jnp.concatenate(
        [jnp.sum(jnp.where(hit, pos, 0.0), axis=0, keepdims=True) for hit in hits], axis=0).astype(I32)
    off_ref[...] = jnp.broadcast_to(off, off_ref.shape).astype(I32)
    cnt_ref[...] = jnp.broadcast_to(cnt, cnt_ref.shape).astype(I32)


def _router(x, wr_hi, wr_lo, b_col, tc):
    t, d = x.shape
    n = t // tc
    return pl.pallas_call(
        _router_kernel,
        grid=(n,),
        in_specs=[pl.BlockSpec((tc, d), lambda i: (i, 0)),
                  pl.BlockSpec((N_EXPERTS, d), lambda i: (0, 0)),
                  pl.BlockSpec((N_EXPERTS, d), lambda i: (0, 0)),
                  pl.BlockSpec((N_EXPERTS, 1), lambda i: (0, 0))],
        out_specs=[pl.BlockSpec((None, TOPK, tc), lambda i: (i, 0, 0)),
                   pl.BlockSpec((None, TOPK, tc), lambda i: (i, 0, 0)),
                   pl.BlockSpec((None, N_EXPERTS, LANES), lambda i: (i, 0, 0)),
                   pl.BlockSpec((None, N_EXPERTS, LANES), lambda i: (i, 0, 0))],
        out_shape=[jax.ShapeDtypeStruct((n, TOPK, tc), I32),
                   jax.ShapeDtypeStruct((n, TOPK, tc), F32),
                   jax.ShapeDtypeStruct((n, N_EXPERTS, LANES), I32),
                   jax.ShapeDtypeStruct((n, N_EXPERTS, LANES), I32)],
        compiler_params=_params(("parallel",)),
        name="router",
    )(x, wr_hi, wr_lo, b_col)


def _experts_kernel(off_ref, cnt_ref, slot_ref, wgt_ref, x_ref, wg_ref, wu_ref, wd_ref,
                    o_ref, sorted_ref):
    tc = x_ref.shape[0]
    ci = pl.program_id(0)
    e = pl.program_id(1)
    tok_per_line = LANES // TOPK

    @pl.when(e == 0)
    def _dispatch():
        for kc in range(ROW_VREGS):
            o_ref[pl.ds(kc, tc, stride=ROW_VREGS), :] = x_ref[:, kc * LANES:(kc + 1) * LANES]
        sorted_ref[pl.ds(TOPK * tc * ROW_VREGS, EXP_BLOCK * ROW_VREGS), :] = jnp.zeros(
            (EXP_BLOCK * ROW_VREGS, LANES), F32)

        def body(i, carry):
            slot_line = slot_ref.at[0, i]
            for u in range(tok_per_line):
                t = i * tok_per_line + u
                row = o_ref[pl.ds(pl.multiple_of(t * ROW_VREGS, ROW_VREGS), ROW_VREGS), :]
                for k in range(TOPK):
                    s = slot_line[u * TOPK + k]
                    sorted_ref[pl.ds(pl.multiple_of(s, ROW_VREGS), ROW_VREGS), :] = row
            return carry

        lax.fori_loop(0, tc // tok_per_line, body, 0)

    base = ci * N_EXPERTS + e * EXPERTS_PER_STEP
    offs = [off_ref[base + j] for j in range(EXPERTS_PER_STEP)]
    cnts = [cnt_ref[base + j] for j in range(EXPERTS_PER_STEP)]
    nblks = [lax.div(cnt + (EXP_BLOCK - 1), EXP_BLOCK) for cnt in cnts]
    nmax = nblks[0]
    for nb in nblks[1:]:
        nmax = jnp.maximum(nmax, nb)
    rowid = lax.broadcasted_iota(I32, (EXP_BLOCK, 1), 0)

    def ffn_blocks(i, carry):
        loaded = []
        for j in range(EXPERTS_PER_STEP):
            active = i < nblks[j]
            row0 = jnp.where(active, offs[j] + i * EXP_BLOCK, TOPK * tc)
            nvalid = jnp.where(active, cnts[j] - i * EXP_BLOCK, 0)
            r0 = row0 * ROW_VREGS
            xbb = jnp.concatenate(
                [sorted_ref[pl.ds(r0 + kc, EXP_BLOCK, stride=ROW_VREGS), :].astype(BF16)
                 for kc in range(ROW_VREGS)], axis=1)
            loaded.append((r0, nvalid, xbb))
        hidden = [(_silu(_bdot(xbb, wg_ref[j])) * _bdot(xbb, wu_ref[j])).astype(BF16)
                  for j, (_, _, xbb) in enumerate(loaded)]
        for j, (r0, nvalid, xbb) in enumerate(loaded):
            for c0 in range(0, D_MODEL, MXU_COLS):
                y = _bdot(hidden[j], wd_ref[j, :, c0:c0 + MXU_COLS])
                y = jnp.where(rowid < nvalid, y, xbb[:, c0:c0 + MXU_COLS].astype(F32))
                for kc in range(c0 // LANES, (c0 + MXU_COLS) // LANES):
                    sorted_ref[pl.ds(r0 + kc, EXP_BLOCK, stride=ROW_VREGS), :] = (
                        y[:, kc * LANES - c0:(kc + 1) * LANES - c0])
        return carry

    lax.fori_loop(0, nmax, ffn_blocks, 0)

    @pl.when(e == pl.num_programs(1) - 1)
    def _combine():
        def body(i, carry):
            slot_line = slot_ref.at[0, i]
            wgt_line = wgt_ref.at[0, i]
            for u in range(tok_per_line):
                t = i * tok_per_line + u
                acc = None
                for k in range(TOPK):
                    s = slot_line[u * TOPK + k]
                    term = wgt_line[u * TOPK + k] * sorted_ref[
                        pl.ds(pl.multiple_of(s, ROW_VREGS), ROW_VREGS), :]
                    acc = term if acc is None else acc + term
                o_ref[pl.ds(pl.multiple_of(t * ROW_VREGS, ROW_VREGS), ROW_VREGS), :] = acc
            return carry

        lax.fori_loop(0, tc // tok_per_line, body, 0)


def _routed_experts(x, slot, wgt, seg_off, seg_cnt, wg, wu, wd, layer, tc):
    t, d = x.shape
    n = t // tc
    sorted_rows = TOPK * tc + EXP_BLOCK
    grid_spec = pltpu.PrefetchScalarGridSpec(
        num_scalar_prefetch=2,
        grid=(n, N_EXPERTS // EXPERTS_PER_STEP),
        in_specs=[pl.BlockSpec((1,) + slot.shape[1:], lambda c, e, *_: (c, 0, 0), memory_space=pltpu.SMEM),
                  pl.BlockSpec((1,) + wgt.shape[1:], lambda c, e, *_: (c, 0, 0), memory_space=pltpu.SMEM),
                  pl.BlockSpec((tc, d), lambda c, e, *_: (c, 0), pipeline_mode=pl.Buffered(1)),
                  pl.BlockSpec((None, EXPERTS_PER_STEP, d, FF_EXPERT), lambda c, e, *_: (layer, e, 0, 0)),
                  pl.BlockSpec((None, EXPERTS_PER_STEP, d, FF_EXPERT), lambda c, e, *_: (layer, e, 0, 0)),
                  pl.BlockSpec((None, EXPERTS_PER_STEP, FF_EXPERT, d), lambda c, e, *_: (layer, e, 0, 0))],
        out_specs=pl.BlockSpec((tc * ROW_VREGS, LANES), lambda c, e, *_: (c, 0), pipeline_mode=pl.Buffered(1)),
        scratch_shapes=[pltpu.VMEM((sorted_rows * ROW_VREGS, LANES), F32)],
    )
    return pl.pallas_call(
        _experts_kernel,
        grid_spec=grid_spec,
        out_shape=jax.ShapeDtypeStruct((t * ROW_VREGS, LANES), F32),
        compiler_params=_params(("arbitrary", "arbitrary"), VMEM_LIMIT_BIG),
        name="routed_experts",
    )(seg_off, seg_cnt, slot, wgt, x, wg, wu, wd)


def _shared_ln_kernel(x_ref, r_ref, wgu_ref, wd_ref, g_ref, b_ref, o_ref):
    tm = x_ref.shape[0]
    x = x_ref[...]
    gu = _bdot(x.astype(BF16), wgu_ref[...])
    ff = wgu_ref.shape[1] // 2
    h = _silu(gu[:, :ff]) * gu[:, ff:]
    shared = _bdot(h.astype(BF16), wd_ref[...])
    routed = jnp.concatenate(
        [r_ref[pl.ds(kc, tm, stride=ROW_VREGS), :] for kc in range(ROW_VREGS)], axis=1)
    z = DEEPNORM_ALPHA * x + (shared + routed)
    o_ref[...] = _layer_norm_rows(z, g_ref[...], b_ref[...])


def _shared_ln(x, routed_rows, wgu, wd, g, b, tm):
    t, d = x.shape
    return pl.pallas_call(
        _shared_ln_kernel,
        grid=(t // tm,),
        in_specs=[pl.BlockSpec((tm, d), lambda i: (i, 0)),
                  pl.BlockSpec((tm * ROW_VREGS, LANES), lambda i: (i, 0)),
                  pl.BlockSpec(wgu.shape, lambda i: (0, 0)),
                  pl.BlockSpec(wd.shape, lambda i: (0, 0)),
                  pl.BlockSpec((1, d), lambda i: (0, 0)),
                  pl.BlockSpec((1, d), lambda i: (0, 0))],
        out_specs=pl.BlockSpec((tm, d), lambda i: (i, 0)),
        out_shape=jax.ShapeDtypeStruct((t, d), F32),
        compiler_params=_params(("parallel",)),
        name="shared_ln",
    )(x, routed_rows, wgu, wd, g.reshape(1, d), b.reshape(1, d))


def _moe_ln(x, p, g, b, tc):
    slot, wgt, off, cnt = _router(x, p["wr_hi"], p["wr_lo"], p["b_col"], tc)
    lines = lambda a: a.transpose(0, 2, 1).reshape(a.shape[0], tc * TOPK // LANES, LANES)
    routed = _routed_experts(x, lines(slot), lines(wgt), off[:, :, 0].reshape(-1), cnt[:, :, 0].reshape(-1),
                             p["wg"], p["wu"], p["wd"], p["layer"], tc)
    return _shared_ln(x, routed, p["wsgu"], p["wsd"], g, b, min(tc, PROMPT_TILE_ROWS))


def _cumsum_kernel(x_ref, o_ref):
    n = x_ref.shape[0]
    blk = LANES
    r = lax.broadcasted_iota(I32, (blk, blk), 0)
    c = lax.broadcasted_iota(I32, (blk, blk), 1)
    tri = (c <= r).astype(BF16)

    def body(i, carry):
        start = pl.multiple_of(i * blk, blk)
        hi, mid, lo = _split3(x_ref[pl.ds(start, blk), :])
        within = (_bdot(tri, lo) + _bdot(tri, mid)) + _bdot(tri, hi)
        o_ref[pl.ds(start, blk), :] = within + carry
        return carry + jnp.sum(x_ref[pl.ds(start, blk), :], axis=0, keepdims=True)

    lax.fori_loop(0, n // blk, body, jnp.zeros((1, x_ref.shape[1]), F32))


def _cumsum_rows(x, batch, seq):
    return pl.pallas_call(
        _cumsum_kernel,
        grid=(batch,),
        in_specs=[pl.BlockSpec((seq, x.shape[1]), lambda b: (b, 0))],
        out_specs=pl.BlockSpec((seq, x.shape[1]), lambda b: (b, 0)),
        out_shape=jax.ShapeDtypeStruct(x.shape, F32),
        compiler_params=_params(("parallel",)),
        name="logf_cumsum",
    )(x)


HEADS_PER_QBLOCK = LANES // FOX_DH
SUM_ROWS = 16


def _fox_prompt_kernel(q_ref, kt_ref, vt_ref, cumq_ref, cumk_ref, o_ref, *, tq):
    hp = pl.program_id(1)
    qi = pl.program_id(2)
    qf = q_ref[...].astype(F32) * (FOX_DH ** -0.5)
    lane = lax.broadcasted_iota(I32, cumq_ref.shape, 1)
    ones_rows = (lax.broadcasted_iota(I32, (SUM_ROWS, tq), 0) == 0).astype(F32)
    heads = range(HEADS_PER_QBLOCK)
    qs = [qf[:, e * FOX_DH:(e + 1) * FOX_DH].astype(BF16) for e in heads]
    cqs = [jnp.sum(jnp.where(lane == hp * HEADS_PER_QBLOCK + e, cumq_ref[...], 0.0), axis=1, keepdims=True)
           for e in heads]

    def block(kb, carry, diagonal):
        start = pl.multiple_of(kb * tq, tq)
        new = []
        for e in heads:
            m, acc = carry[e]
            rows = slice(e * FOX_DH, (e + 1) * FOX_DH)
            kt = kt_ref[rows, pl.ds(start, tq)].astype(BF16)
            vt = jnp.concatenate([vt_ref[rows, pl.ds(start, tq)], ones_rows], axis=0).astype(BF16)
            ck = cumk_ref[e:e + 1, pl.ds(start, tq)]
            s = _bdot(qs[e], kt) + cqs[e] - ck
            if diagonal:
                qpos = lax.broadcasted_iota(I32, (tq, 1), 0)
                kpos = lax.broadcasted_iota(I32, (1, tq), 1)
                s = jnp.where(kpos <= qpos, s, NEG_INF)
            m_new = jnp.maximum(m, jnp.max(s, axis=1, keepdims=True))
            a = jnp.exp(m - m_new)
            p = jnp.exp(s - m_new)
            new.append((m_new, a * acc + lax.dot_general(p.astype(BF16), vt, _NT, preferred_element_type=F32)))
        return tuple(new)

    init = tuple((jnp.full((tq, 1), -jnp.inf, F32), jnp.zeros((tq, FOX_DH + SUM_ROWS), F32)) for _ in heads)
    carry = lax.fori_loop(0, qi, lambda kb, c: block(kb, c, False), init)
    final = block(qi, carry, True)
    o_ref[...] = jnp.concatenate([acc[:, :FOX_DH] / acc[:, FOX_DH:FOX_DH + 1] for _, acc in final],
                                 axis=1).astype(o_ref.dtype)


def _fox_prompt(q, kt, vt, cum_tok, cum_row, batch, tq):
    t, fd = q.shape
    l = t // batch
    nq = l // tq
    n_pairs = fd // LANES
    kern = functools.partial(_fox_prompt_kernel, tq=tq)
    return pl.pallas_call(
        kern,
        grid=(batch, n_pairs, nq),
        in_specs=[pl.BlockSpec((tq, LANES), lambda bi, hp, qi: (bi * nq + qi, hp)),
                  pl.BlockSpec((None, LANES, l), lambda bi, hp, qi: (bi, hp, 0)),
                  pl.BlockSpec((None, LANES, l), lambda bi, hp, qi: (bi, hp, 0)),
                  pl.BlockSpec((tq, LANES), lambda bi, hp, qi: (bi * nq + qi, 0)),
                  pl.BlockSpec((None, None, HEADS_PER_QBLOCK, l), lambda bi, hp, qi: (bi, hp, 0, 0))],
        out_specs=pl.BlockSpec((tq, LANES), lambda bi, hp, qi: (bi * nq + qi, hp)),
        out_shape=jax.ShapeDtypeStruct((t, fd), BF16),
        compiler_params=_params(("parallel", "parallel", "arbitrary")),
        name="fox_prompt",
    )(q, kt, vt, cum_tok, cum_row)


N_QROWS = 128
PAGES_PER_STEP = 16
TAIL_ONES_ROW0 = 3 * FOX_HEADS


def _fox_sample_kernel(pt_ref, q_ref, lfo_ref, kto_ref, vto_ref, *rest, n_steps, dec_seq):
    del pt_ref
    cache_refs = rest[:3 * PAGES_PER_STEP]
    o_ref = rest[3 * PAGES_PER_STEP]
    qbd_ref, qtail_ref, m_ref, l_ref, acc_ref, carry_ref = rest[3 * PAGES_PER_STEP + 1:]
    step = pl.program_id(1)
    n_iota = lax.broadcasted_iota(I32, (N_QROWS, 1), 0)
    row_head = n_iota // dec_seq
    row_q = n_iota % dec_seq
    pos = lax.broadcasted_iota(I32, (1, PAGE_SIZE), 1)
    r = lax.broadcasted_iota(I32, (PAGE_SIZE, PAGE_SIZE), 0)
    c = lax.broadcasted_iota(I32, (PAGE_SIZE, PAGE_SIZE), 1)

    def lane_sums(x_t, sel):
        hi, mid, lo = _split3(x_t)
        return (_bdot(lo, sel) + _bdot(mid, sel)) + _bdot(hi, sel)

    def bias_rows(key_bias_t):
        hi, mid, lo = _split3(key_bias_t)
        ones_rows = (lax.broadcasted_iota(I32, (SUBLANES, PAGE_SIZE), 0) < 3).astype(F32)
        return jnp.concatenate(
            [hi.astype(F32), mid.astype(F32), lo.astype(F32), ones_rows,
             jnp.zeros((LANES - TAIL_ONES_ROW0 - SUBLANES, PAGE_SIZE), F32)], axis=0).astype(BF16)

    def attend(kts, vts, key_biases_t, valid):
        kt = jnp.concatenate([x.astype(BF16) for x in kts], axis=1)
        vt = jnp.concatenate([x.astype(BF16) for x in vts], axis=1)
        ktail = jnp.concatenate([bias_rows(x) for x in key_biases_t], axis=1)
        s = _bdot(qbd_ref[...], kt) + _bdot(qtail_ref[...], ktail)
        if valid is not None:
            s = jnp.where(valid, s, NEG_INF)
        m_old = m_ref[...]
        m_new = jnp.maximum(m_old, jnp.max(s, axis=1, keepdims=True))
        a = jnp.exp(m_old - m_new)
        p = jnp.exp(s - m_new)
        l_ref[...] = a * l_ref[...] + jnp.sum(p, axis=1, keepdims=True)
        acc_ref[...] = a * acc_ref[...] + lax.dot_general(
            p.astype(BF16), vt, _NT, preferred_element_type=F32)
        m_ref[...] = m_new

    @pl.when(step == 0)
    def _own_rows():
        q = q_ref[0] * (FOX_DH ** -0.5)
        qt = jnp.concatenate([q] * FOX_HEADS, axis=0)
        rr = lax.broadcasted_iota(I32, qt.shape, 0)
        cc = lax.broadcasted_iota(I32, qt.shape, 1)
        qbd_ref[...] = jnp.where(cc // FOX_DH == rr // dec_seq, qt, 0.0).astype(BF16)
        prefix_t = lane_sums(lfo_ref[0], (r <= c).astype(BF16))
        rep = (lax.broadcasted_iota(I32, (N_QROWS, FOX_HEADS), 1)
               == lax.broadcasted_iota(I32, (N_QROWS, FOX_HEADS), 0) // dec_seq).astype(BF16)
        hi, mid, lo = _split3(prefix_t)
        by_row = (_bdot(rep, lo) + _bdot(rep, mid)) + _bdot(rep, hi)
        pre_col = jnp.sum(jnp.where(pos == row_q, by_row, 0.0), axis=1, keepdims=True)
        hi, mid, lo = (x.astype(F32) for x in _split3(pre_col))
        lane = lax.broadcasted_iota(I32, (N_QROWS, LANES), 1)
        tail = jnp.where(jnp.logical_and(lane < TAIL_ONES_ROW0, lane % FOX_HEADS == row_head), 1.0, 0.0)
        tail = jnp.where(lane == TAIL_ONES_ROW0, hi, tail)
        tail = jnp.where(lane == TAIL_ONES_ROW0 + 1, mid, tail)
        tail = jnp.where(lane == TAIL_ONES_ROW0 + 2, lo, tail)
        qtail_ref[...] = tail.astype(BF16)
        m_ref[...] = jnp.full(m_ref.shape, -jnp.inf, F32)
        l_ref[...] = jnp.zeros(l_ref.shape, F32)
        acc_ref[...] = jnp.zeros(acc_ref.shape, F32)
        carry_ref[...] = jnp.zeros(carry_ref.shape, F32)
        valid = jnp.logical_and(pos <= row_q, pos < dec_seq)
        attend([kto_ref[0]], [vto_ref[0]], [-prefix_t], valid)

    @pl.when(step > 0)
    def _cache_pages():
        after = (r > c).astype(BF16)
        carry = carry_ref[...]
        kts, vts, biases = [], [], []
        for u in range(PAGES_PER_STEP):
            kt_ref, vt_ref, lf_ref = cache_refs[3 * u:3 * u + 3]
            lf_t = lf_ref[0]
            kts.append(kt_ref[0])
            vts.append(vt_ref[0])
            biases.append(lane_sums(lf_t, after) + carry)
            carry = carry + jnp.sum(lf_t, axis=1, keepdims=True)
        attend(kts, vts, biases, None)
        carry_ref[...] = carry

    @pl.when(step == n_steps - 1)
    def _finish():
        col_head = lax.broadcasted_iota(I32, (dec_seq, D_MODEL), 1) // FOX_DH
        out = jnp.zeros((dec_seq, D_MODEL), F32)
        for h in range(FOX_HEADS):
            rows = slice(h * dec_seq, (h + 1) * dec_seq)
            out = out + jnp.where(col_head == h, acc_ref[rows, :] / l_ref[rows, :], 0.0)
        o_ref[0] = out


def _fox_sample(q, lft_own, kt_own, vt_own, kt_cache, vt_cache, lft_cache, page_table):
    db, dec_seq, _ = q.shape
    n_pages = page_table.shape[1]
    n_steps = 1 + n_pages // PAGES_PER_STEP
    fd = FOX_HEADS * FOX_DH
    kern = functools.partial(_fox_sample_kernel, n_steps=n_steps, dec_seq=dec_seq)

    def page(u):
        return lambda b, s, pt: (pt[b, n_pages - 1 - (jnp.maximum(s, 1) - 1) * PAGES_PER_STEP - u], 0, 0)

    cache_specs = []
    cache_args = []
    for u in range(PAGES_PER_STEP):
        cache_specs += [pl.BlockSpec((1, fd, PAGE_SIZE), page(u)),
                        pl.BlockSpec((1, fd, PAGE_SIZE), page(u)),
                        pl.BlockSpec((1, FOX_HEADS, PAGE_SIZE), page(u))]
        cache_args += [kt_cache, vt_cache, lft_cache]
    grid_spec = pltpu.PrefetchScalarGridSpec(
        num_scalar_prefetch=1,
        grid=(db, n_steps),
        in_specs=[pl.BlockSpec((1, dec_seq, D_MODEL), lambda b, s, pt: (b, 0, 0)),
                  pl.BlockSpec((1, FOX_HEADS, PAGE_SIZE), lambda b, s, pt: (b, 0, 0)),
                  pl.BlockSpec((1, fd, PAGE_SIZE), lambda b, s, pt: (b, 0, 0)),
                  pl.BlockSpec((1, fd, PAGE_SIZE), lambda b, s, pt: (b, 0, 0))] + cache_specs,
        out_specs=pl.BlockSpec((1, dec_seq, D_MODEL), lambda b, s, pt: (b, 0, 0)),
        scratch_shapes=[pltpu.VMEM((N_QROWS, fd), BF16),
                        pltpu.VMEM((N_QROWS, LANES), BF16),
                        pltpu.VMEM((N_QROWS, 1), F32),
                        pltpu.VMEM((N_QROWS, 1), F32),
                        pltpu.VMEM((N_QROWS, fd), F32),
                        pltpu.VMEM((FOX_HEADS, 1), F32)],
    )
    return pl.pallas_call(
        kern,
        grid_spec=grid_spec,
        out_shape=jax.ShapeDtypeStruct((db, dec_seq, D_MODEL), F32),
        compiler_params=_params(("parallel", "arbitrary")),
        name="fox_sample",
    )(page_table, q, lft_own, kt_own, vt_own, *cache_args)


def _moe_params(layer, w_router, b_router, w_gate, w_up, w_down, ws_gate, ws_up, ws_down):
    wr_t = w_router[layer].T
    wr_hi = wr_t.astype(BF16)
    wr_lo = (wr_t - wr_hi.astype(F32)).astype(BF16)
    return dict(
        wr_hi=wr_hi, wr_lo=wr_lo, b_col=b_router[layer].reshape(N_EXPERTS, 1),
        layer=layer, wg=w_gate, wu=w_up, wd=w_down,
        wsgu=jnp.concatenate([ws_gate[layer], ws_up[layer]], axis=1).astype(BF16),
        wsd=ws_down[layer].astype(BF16))


def kernel(x_prompt, x_sample, state_ret, cache_k, cache_v, cache_logf, page_table, ret_w_in, ret_w_out,
           fox_w_kvf, fox_b_f, fox_w_q, fox_w_out, ln_g, ln_b, moe_w_router, moe_b_router, moe_w_gate,
           moe_w_up, moe_w_down, moe_ws_gate, moe_ws_up, moe_ws_down):
    bp, lp, d = x_prompt.shape
    bs, ls, _ = x_sample.shape
    tp, ts = bp * lp, bs * ls
    assert fox_heads_rows(ls) == N_QROWS
    xp = x_prompt.reshape(tp, d)
    xs = x_sample.reshape(ts, d)
    hq = RET_HEADS * RET_DK
    hv = RET_HEADS * RET_DV
    fd = FOX_HEADS * FOX_DH
    tm_p, tn_p = PROMPT_TILE_ROWS, PROJ_TILE_COLS
    tc_p = MOE_TOKEN_TILE

    wg_all, wu_all, wd_all = moe_w_gate.astype(BF16), moe_w_up.astype(BF16), moe_w_down.astype(BF16)
    moe = [_moe_params(layer, moe_w_router, moe_b_router, wg_all, wu_all, wd_all,
                       moe_ws_gate, moe_ws_up, moe_ws_down) for layer in range(DEPTH)]

    w_in = ret_w_in[0].astype(BF16)
    w_out = ret_w_out[0].astype(BF16)
    cos_p, sin_p = _rope_tables(jnp.arange(lp))
    cos_s, sin_s = _rope_tables(PAST_LEN + jnp.arange(ls))
    cos_s, sin_s = jnp.tile(cos_s, (bs, 1)), jnp.tile(sin_s, (bs, 1))

    qk_p = _qk_rope(xp, w_in, cos_p, sin_p, tm_p)
    v_p = _matmul(xp, w_in, 2 * hq, hv, BF16, tm_p, tn_p)
    g_p = _matmul(xp, w_in, 2 * hq + hv, hv, F32, tm_p, tn_p)
    o_p, st_p = _retention(qk_p, v_p, g_p, jnp.zeros((bp, RET_HEADS, RET_DK, RET_DV), F32),
                           _retention_decays(min(RET_CHUNK, lp)), bp, lp, BF16)
    xp = _mm_res_ln(o_p, w_out, xp, ln_g[0, 0], ln_b[0, 0], tm_p)

    qk_s = _qk_rope(xs, w_in, cos_s, sin_s, ts)
    v_s = _matmul(xs, w_in, 2 * hq, hv, F32, ts, tn_p)
    g_s = _matmul(xs, w_in, 2 * hq + hv, hv, F32, ts, tn_p)
    o_s, st_s = _retention(qk_s, v_s, g_s, state_ret[0], _retention_decays(min(RET_CHUNK, ls)), bs, ls, F32)
    xs = _mm_res_ln(o_s, w_out, xs, ln_g[0, 0], ln_b[0, 0], ts)

    xp = _moe_ln(xp, moe[0], ln_g[0, 1], ln_b[0, 1], tc_p)
    xs = _moe_ln(xs, moe[0], ln_g[0, 1], ln_b[0, 1], ts)

    w_kvf = fox_w_kvf.astype(BF16)
    w_kv_t = w_kvf[:, :2 * fd].T
    w_f_pad = jnp.zeros((d, LANES), BF16).at[:, :FOX_HEADS].set(w_kvf[:, 2 * fd:])
    b_f_pad = jnp.zeros((1, LANES), F32).at[0, :FOX_HEADS].set(fox_b_f)
    kt_p, vt_p = _matmul_t_pair(xp, w_kv_t, bp, tm_p)
    lf_p = _logf_proj(xp, w_f_pad, b_f_pad, tm_p)
    per_seq = lambda a_t: a_t[0].reshape(fd, bs, ls).transpose(1, 0, 2)
    kt_s, vt_s = (per_seq(a) for a in _matmul_t_pair(xs, w_kv_t, 1, ts))
    lf_s = _logf_proj(xs, w_f_pad, b_f_pad, ts)

    as_blhd = lambda a_t, b, l: a_t.reshape(b, FOX_HEADS, FOX_DH, l).transpose(0, 3, 1, 2)
    k_p, v_p4 = as_blhd(kt_p, bp, lp), as_blhd(vt_p, bp, lp)
    k_s, v_s4 = as_blhd(kt_s, bs, ls), as_blhd(vt_s, bs, ls)

    w_q = fox_w_q[0].astype(BF16)
    w_o = fox_w_out[0].astype(BF16)
    q_p = _matmul(xp, w_q, 0, fd, BF16, tm_p, tn_p)
    cum_p = _cumsum_rows(lf_p, bp, lp)
    cum_row = cum_p[:, :FOX_HEADS].reshape(bp, lp, FOX_HEADS).transpose(0, 2, 1).reshape(
        bp, FOX_HEADS // HEADS_PER_QBLOCK, HEADS_PER_QBLOCK, lp)
    att_p = _fox_prompt(q_p, kt_p, vt_p, cum_p, cum_row, bp, ATTN_TILE)
    xp = _mm_res_ln(att_p, w_o, xp, ln_g[1, 0], ln_b[1, 0], tm_p)

    q_s = _matmul(xs, w_q, 0, fd, F32, ts, tn_p)
    n_pool = cache_k.shape[0]
    kt_cache = cache_k.transpose(0, 2, 3, 1).reshape(n_pool, fd, PAGE_SIZE)
    vt_cache = cache_v.transpose(0, 2, 3, 1).reshape(n_pool, fd, PAGE_SIZE)
    lft_cache = cache_logf.transpose(0, 2, 1)
    pad_pos = lambda a_t: jnp.pad(a_t, ((0, 0), (0, 0), (0, PAGE_SIZE - ls)))
    lft_own = pad_pos(lf_s[:, :FOX_HEADS].reshape(bs, ls, FOX_HEADS).transpose(0, 2, 1))
    att_s = _fox_sample(q_s.reshape(bs, ls, d), lft_own, pad_pos(kt_s), pad_pos(vt_s),
                        kt_cache, vt_cache, lft_cache, page_table)
    xs = _mm_res_ln(att_s.reshape(ts, fd), w_o, xs, ln_g[1, 0], ln_b[1, 0], ts)

    xp = _moe_ln(xp, moe[1], ln_g[1, 1], ln_b[1, 1], tc_p)
    xs = _moe_ln(xs, moe[1], ln_g[1, 1], ln_b[1, 1], ts)

    return (xp.reshape(bp, lp, d), xs.reshape(bs, ls, d), st_p[None], st_s[None],
            k_p, v_p4, lf_p[:, :FOX_HEADS].reshape(bp, lp, FOX_HEADS),
            k_s, v_s4, lf_s[:, :FOX_HEADS].reshape(bs, ls, FOX_HEADS))


def fox_heads_rows(dec_seq):
    return FOX_HEADS * dec_seq
```

```python
import functools

import jax
import jax.numpy as jnp
from jax import lax
from jax.experimental import pallas as pl
from jax.experimental.pallas import tpu as pltpu

F32 = jnp.float32
BF16 = jnp.bfloat16
I32 = jnp.int32

D_MODEL = 1024
DEPTH = 2
PAST_LEN = 8192
PAGE_SIZE = 128
N_A_LAYERS = DEPTH // 2
RET_HEADS = 4
RET_DK = D_MODEL // RET_HEADS
RET_DV = 2 * RET_DK
RET_CHUNK = 128
ROPE_BASE = 10000.0
FOX_HEADS = 16
FOX_DH = D_MODEL // FOX_HEADS
NEG_INF = -1e30
N_EXPERTS = 64
TOPK = 8
N_GROUPS = 8
GROUP_SIZE = N_EXPERTS // N_GROUPS
TOPK_GROUPS = 4
FF_EXPERT = D_MODEL // 4
ROUTED_SCALE = 2.5
DEEPNORM_ALPHA = (2.0 * DEPTH) ** 0.25
LN_EPS = 1e-5
GN_EPS = 1e-5

LANES = 128
SUBLANES = 8
ROW_VREGS = D_MODEL // LANES
MXU_COLS = 256
EXP_BLOCK = 160
EXPERTS_PER_STEP = 4
VMEM_LIMIT_BIG = 60 * 1024 * 1024
VMEM_LIMIT = 48 * 1024 * 1024

PROMPT_TILE_ROWS = 1024
PROJ_TILE_COLS = 1024
MOE_TOKEN_TILE = 1024
ATTN_TILE = 1024

_NT = (((1,), (1,)), ((), ()))
_TN = (((0,), (0,)), ((), ()))


def _params(sem, vmem=VMEM_LIMIT):
    return pltpu.CompilerParams(dimension_semantics=sem, vmem_limit_bytes=vmem)


def _bdot(a, b):
    return jnp.dot(a, b, preferred_element_type=F32)


def _layer_norm_rows(z, g, b):
    mu = jnp.mean(z, axis=-1, keepdims=True)
    zc = z - mu
    var = jnp.mean(zc * zc, axis=-1, keepdims=True)
    return zc * lax.rsqrt(var + LN_EPS) * g + b


def _silu(x):
    return x * jax.nn.sigmoid(x)


def _split3(x):
    hi = x.astype(BF16)
    r1 = x - hi.astype(F32)
    mid = r1.astype(BF16)
    lo = (r1 - mid.astype(F32)).astype(BF16)
    return hi, mid, lo


def _mm_kernel(x_ref, w_ref, o_ref):
    o_ref[...] = _bdot(x_ref[...].astype(BF16), w_ref[...]).astype(o_ref.dtype)


def _matmul(x, w, col0, n, out_dtype, tm, tn):
    m, k = x.shape
    j0 = col0 // tn
    return pl.pallas_call(
        _mm_kernel,
        grid=(m // tm, n // tn),
        in_specs=[pl.BlockSpec((tm, k), lambda i, j: (i, 0)),
                  pl.BlockSpec((k, tn), lambda i, j: (0, j + j0))],
        out_specs=pl.BlockSpec((tm, tn), lambda i, j: (i, j)),
        out_shape=jax.ShapeDtypeStruct((m, n), out_dtype),
        compiler_params=_params(("parallel", "arbitrary")),
        name="proj",
    )(x, w)


def _mm_t_pair_kernel(x_ref, wt_ref, o0_ref, o1_ref):
    j = pl.program_id(1)
    res = lax.dot_general(wt_ref[...], x_ref[...].astype(BF16), _NT, preferred_element_type=F32)

    @pl.when(j == 0)
    def _():
        o0_ref[...] = res

    @pl.when(j == 1)
    def _():
        o1_ref[...] = res


def _matmul_t_pair(x, w_t, batch, tm):
    m, k = x.shape
    n = w_t.shape[0] // 2
    seq = m // batch
    nper = seq // tm
    out_spec = pl.BlockSpec((None, n, tm), lambda i, j: (i // nper, 0, i % nper))
    return pl.pallas_call(
        _mm_t_pair_kernel,
        grid=(m // tm, 2),
        in_specs=[pl.BlockSpec((tm, k), lambda i, j: (i, 0)),
                  pl.BlockSpec((n, k), lambda i, j: (j, 0))],
        out_specs=[out_spec, out_spec],
        out_shape=[jax.ShapeDtypeStruct((batch, n, seq), F32)] * 2,
        compiler_params=_params(("parallel", "arbitrary")),
        name="proj_t",
    )(x, w_t)


def _mm_rope_kernel(x_ref, w_ref, cos_ref, sin_ref, o_ref, *, k_scale):
    j = pl.program_id(1)
    acc = _bdot(x_ref[...].astype(BF16), w_ref[...])
    half = RET_DK // 2
    c = cos_ref[...]
    s = sin_ref[...]
    sc = jnp.where(j == 0, 1.0, k_scale).astype(F32)
    for h in range(RET_HEADS):
        lo = h * RET_DK
        x1 = acc[:, lo:lo + half]
        x2 = acc[:, lo + half:lo + RET_DK]
        o_ref[:, lo:lo + half] = (x1 * c - x2 * s) * sc
        o_ref[:, lo + half:lo + RET_DK] = (x2 * c + x1 * s) * sc


def _qk_rope(x, w, cos_rows, sin_rows, tm):
    m, k = x.shape
    hq = RET_HEADS * RET_DK
    nper = cos_rows.shape[0] // tm
    kern = functools.partial(_mm_rope_kernel, k_scale=RET_DK ** -0.5)
    return pl.pallas_call(
        kern,
        grid=(m // tm, 2),
        in_specs=[pl.BlockSpec((tm, k), lambda i, j: (i, 0)),
                  pl.BlockSpec((k, hq), lambda i, j: (0, j)),
                  pl.BlockSpec((tm, RET_DK // 2), lambda i, j: (i % nper, 0)),
                  pl.BlockSpec((tm, RET_DK // 2), lambda i, j: (i % nper, 0))],
        out_specs=pl.BlockSpec((tm, hq), lambda i, j: (i, j)),
        out_shape=jax.ShapeDtypeStruct((m, 2 * hq), F32),
        compiler_params=_params(("parallel", "arbitrary")),
        name="qk_rope",
    )(x, w, cos_rows, sin_rows)


def _mm_res_ln_kernel(a_ref, w_ref, x_ref, g_ref, b_ref, o_ref):
    mix = _bdot(a_ref[...].astype(BF16), w_ref[...])
    z = DEEPNORM_ALPHA * x_ref[...] + mix
    o_ref[...] = _layer_norm_rows(z, g_ref[...], b_ref[...])


def _mm_res_ln(a, w, x, g, b, tm):
    m, k = a.shape
    d = w.shape[1]
    return pl.pallas_call(
        _mm_res_ln_kernel,
        grid=(m // tm,),
        in_specs=[pl.BlockSpec((tm, k), lambda i: (i, 0)),
                  pl.BlockSpec((k, d), lambda i: (0, 0)),
                  pl.BlockSpec((tm, d), lambda i: (i, 0)),
                  pl.BlockSpec((1, d), lambda i: (0, 0)),
                  pl.BlockSpec((1, d), lambda i: (0, 0))],
        out_specs=pl.BlockSpec((tm, d), lambda i: (i, 0)),
        out_shape=jax.ShapeDtypeStruct((m, d), F32),
        compiler_params=_params(("parallel",)),
        name="mix_out_ln",
    )(a, w, x, g.reshape(1, d), b.reshape(1, d))


def _mm_logsig_kernel(x_ref, w_ref, b_ref, o_ref):
    z = _bdot(x_ref[...].astype(BF16), w_ref[...]) + b_ref[...]
    o_ref[...] = -(jnp.maximum(-z, 0.0) + jnp.log1p(jnp.exp(-jnp.abs(z))))


def _logf_proj(x, w_pad, b_pad, tm):
    m, k = x.shape
    return pl.pallas_call(
        _mm_logsig_kernel,
        grid=(m // tm,),
        in_specs=[pl.BlockSpec((tm, k), lambda i: (i, 0)),
                  pl.BlockSpec((k, LANES), lambda i: (0, 0)),
                  pl.BlockSpec((1, LANES), lambda i: (0, 0))],
        out_specs=pl.BlockSpec((tm, LANES), lambda i: (i, 0)),
        out_shape=jax.ShapeDtypeStruct((m, LANES), F32),
        compiler_params=_params(("parallel",)),
        name="logf_proj",
    )(x, w_pad, b_pad)


def _ret_kernel(q_ref, k_ref, v_ref, g_ref, s0_ref, din_ref, dq_ref, dk_ref, dc_ref,
                o_ref, sout_ref, state_ref):
    c = pl.program_id(1)

    @pl.when(c == 0)
    def _():
        state_ref[...] = s0_ref[...]

    for h in range(RET_HEADS):
        qcols = slice(h * RET_DK, (h + 1) * RET_DK)
        vcols = slice(h * RET_DV, (h + 1) * RET_DV)
        q = q_ref[:, qcols].astype(BF16)
        k = k_ref[:, qcols]
        v = v_ref[:, vcols].astype(BF16)
        s_prev = state_ref[h]
        scores = lax.dot_general(q, k.astype(BF16), _NT, preferred_element_type=F32) * din_ref[h]
        o = _bdot(scores.astype(BF16), v) + _bdot(q, s_prev.astype(BF16)) * dq_ref[h]
        kd = (k * dk_ref[h]).astype(BF16)
        s_new = s_prev * dc_ref[h] + lax.dot_general(kd, v, _TN, preferred_element_type=F32)
        state_ref[h] = s_new

        mu = jnp.mean(o, axis=-1, keepdims=True)
        oc = o - mu
        var = jnp.mean(oc * oc, axis=-1, keepdims=True)
        on = oc * lax.rsqrt(var + GN_EPS)
        o_ref[:, vcols] = (_silu(g_ref[:, vcols]) * on).astype(o_ref.dtype)

    @pl.when(c == pl.num_programs(1) - 1)
    def _():
        sout_ref[...] = state_ref[...]


def _retention(qk, v, g, s0, decays, batch, seq, out_dtype):
    chunk = min(RET_CHUNK, seq)
    nc = seq // chunk
    din, dq, dk, dc = decays
    h_ = RET_HEADS
    hq, hv = h_ * RET_DK, h_ * RET_DV
    whole = lambda a: pl.BlockSpec(a.shape, lambda b, c: (0,) * a.ndim)
    return pl.pallas_call(
        _ret_kernel,
        grid=(batch, nc),
        in_specs=[pl.BlockSpec((chunk, hq), lambda b, c: (b * nc + c, 0)),
                  pl.BlockSpec((chunk, hq), lambda b, c: (b * nc + c, 1)),
                  pl.BlockSpec((chunk, hv), lambda b, c: (b * nc + c, 0)),
                  pl.BlockSpec((chunk, hv), lambda b, c: (b * nc + c, 0)),
                  pl.BlockSpec((None, h_, RET_DK, RET_DV), lambda b, c: (b, 0, 0, 0)),
                  whole(din), whole(dq), whole(dk), whole(dc)],
        out_specs=[pl.BlockSpec((chunk, hv), lambda b, c: (b * nc + c, 0)),
                   pl.BlockSpec((None, h_, RET_DK, RET_DV), lambda b, c: (b, 0, 0, 0))],
        out_shape=[jax.ShapeDtypeStruct((batch * seq, hv), out_dtype),
                   jax.ShapeDtypeStruct((batch, h_, RET_DK, RET_DV), F32)],
        scratch_shapes=[pltpu.VMEM((h_, RET_DK, RET_DV), F32)],
        compiler_params=_params(("parallel", "arbitrary")),
        name="retention",
    )(qk, qk, v, g, s0, din, dq, dk, dc)


def _retention_decays(chunk):
    h = jnp.arange(RET_HEADS, dtype=F32)
    log_g = jnp.log1p(-(2.0 ** (-5.0 - h)))
    idx = jnp.arange(chunk, dtype=F32)
    rel = idx[:, None] - idx[None, :]
    din = jnp.where(rel >= 0, jnp.exp(log_g[:, None, None] * jnp.maximum(rel, 0.0)), 0.0)
    dq = jnp.exp(log_g[:, None] * (idx[None, :] + 1.0))[:, :, None]
    dk = jnp.exp(log_g[:, None] * (chunk - 1.0 - idx[None, :]))[:, :, None]
    dc = jnp.exp(log_g * chunk)[:, None, None]
    return din, dq, dk, dc


def _rope_tables(pos):
    half = RET_DK // 2
    inv_freq = ROPE_BASE ** (-jnp.arange(half, dtype=F32) / half)
    ang = pos.astype(F32)[:, None] * inv_freq[None, :]
    return jnp.cos(ang), jnp.sin(ang)


def _router_kernel(x_ref, wh_ref, wl_ref, b_ref, slot_ref, wgt_ref, off_ref, cnt_ref):
    tc = x_ref.shape[0]
    x = x_ref[...]
    xh = x.astype(BF16)
    xl = (x - xh.astype(F32)).astype(BF16)
    wh = wh_ref[...]
    wl = wl_ref[...]
    logits = (lax.dot_general(wh, xh, _NT, preferred_element_type=F32)
              + lax.dot_general(wh, xl, _NT, preferred_element_type=F32)
              + lax.dot_general(wl, xh, _NT, preferred_element_type=F32))
    scores = jax.nn.sigmoid(logits)
    biased = scores + b_ref[...]

    jj = lax.broadcasted_iota(I32, (GROUP_SIZE, tc), 0)
    groups = [biased[g * GROUP_SIZE:(g + 1) * GROUP_SIZE, :] for g in range(N_GROUPS)]
    gscore = []
    for rows in groups:
        m1 = jnp.max(rows, axis=0, keepdims=True)
        j1 = jnp.min(jnp.where(rows == m1, jj, GROUP_SIZE), axis=0, keepdims=True)
        m2 = jnp.max(jnp.where(jj == j1, -jnp.inf, rows), axis=0, keepdims=True)
        gscore.append(m1 + m2)
    gsel = [jnp.zeros((1, tc), dtype=jnp.bool_) for _ in range(N_GROUPS)]
    for _ in range(TOPK_GROUPS):
        gm = gscore[0]
        for sc in gscore[1:]:
            gm = jnp.maximum(gm, sc)
        found = jnp.zeros((1, tc), dtype=jnp.bool_)
        for g in range(N_GROUPS):
            hit = jnp.logical_and(gscore[g] == gm, jnp.logical_not(found))
            found = jnp.logical_or(found, hit)
            gsel[g] = jnp.logical_or(gsel[g], hit)
            gscore[g] = jnp.where(hit, -jnp.inf, gscore[g])
    masked = jnp.concatenate(
        [jnp.where(jnp.broadcast_to(gsel[g], groups[g].shape), groups[g], -jnp.inf)
         for g in range(N_GROUPS)], axis=0)
    ei = lax.broadcasted_iota(I32, masked.shape, 0)
    hits = []
    wk = []
    for _ in range(TOPK):
        m = jnp.max(masked, axis=0, keepdims=True)
        first = jnp.min(jnp.where(masked == m, ei, N_EXPERTS), axis=0, keepdims=True)
        hit = ei == first
        hits.append(hit)
        wk.append(jnp.sum(jnp.where(hit, scores, 0.0), axis=0, keepdims=True))
        masked = jnp.where(hit, -jnp.inf, masked)
    wsum = wk[0]
    for w in wk[1:]:
        wsum = wsum + w
    wgt_ref[...] = jnp.concatenate([w / wsum * ROUTED_SCALE for w in wk], axis=0)

    chosen = hits[0]
    for hit in hits[1:]:
        chosen = jnp.logical_or(chosen, hit)
    chosen = chosen.astype(BF16)
    r = lax.broadcasted_iota(I32, (tc, tc), 0)
    c = lax.broadcasted_iota(I32, (tc, tc), 1)
    rank = _bdot(chosen, (r <= c).astype(BF16))
    er = lax.broadcasted_iota(I32, (N_EXPERTS, N_EXPERTS), 0)
    ec = lax.broadcasted_iota(I32, (N_EXPERTS, N_EXPERTS), 1)
    below = _bdot((ec < er).astype(BF16), chosen)
    off = jnp.sum(below, axis=1, keepdims=True)
    cnt = rank[:, tc - 1:tc]
    pos = (off + rank - 1.0) * ROW_VREGS
    slot_ref[...] = jnp.concatenate(
        [jnp.sum(jnp.where(hit, pos, 0.0), axis=0, keepdims=True) for hit in hits], axis=0).astype(I32)
    off_ref[...] = jnp.broadcast_to(off, off_ref.shape).astype(I32)
    cnt_ref[...] = jnp.broadcast_to(cnt, cnt_ref.shape).astype(I32)


def _router(x, wr_hi, wr_lo, b_col, tc):
    t, d = x.shape
    n = t // tc
    return pl.pallas_call(
        _router_kernel,
        grid=(n,),
        in_specs=[pl.BlockSpec((tc, d), lambda i: (i, 0)),
                  pl.BlockSpec((N_EXPERTS, d), lambda i: (0, 0)),
                  pl.BlockSpec((N_EXPERTS, d), lambda i: (0, 0)),
                  pl.BlockSpec((N_EXPERTS, 1), lambda i: (0, 0))],
        out_specs=[pl.BlockSpec((None, TOPK, tc), lambda i: (i, 0, 0)),
                   pl.BlockSpec((None, TOPK, tc), lambda i: (i, 0, 0)),
                   pl.BlockSpec((None, N_EXPERTS, LANES), lambda i: (i, 0, 0)),
                   pl.BlockSpec((None, N_EXPERTS, LANES), lambda i: (i, 0, 0))],
        out_shape=[jax.ShapeDtypeStruct((n, TOPK, tc), I32),
                   jax.ShapeDtypeStruct((n, TOPK, tc), F32),
                   jax.ShapeDtypeStruct((n, N_EXPERTS, LANES), I32),
                   jax.ShapeDtypeStruct((n, N_EXPERTS, LANES), I32)],
        compiler_params=_params(("parallel",)),
        name="router",
    )(x, wr_hi, wr_lo, b_col)


def _experts_kernel(off_ref, cnt_ref, slot_ref, wgt_ref, x_ref, wg_ref, wu_ref, wd_ref,
                    o_ref, sorted_ref):
    tc = x_ref.shape[0]
    ci = pl.program_id(0)
    e = pl.program_id(1)
    tok_per_line = LANES // TOPK

    @pl.when(e == 0)
    def _dispatch():
        for kc in range(ROW_VREGS):
            o_ref[pl.ds(kc, tc, stride=ROW_VREGS), :] = x_ref[:, kc * LANES:(kc + 1) * LANES]
        sorted_ref[pl.ds(TOPK * tc * ROW_VREGS, EXP_BLOCK * ROW_VREGS), :] = jnp.zeros(
            (EXP_BLOCK * ROW_VREGS, LANES), F32)

        def body(i, carry):
            slot_line = slot_ref.at[0, i]
            for u in range(tok_per_line):
                t = i * tok_per_line + u
                row = o_ref[pl.ds(pl.multiple_of(t * ROW_VREGS, ROW_VREGS), ROW_VREGS), :]
                for k in range(TOPK):
                    s = slot_line[u * TOPK + k]
                    sorted_ref[pl.ds(pl.multiple_of(s, ROW_VREGS), ROW_VREGS), :] = row
            return carry

        lax.fori_loop(0, tc // tok_per_line, body, 0)

    base = ci * N_EXPERTS + e * EXPERTS_PER_STEP
    offs = [off_ref[base + j] for j in range(EXPERTS_PER_STEP)]
    cnts = [cnt_ref[base + j] for j in range(EXPERTS_PER_STEP)]
    nblks = [lax.div(cnt + (EXP_BLOCK - 1), EXP_BLOCK) for cnt in cnts]
    nmax = nblks[0]
    for nb in nblks[1:]:
        nmax = jnp.maximum(nmax, nb)
    rowid = lax.broadcasted_iota(I32, (EXP_BLOCK, 1), 0)

    def ffn_blocks(i, carry):
        loaded = []
        for j in range(EXPERTS_PER_STEP):
            active = i < nblks[j]
            row0 = jnp.where(active, offs[j] + i * EXP_BLOCK, TOPK * tc)
            nvalid = jnp.where(active, cnts[j] - i * EXP_BLOCK, 0)
            r0 = row0 * ROW_VREGS
            xbb = jnp.concatenate(
                [sorted_ref[pl.ds(r0 + kc, EXP_BLOCK, stride=ROW_VREGS), :].astype(BF16)
                 for kc in range(ROW_VREGS)], axis=1)
            loaded.append((r0, nvalid, xbb))
        hidden = [(_silu(_bdot(xbb, wg_ref[j])) * _bdot(xbb, wu_ref[j])).astype(BF16)
                  for j, (_, _, xbb) in enumerate(loaded)]
        for j, (r0, nvalid, xbb) in enumerate(loaded):
            for c0 in range(0, D_MODEL, MXU_COLS):
                y = _bdot(hidden[j], wd_ref[j, :, c0:c0 + MXU_COLS])
                y = jnp.where(rowid < nvalid, y, xbb[:, c0:c0 + MXU_COLS].astype(F32))
                for kc in range(c0 // LANES, (c0 + MXU_COLS) // LANES):
                    sorted_ref[pl.ds(r0 + kc, EXP_BLOCK, stride=ROW_VREGS), :] = (
                        y[:, kc * LANES - c0:(kc + 1) * LANES - c0])
        return carry

    lax.fori_loop(0, nmax, ffn_blocks, 0)

    @pl.when(e == pl.num_programs(1) - 1)
    def _combine():
        def body(i, carry):
            slot_line = slot_ref.at[0, i]
            wgt_line = wgt_ref.at[0, i]
            for u in range(tok_per_line):
                t = i * tok_per_line + u
                acc = None
                for k in range(TOPK):
                    s = slot_line[u * TOPK + k]
                    term = wgt_line[u * TOPK + k] * sorted_ref[
                        pl.ds(pl.multiple_of(s, ROW_VREGS), ROW_VREGS), :]
                    acc = term if acc is None else acc + term
                o_ref[pl.ds(pl.multiple_of(t * ROW_VREGS, ROW_VREGS), ROW_VREGS), :] = acc
            return carry

        lax.fori_loop(0, tc // tok_per_line, body, 0)


def _routed_experts(x, slot, wgt, seg_off, seg_cnt, wg, wu, wd, layer, tc):
    t, d = x.shape
    n = t // tc
    sorted_rows = TOPK * tc + EXP_BLOCK
    grid_spec = pltpu.PrefetchScalarGridSpec(
        num_scalar_prefetch=2,
        grid=(n, N_EXPERTS // EXPERTS_PER_STEP),
        in_specs=[pl.BlockSpec((1,) + slot.shape[1:], lambda c, e, *_: (c, 0, 0), memory_space=pltpu.SMEM),
                  pl.BlockSpec((1,) + wgt.shape[1:], lambda c, e, *_: (c, 0, 0), memory_space=pltpu.SMEM),
                  pl.BlockSpec((tc, d), lambda c, e, *_: (c, 0), pipeline_mode=pl.Buffered(1)),
                  pl.BlockSpec((None, EXPERTS_PER_STEP, d, FF_EXPERT), lambda c, e, *_: (layer, e, 0, 0)),
                  pl.BlockSpec((None, EXPERTS_PER_STEP, d, FF_EXPERT), lambda c, e, *_: (layer, e, 0, 0)),
                  pl.BlockSpec((None, EXPERTS_PER_STEP, FF_EXPERT, d), lambda c, e, *_: (layer, e, 0, 0))],
        out_specs=pl.BlockSpec((tc * ROW_VREGS, LANES), lambda c, e, *_: (c, 0), pipeline_mode=pl.Buffered(1)),
        scratch_shapes=[pltpu.VMEM((sorted_rows * ROW_VREGS, LANES), F32)],
    )
    return pl.pallas_call(
        _experts_kernel,
        grid_spec=grid_spec,
        out_shape=jax.ShapeDtypeStruct((t * ROW_VREGS, LANES), F32),
        compiler_params=_params(("arbitrary", "arbitrary"), VMEM_LIMIT_BIG),
        name="routed_experts",
    )(seg_off, seg_cnt, slot, wgt, x, wg, wu, wd)


def _shared_ln_kernel(x_ref, r_ref, wgu_ref, wd_ref, g_ref, b_ref, o_ref):
    tm = x_ref.shape[0]
    x = x_ref[...]
    gu = _bdot(x.astype(BF16), wgu_ref[...])
    ff = wgu_ref.shape[1] // 2
    h = _silu(gu[:, :ff]) * gu[:, ff:]
    shared = _bdot(h.astype(BF16), wd_ref[...])
    routed = jnp.concatenate(
        [r_ref[pl.ds(kc, tm, stride=ROW_VREGS), :] for kc in range(ROW_VREGS)], axis=1)
    z = DEEPNORM_ALPHA * x + (shared + routed)
    o_ref[...] = _layer_norm_rows(z, g_ref[...], b_ref[...])


def _shared_ln(x, routed_rows, wgu, wd, g, b, tm):
    t, d = x.shape
    return pl.pallas_call(
        _shared_ln_kernel,
        grid=(t // tm,),
        in_specs=[pl.BlockSpec((tm, d), lambda i: (i, 0)),
                  pl.BlockSpec((tm * ROW_VREGS, LANES), lambda i: (i, 0)),
                  pl.BlockSpec(wgu.shape, lambda i: (0, 0)),
                  pl.BlockSpec(wd.shape, lambda i: (0, 0)),
                  pl.BlockSpec((1, d), lambda i: (0, 0)),
                  pl.BlockSpec((1, d), lambda i: (0, 0))],
        out_specs=pl.BlockSpec((tm, d), lambda i: (i, 0)),
        out_shape=jax.ShapeDtypeStruct((t, d), F32),
        compiler_params=_params(("parallel",)),
        name="shared_ln",
    )(x, routed_rows, wgu, wd, g.reshape(1, d), b.reshape(1, d))


def _moe_ln(x, p, g, b, tc):
    slot, wgt, off, cnt = _router(x, p["wr_hi"], p["wr_lo"], p["b_col"], tc)
    lines = lambda a: a.transpose(0, 2, 1).reshape(a.shape[0], tc * TOPK // LANES, LANES)
    routed = _routed_experts(x, lines(slot), lines(wgt), off[:, :, 0].reshape(-1), cnt[:, :, 0].reshape(-1),
                             p["wg"], p["wu"], p["wd"], p["layer"], tc)
    return _shared_ln(x, routed, p["wsgu"], p["wsd"], g, b, min(tc, PROMPT_TILE_ROWS))


def _cumsum_kernel(x_ref, o_ref):
    n = x_ref.shape[0]
    blk = LANES
    r = lax.broadcasted_iota(I32, (blk, blk), 0)
    c = lax.broadcasted_iota(I32, (blk, blk), 1)
    tri = (c <= r).astype(BF16)

    def body(i, carry):
        start = pl.multiple_of(i * blk, blk)
        hi, mid, lo = _split3(x_ref[pl.ds(start, blk), :])
        within = (_bdot(tri, lo) + _bdot(tri, mid)) + _bdot(tri, hi)
        o_ref[pl.ds(start, blk), :] = within + carry
        return carry + jnp.sum(x_ref[pl.ds(start, blk), :], axis=0, keepdims=True)

    lax.fori_loop(0, n // blk, body, jnp.zeros((1, x_ref.shape[1]), F32))


def _cumsum_rows(x, batch, seq):
    return pl.pallas_call(
        _cumsum_kernel,
        grid=(batch,),
        in_specs=[pl.BlockSpec((seq, x.shape[1]), lambda b: (b, 0))],
        out_specs=pl.BlockSpec((seq, x.shape[1]), lambda b: (b, 0)),
        out_shape=jax.ShapeDtypeStruct(x.shape, F32),
        compiler_params=_params(("parallel",)),
        name="logf_cumsum",
    )(x)


HEADS_PER_QBLOCK = LANES // FOX_DH
SUM_ROWS = 16


def _fox_prompt_kernel(q_ref, kt_ref, vt_ref, cumq_ref, cumk_ref, o_ref, *, tq):
    hp = pl.program_id(1)
    qi = pl.program_id(2)
    qf = q_ref[...].astype(F32) * (FOX_DH ** -0.5)
    lane = lax.broadcasted_iota(I32, cumq_ref.shape, 1)
    ones_rows = (lax.broadcasted_iota(I32, (SUM_ROWS, tq), 0) == 0).astype(F32)
    heads = range(HEADS_PER_QBLOCK)
    qs = [qf[:, e * FOX_DH:(e + 1) * FOX_DH].astype(BF16) for e in heads]
    cqs = [jnp.sum(jnp.where(lane == hp * HEADS_PER_QBLOCK + e, cumq_ref[...], 0.0), axis=1, keepdims=True)
           for e in heads]

    def block(kb, carry, diagonal):
        start = pl.multiple_of(kb * tq, tq)
        new = []
        for e in heads:
            m, acc = carry[e]
            rows = slice(e * FOX_DH, (e + 1) * FOX_DH)
            kt = kt_ref[rows, pl.ds(start, tq)].astype(BF16)
            vt = jnp.concatenate([vt_ref[rows, pl.ds(start, tq)], ones_rows], axis=0).astype(BF16)
            ck = cumk_ref[e:e + 1, pl.ds(start, tq)]
            s = _bdot(qs[e], kt) + cqs[e] - ck
            if diagonal:
                qpos = lax.broadcasted_iota(I32, (tq, 1), 0)
                kpos = lax.broadcasted_iota(I32, (1, tq), 1)
                s = jnp.where(kpos <= qpos, s, NEG_INF)
            m_new = jnp.maximum(m, jnp.max(s, axis=1, keepdims=True))
            a = jnp.exp(m - m_new)
            p = jnp.exp(s - m_new)
            new.append((m_new, a * acc + lax.dot_general(p.astype(BF16), vt, _NT, preferred_element_type=F32)))
        return tuple(new)

    init = tuple((jnp.full((tq, 1), -jnp.inf, F32), jnp.zeros((tq, FOX_DH + SUM_ROWS), F32)) for _ in heads)
    carry = lax.fori_loop(0, qi, lambda kb, c: block(kb, c, False), init)
    final = block(qi, carry, True)
    o_ref[...] = jnp.concatenate([acc[:, :FOX_DH] / acc[:, FOX_DH:FOX_DH + 1] for _, acc in final],
                                 axis=1).astype(o_ref.dtype)


def _fox_prompt(q, kt, vt, cum_tok, cum_row, batch, tq):
    t, fd = q.shape
    l = t // batch
    nq = l // tq
    n_pairs = fd // LANES
    kern = functools.partial(_fox_prompt_kernel, tq=tq)
    return pl.pallas_call(
        kern,
        grid=(batch, n_pairs, nq),
        in_specs=[pl.BlockSpec((tq, LANES), lambda bi, hp, qi: (bi * nq + qi, hp)),
                  pl.BlockSpec((None, LANES, l), lambda bi, hp, qi: (bi, hp, 0)),
                  pl.BlockSpec((None, LANES, l), lambda bi, hp, qi: (bi, hp, 0)),
                  pl.BlockSpec((tq, LANES), lambda bi, hp, qi: (bi * nq + qi, 0)),
                  pl.BlockSpec((None, None, HEADS_PER_QBLOCK, l), lambda bi, hp, qi: (bi, hp, 0, 0))],
        out_specs=pl.BlockSpec((tq, LANES), lambda bi, hp, qi: (bi * nq + qi, hp)),
        out_shape=jax.ShapeDtypeStruct((t, fd), BF16),
        compiler_params=_params(("parallel", "parallel", "arbitrary")),
        name="fox_prompt",
    )(q, kt, vt, cum_tok, cum_row)


N_QROWS = 128
PAGES_PER_STEP = 16
TAIL_ONES_ROW0 = 3 * FOX_HEADS


def _fox_sample_kernel(pt_ref, q_ref, lfo_ref, kto_ref, vto_ref, *rest, n_steps, dec_seq):
    del pt_ref
    cache_refs = rest[:3 * PAGES_PER_STEP]
    o_ref = rest[3 * PAGES_PER_STEP]
    qbd_ref, qtail_ref, m_ref, l_ref, acc_ref, carry_ref = rest[3 * PAGES_PER_STEP + 1:]
    step = pl.program_id(1)
    n_iota = lax.broadcasted_iota(I32, (N_QROWS, 1), 0)
    row_head = n_iota // dec_seq
    row_q = n_iota % dec_seq
    pos = lax.broadcasted_iota(I32, (1, PAGE_SIZE), 1)
    r = lax.broadcasted_iota(I32, (PAGE_SIZE, PAGE_SIZE), 0)
    c = lax.broadcasted_iota(I32, (PAGE_SIZE, PAGE_SIZE), 1)

    def lane_sums(x_t, sel):
        hi, mid, lo = _split3(x_t)
        return (_bdot(lo, sel) + _bdot(mid, sel)) + _bdot(hi, sel)

    def bias_rows(key_bias_t):
        hi, mid, lo = _split3(key_bias_t)
        ones_rows = (lax.broadcasted_iota(I32, (SUBLANES, PAGE_SIZE), 0) < 3).astype(F32)
        return jnp.concatenate(
            [hi.astype(F32), mid.astype(F32), lo.astype(F32), ones_rows,
             jnp.zeros((LANES - TAIL_ONES_ROW0 - SUBLANES, PAGE_SIZE), F32)], axis=0).astype(BF16)

    def attend(kts, vts, key_biases_t, valid):
        kt = jnp.concatenate([x.astype(BF16) for x in kts], axis=1)
        vt = jnp.concatenate([x.astype(BF16) for x in vts], axis=1)
        ktail = jnp.concatenate([bias_rows(x) for x in key_biases_t], axis=1)
        s = _bdot(qbd_ref[...], kt) + _bdot(qtail_ref[...], ktail)
        if valid is not None:
            s = jnp.where(valid, s, NEG_INF)
        m_old = m_ref[...]
        m_new = jnp.maximum(m_old, jnp.max(s, axis=1, keepdims=True))
        a = jnp.exp(m_old - m_new)
        p = jnp.exp(s - m_new)
        l_ref[...] = a * l_ref[...] + jnp.sum(p, axis=1, keepdims=True)
        acc_ref[...] = a * acc_ref[...] + lax.dot_general(
            p.astype(BF16), vt, _NT, preferred_element_type=F32)
        m_ref[...] = m_new

    @pl.when(step == 0)
    def _own_rows():
        q = q_ref[0] * (FOX_DH ** -0.5)
        qt = jnp.concatenate([q] * FOX_HEADS, axis=0)
        rr = lax.broadcasted_iota(I32, qt.shape, 0)
        cc = lax.broadcasted_iota(I32, qt.shape, 1)
        qbd_ref[...] = jnp.where(cc // FOX_DH == rr // dec_seq, qt, 0.0).astype(BF16)
        prefix_t = lane_sums(lfo_ref[0], (r <= c).astype(BF16))
        rep = (lax.broadcasted_iota(I32, (N_QROWS, FOX_HEADS), 1)
               == lax.broadcasted_iota(I32, (N_QROWS, FOX_HEADS), 0) // dec_seq).astype(BF16)
        hi, mid, lo = _split3(prefix_t)
        by_row = (_bdot(rep, lo) + _bdot(rep, mid)) + _bdot(rep, hi)
        pre_col = jnp.sum(jnp.where(pos == row_q, by_row, 0.0), axis=1, keepdims=True)
        hi, mid, lo = (x.astype(F32) for x in _split3(pre_col))
        lane = lax.broadcasted_iota(I32, (N_QROWS, LANES), 1)
        tail = jnp.where(jnp.logical_and(lane < TAIL_ONES_ROW0, lane % FOX_HEADS == row_head), 1.0, 0.0)
        tail = jnp.where(lane == TAIL_ONES_ROW0, hi, tail)
        tail = jnp.where(lane == TAIL_ONES_ROW0 + 1, mid, tail)
        tail = jnp.where(lane == TAIL_ONES_ROW0 + 2, lo, tail)
        qtail_ref[...] = tail.astype(BF16)
        m_ref[...] = jnp.full(m_ref.shape, -jnp.inf, F32)
        l_ref[...] = jnp.zeros(l_ref.shape, F32)
        acc_ref[...] = jnp.zeros(acc_ref.shape, F32)
        carry_ref[...] = jnp.zeros(carry_ref.shape, F32)
        valid = jnp.logical_and(pos <= row_q, pos < dec_seq)
        attend([kto_ref[0]], [vto_ref[0]], [-prefix_t], valid)

    @pl.when(step > 0)
    def _cache_pages():
        after = (r > c).astype(BF16)
        carry = carry_ref[...]
        kts, vts, biases = [], [], []
        for u in range(PAGES_PER_STEP):
            kt_ref, vt_ref, lf_ref = cache_refs[3 * u:3 * u + 3]
            lf_t = lf_ref[0]
            kts.append(kt_ref[0])
            vts.append(vt_ref[0])
            biases.append(lane_sums(lf_t, after) + carry)
            carry = carry + jnp.sum(lf_t, axis=1, keepdims=True)
        attend(kts, vts, biases, None)
        carry_ref[...] = carry

    @pl.when(step == n_steps - 1)
    def _finish():
        col_head = lax.broadcasted_iota(I32, (dec_seq, D_MODEL), 1) // FOX_DH
        out = jnp.zeros((dec_seq, D_MODEL), F32)
        for h in range(FOX_HEADS):
            rows = slice(h * dec_seq, (h + 1) * dec_seq)
            out = out + jnp.where(col_head == h, acc_ref[rows, :] / l_ref[rows, :], 0.0)
        o_ref[0] = out


def _fox_sample(q, lft_own, kt_own, vt_own, kt_cache, vt_cache, lft_cache, page_table):
    db, dec_seq, _ = q.shape
    n_pages = page_table.shape[1]
    n_steps = 1 + n_pages // PAGES_PER_STEP
    fd = FOX_HEADS * FOX_DH
    kern = functools.partial(_fox_sample_kernel, n_steps=n_steps, dec_seq=dec_seq)

    def page(u):
        return lambda b, s, pt: (pt[b, n_pages - 1 - (jnp.maximum(s, 1) - 1) * PAGES_PER_STEP - u], 0, 0)

    cache_specs = []
    cache_args = []
    for u in range(PAGES_PER_STEP):
        cache_specs += [pl.BlockSpec((1, fd, PAGE_SIZE), page(u)),
                        pl.BlockSpec((1, fd, PAGE_SIZE), page(u)),
                        pl.BlockSpec((1, FOX_HEADS, PAGE_SIZE), page(u))]
        cache_args += [kt_cache, vt_cache, lft_cache]
    grid_spec = pltpu.PrefetchScalarGridSpec(
        num_scalar_prefetch=1,
        grid=(db, n_steps),
        in_specs=[pl.BlockSpec((1, dec_seq, D_MODEL), lambda b, s, pt: (b, 0, 0)),
                  pl.BlockSpec((1, FOX_HEADS, PAGE_SIZE), lambda b, s, pt: (b, 0, 0)),
                  pl.BlockSpec((1, fd, PAGE_SIZE), lambda b, s, pt: (b, 0, 0)),
                  pl.BlockSpec((1, fd, PAGE_SIZE), lambda b, s, pt: (b, 0, 0))] + cache_specs,
        out_specs=pl.BlockSpec((1, dec_seq, D_MODEL), lambda b, s, pt: (b, 0, 0)),
        scratch_shapes=[pltpu.VMEM((N_QROWS, fd), BF16),
                        pltpu.VMEM((N_QROWS, LANES), BF16),
                        pltpu.VMEM((N_QROWS, 1), F32),
                        pltpu.VMEM((N_QROWS, 1), F32),
                        pltpu.VMEM((N_QROWS, fd), F32),
                        pltpu.VMEM((FOX_HEADS, 1), F32)],
    )
    return pl.pallas_call(
        kern,
        grid_spec=grid_spec,
        out_shape=jax.ShapeDtypeStruct((db, dec_seq, D_MODEL), F32),
        compiler_params=_params(("parallel", "arbitrary")),
        name="fox_sample",
    )(page_table, q, lft_own, kt_own, vt_own, *cache_args)


def _moe_params(layer, w_router, b_router, w_gate, w_up, w_down, ws_gate, ws_up, ws_down):
    wr_t = w_router[layer].T
    wr_hi = wr_t.astype(BF16)
    wr_lo = (wr_t - wr_hi.astype(F32)).astype(BF16)
    return dict(
        wr_hi=wr_hi, wr_lo=wr_lo, b_col=b_router[layer].reshape(N_EXPERTS, 1),
        layer=layer, wg=w_gate, wu=w_up, wd=w_down,
        wsgu=jnp.concatenate([ws_gate[layer], ws_up[layer]], axis=1).astype(BF16),
        wsd=ws_down[layer].astype(BF16))


def kernel(x_prompt, x_sample, state_ret, cache_k, cache_v, cache_logf, page_table, ret_w_in, ret_w_out,
           fox_w_kvf, fox_b_f, fox_w_q, fox_w_out, ln_g, ln_b, moe_w_router, moe_b_router, moe_w_gate,
           moe_w_up, moe_w_down, moe_ws_gate, moe_ws_up, moe_ws_down):
    bp, lp, d = x_prompt.shape
    bs, ls, _ = x_sample.shape
    tp, ts = bp * lp, bs * ls
    assert fox_heads_rows(ls) == N_QROWS
    xp = x_prompt.reshape(tp, d)
    xs = x_sample.reshape(ts, d)
    hq = RET_HEADS * RET_DK
    hv = RET_HEADS * RET_DV
    fd = FOX_HEADS * FOX_DH
    tm_p, tn_p = PROMPT_TILE_ROWS, PROJ_TILE_COLS
    tc_p = MOE_TOKEN_TILE

    wg_all, wu_all, wd_all = moe_w_gate.astype(BF16), moe_w_up.astype(BF16), moe_w_down.astype(BF16)
    moe = [_moe_params(layer, moe_w_router, moe_b_router, wg_all, wu_all, wd_all,
                       moe_ws_gate, moe_ws_up, moe_ws_down) for layer in range(DEPTH)]

    w_in = ret_w_in[0].astype(BF16)
    w_out = ret_w_out[0].astype(BF16)
    cos_p, sin_p = _rope_tables(jnp.arange(lp))
    cos_s, sin_s = _rope_tables(PAST_LEN + jnp.arange(ls))
    cos_s, sin_s = jnp.tile(cos_s, (bs, 1)), jnp.tile(sin_s, (bs, 1))

    qk_p = _qk_rope(xp, w_in, cos_p, sin_p, tm_p)
    v_p = _matmul(xp, w_in, 2 * hq, hv, BF16, tm_p, tn_p)
    g_p = _matmul(xp, w_in, 2 * hq + hv, hv, F32, tm_p, tn_p)
    o_p, st_p = _retention(qk_p, v_p, g_p, jnp.zeros((bp, RET_HEADS, RET_DK, RET_DV), F32),
                           _retention_decays(min(RET_CHUNK, lp)), bp, lp, BF16)
    xp = _mm_res_ln(o_p, w_out, xp, ln_g[0, 0], ln_b[0, 0], tm_p)

    qk_s = _qk_rope(xs, w_in, cos_s, sin_s, ts)
    v_s = _matmul(xs, w_in, 2 * hq, hv, F32, ts, tn_p)
    g_s = _matmul(xs, w_in, 2 * hq + hv, hv, F32, ts, tn_p)
    o_s, st_s = _retention(qk_s, v_s, g_s, state_ret[0], _retention_decays(min(RET_CHUNK, ls)), bs, ls, F32)
    xs = _mm_res_ln(o_s, w_out, xs, ln_g[0, 0], ln_b[0, 0], ts)

    xp = _moe_ln(xp, moe[0], ln_g[0, 1], ln_b[0, 1], tc_p)
    xs = _moe_ln(xs, moe[0], ln_g[0, 1], ln_b[0, 1], ts)

    w_kvf = fox_w_kvf.astype(BF16)
    w_kv_t = w_kvf[:, :2 * fd].T
    w_f_pad = jnp.zeros((d, LANES), BF16).at[:, :FOX_HEADS].set(w_kvf[:, 2 * fd:])
    b_f_pad = jnp.zeros((1, LANES), F32).at[0, :FOX_HEADS].set(fox_b_f)
    kt_p, vt_p = _matmul_t_pair(xp, w_kv_t, bp, tm_p)
    lf_p = _logf_proj(xp, w_f_pad, b_f_pad, tm_p)
    per_seq = lambda a_t: a_t[0].reshape(fd, bs, ls).transpose(1, 0, 2)
    kt_s, vt_s = (per_seq(a) for a in _matmul_t_pair(xs, w_kv_t, 1, ts))
    lf_s = _logf_proj(xs, w_f_pad, b_f_pad, ts)

    as_blhd = lambda a_t, b, l: a_t.reshape(b, FOX_HEADS, FOX_DH, l).transpose(0, 3, 1, 2)
    k_p, v_p4 = as_blhd(kt_p, bp, lp), as_blhd(vt_p, bp, lp)
    k_s, v_s4 = as_blhd(kt_s, bs, ls), as_blhd(vt_s, bs, ls)

    w_q = fox_w_q[0].astype(BF16)
    w_o = fox_w_out[0].astype(BF16)
    q_p = _matmul(xp, w_q, 0, fd, BF16, tm_p, tn_p)
    cum_p = _cumsum_rows(lf_p, bp, lp)
    cum_row = cum_p[:, :FOX_HEADS].reshape(bp, lp, FOX_HEADS).transpose(0, 2, 1).reshape(
        bp, FOX_HEADS // HEADS_PER_QBLOCK, HEADS_PER_QBLOCK, lp)
    att_p = _fox_prompt(q_p, kt_p, vt_p, cum_p, cum_row, bp, ATTN_TILE)
    xp = _mm_res_ln(att_p, w_o, xp, ln_g[1, 0], ln_b[1, 0], tm_p)

    q_s = _matmul(xs, w_q, 0, fd, F32, ts, tn_p)
    n_pool = cache_k.shape[0]
    kt_cache = cache_k.transpose(0, 2, 3, 1).reshape(n_pool, fd, PAGE_SIZE)
    vt_cache = cache_v.transpose(0, 2, 3, 1).reshape(n_pool, fd, PAGE_SIZE)
    lft_cache = cache_logf.transpose(0, 2, 1)
    pad_pos = lambda a_t: jnp.pad(a_t, ((0, 0), (0, 0), (0, PAGE_SIZE - ls)))
    lft_own = pad_pos(lf_s[:, :FOX_HEADS].reshape(bs, ls, FOX_HEADS).transpose(0, 2, 1))
    att_s = _fox_sample(q_s.reshape(bs, ls, d), lft_own, pad_pos(kt_s), pad_pos(vt_s),
                        kt_cache, vt_cache, lft_cache, page_table)
    xs = _mm_res_ln(att_s.reshape(ts, fd), w_o, xs, ln_g[1, 0], ln_b[1, 0], ts)

    xp = _moe_ln(xp, moe[1], ln_g[1, 1], ln_b[1, 1], tc_p)
    xs = _moe_ln(xs, moe[1], ln_g[1, 1], ln_b[1, 1], ts)

    return (xp.reshape(bp, lp, d), xs.reshape(bs, ls, d), st_p[None], st_s[None],
            k_p, v_p4, lf_p[:, :FOX_HEADS].reshape(bp, lp, FOX_HEADS),
            k_s, v_s4, lf_s[:, :FOX_HEADS].reshape(bs, ls, FOX_HEADS))


def fox_heads_rows(dec_seq):
    return FOX_HEADS * dec_seq
```
